```python
import math
import jax, jax.numpy as jnp
from jax import lax
import numpy as np

D_MODEL = 1024
BATCH = 8
SEQ = 2048
DEPTH = 1

MEM_LEN = 256
RMS_EPS = 1e-6
NEG = -1e30

ATTN_HEADS = 8
ATTN_HEAD_DIM = 64
ATTN_WIDTH = ATTN_HEADS * ATTN_HEAD_DIM
MOBA_BLOCK = 256
MOBA_TOPK = 3
MOBA_Q_CHUNK = 64

REL_BUCKETS = 32
REL_MAX_DIST = 1024

CONV_WIDTH = 512
CONV_K = 3

XATTN_HEADS = 4
XATTN_HEAD_DIM = 128
XATTN_WIDTH = XATTN_HEADS * XATTN_HEAD_DIM

N_BRANCH = 3
PROJ_SPLITS = [ATTN_WIDTH, ATTN_WIDTH, ATTN_WIDTH, CONV_WIDTH, CONV_WIDTH, CONV_WIDTH, XATTN_WIDTH, N_BRANCH * D_MODEL]
PROJ_WIDTH = sum(PROJ_SPLITS)
PROJ_OFFSETS = [int(o) for o in np.cumsum(PROJ_SPLITS)[:-1]]

PEER_HEADS = 8
PEER_NKEYS = 128
PEER_EXPERTS = PEER_NKEYS * PEER_NKEYS
PEER_DHALF = 128
PEER_TOPK = 16
PEER_TOKEN_CHUNK = 128

kernel_name = "hybrid_moba_conv_xattn_peer"


def rms_norm(x, g):
    xf = x.astype(jnp.float32)
    y = xf * lax.rsqrt(jnp.mean(xf * xf, axis=-1, keepdims=True) + RMS_EPS)
    return (y * g.astype(jnp.float32)).astype(x.dtype)


def rel_bucket(dist):
    max_exact = REL_BUCKETS // 2
    d = jnp.maximum(dist, 0)
    df = jnp.maximum(d, 1).astype(jnp.float32)
    large = max_exact + (jnp.log(df / max_exact) / math.log(REL_MAX_DIST / max_exact)
                         * (REL_BUCKETS - max_exact)).astype(jnp.int32)
    large = jnp.minimum(large, REL_BUCKETS - 1)
    return jnp.where(d < max_exact, d, large)


def split_heads(t, n_heads):
    b, s, w = t.shape
    return t.reshape(b, s, n_heads, w // n_heads).transpose(0, 2, 1, 3)


def merge_heads(t):
    b, h, s, dh = t.shape
    return t.transpose(0, 2, 1, 3).reshape(b, s, h * dh)


def moba_attention(q, k, v, rel_bias):
    B, H, S, dh = q.shape
    L = MOBA_BLOCK
    nb = -(-S // L)
    pad = nb * L - S
    kp = jnp.pad(k, ((0, 0), (0, 0), (0, pad), (0, 0)))
    vp = jnp.pad(v, ((0, 0), (0, 0), (0, pad), (0, 0)))
    kb = kp.reshape(B, H, nb, L, dh)
    vb = vp.reshape(B, H, nb, L, dh)
    k_mean = jnp.mean(kb.astype(jnp.float32), axis=3)
    q_blk = jnp.arange(S) // L
    gate = jnp.einsum('bhsd,bhnd->bhsn', q.astype(jnp.float32), k_mean)
    past = jnp.arange(nb)[None, :] < q_blk[:, None]
    gate = jnp.where(past, gate, NEG)
    n_sel = min(MOBA_TOPK, nb)
    _, sel = lax.top_k(gate, n_sel)
    sel_valid = sel < q_blk[:, None]
    scale = dh ** -0.5
    rel_t = rel_bias.T
    b_ix = jnp.arange(B)[:, None, None, None]
    h_ix = jnp.arange(H)[None, :, None, None]
    Qc = MOBA_Q_CHUNK
    n_chunks = S // Qc

    def chunk(c):
        start = c * Qc
        qc = lax.dynamic_slice_in_dim(q, start, Qc, axis=2)
        sc = lax.dynamic_slice_in_dim(sel, start, Qc, axis=2)
        vc = lax.dynamic_slice_in_dim(sel_valid, start, Qc, axis=2)
        qpos = start + jnp.arange(Qc)
        blk = start // L
        kg = kb[b_ix, h_ix, sc]
        vg = vb[b_ix, h_ix, sc]
        kpos_g = sc[..., None] * L + jnp.arange(L)
        bias_g = rel_t[h_ix[..., None], rel_bucket(qpos[:, None, None] - kpos_g)]
        lg = jnp.einsum('bhqd,bhqnld->bhqnl', qc, kg).astype(jnp.float32) * scale + bias_g
        lg = jnp.where(vc[..., None], lg, NEG)
        ko = lax.dynamic_slice_in_dim(kp, blk * L, L, axis=2)
        vo = lax.dynamic_slice_in_dim(vp, blk * L, L, axis=2)
        dist_o = qpos[:, None] - (blk * L + jnp.arange(L))[None, :]
        bias_o = rel_t[:, rel_bucket(dist_o)]
        lo = jnp.einsum('bhqd,bhld->bhql', qc, ko).astype(jnp.float32) * scale + bias_o
        lo = jnp.where(dist_o >= 0, lo, NEG)
        logits = jnp.concatenate([lg.reshape(B, H, Qc, n_sel * L), lo], axis=-1)
        p = jax.nn.softmax(logits, axis=-1)
        pg = p[..., :n_sel * L].reshape(B, H, Qc, n_sel, L).astype(v.dtype)
        po = p[..., n_sel * L:].astype(v.dtype)
        return (jnp.einsum('bhqnl,bhqnld->bhqd', pg, vg)
                + jnp.einsum('bhql,bhld->bhqd', po, vo))

    outs = lax.map(chunk, jnp.arange(n_chunks))
    return outs.transpose(1, 2, 0, 3, 4).reshape(B, H, S, dh)


def short_conv(u, w):
    C = u.shape[-1]
    return lax.conv_general_dilated(
        u, w[:, None, :], window_strides=(1,), padding=((CONV_K - 1, 0),),
        dimension_numbers=('NWC', 'WIO', 'NWC'), feature_group_count=C)


def cross_attention(q, k, v):
    scale = q.shape[-1] ** -0.5
    logits = jnp.einsum('bhsd,bhmd->bhsm', q, k).astype(jnp.float32) * scale
    p = jax.nn.softmax(logits, axis=-1)
    return jnp.einsum('bhsm,bhmd->bhsd', p.astype(v.dtype), v)


def peer(x, w_query, sub_keys, expert_u, expert_v):
    B, S, D = x.shape
    T = B * S
    xt = x.reshape(T, D)
    CH = PEER_TOKEN_CHUNK
    K = PEER_TOPK

    def chunk(c):
        xc = lax.dynamic_slice_in_dim(xt, c * CH, CH, axis=0)
        qry = (xc @ w_query).reshape(CH, PEER_HEADS, 2, PEER_DHALF)
        s = jnp.einsum('thpd,hpnd->thpn', qry, sub_keys).astype(jnp.float32)
        s_top, i_top = lax.top_k(s, K)
        cand = s_top[:, :, 0, :, None] + s_top[:, :, 1, None, :]
        cand_idx = i_top[:, :, 0, :, None] * PEER_NKEYS + i_top[:, :, 1, None, :]
        best_s, best_pos = lax.top_k(cand.reshape(CH, PEER_HEADS, K * K), K)
        e_idx = jnp.take_along_axis(cand_idx.reshape(CH, PEER_HEADS, K * K), best_pos, axis=-1)
        g = jax.nn.softmax(best_s, axis=-1)
        u = expert_u[e_idx]
        vv = expert_v[e_idx]
        act = jax.nn.gelu(jnp.einsum('td,thkd->thk', xc, u).astype(jnp.float32), approximate=False)
        return jnp.einsum('thk,thkd->td', (g * act).astype(x.dtype), vv)

    out = lax.map(chunk, jnp.arange(T // CH))
    return out.reshape(B, S, D)


def setup_inputs(seed: int = 0) -> dict:
    key = jax.random.key(seed)
    ks = jax.random.split(key, 24)
    f32 = jnp.float32
    D = D_MODEL

    def nrm(k, shape, scale):
        return jax.random.normal(k, shape, f32) * scale

    def gain(k, shape):
        return 1.0 + 0.1 * jax.random.normal(k, shape, f32)

    return {
        "x": nrm(ks[0], (BATCH, SEQ, D), 1.0),
        "mem": nrm(ks[1], (BATCH, MEM_LEN, D), 1.0),
        "g_mix": gain(ks[2], (DEPTH, D)),
        "w_in": nrm(ks[3], (DEPTH, D, PROJ_WIDTH), D ** -0.5),
        "q_gain_a": gain(ks[4], (DEPTH, ATTN_HEAD_DIM)),
        "k_gain_a": gain(ks[5], (DEPTH, ATTN_HEAD_DIM)),
        "rel_bias": nrm(ks[6], (REL_BUCKETS, ATTN_HEADS), 0.3),
        "conv_w": nrm(ks[7], (DEPTH, CONV_K, CONV_WIDTH), CONV_K ** -0.5),
        "g_mem": gain(ks[8], (DEPTH, D)),
        "w_mem_kv": nrm(ks[9], (DEPTH, D, 2 * XATTN_WIDTH), D ** -0.5),
        "q_gain_x": gain(ks[10], (DEPTH, XATTN_HEAD_DIM)),
        "k_gain_x": gain(ks[11], (DEPTH, XATTN_HEAD_DIM)),
        "w_br_attn": nrm(ks[12], (DEPTH, ATTN_WIDTH, D), ATTN_WIDTH ** -0.5),
        "w_br_conv": nrm(ks[13], (DEPTH, CONV_WIDTH, D), CONV_WIDTH ** -0.5),
        "w_br_x": nrm(ks[14], (DEPTH, XATTN_WIDTH, D), XATTN_WIDTH ** -0.5),
        "w_out": nrm(ks[15], (DEPTH, D, D), D ** -0.5),
        "g_ffn": gain(ks[16], (DEPTH, D)),
        "peer_w_query": nrm(ks[17], (DEPTH, D, PEER_HEADS * 2 * PEER_DHALF), D ** -0.5),
        "peer_sub_keys": nrm(ks[18], (DEPTH, PEER_HEADS, 2, PEER_NKEYS, PEER_DHALF), PEER_DHALF ** -0.5),
        "peer_u": nrm(ks[19], (DEPTH, PEER_EXPERTS, D), D ** -0.5),
        "peer_v": nrm(ks[20], (DEPTH, PEER_EXPERTS, D), 0.5),
    }


def reference(x, mem, g_mix, w_in, q_gain_a, k_gain_a, rel_bias, conv_w, g_mem, w_mem_kv,
              q_gain_x, k_gain_x, w_br_attn, w_br_conv, w_br_x, w_out, g_ffn,
              peer_w_query, peer_sub_keys, peer_u, peer_v):
    B, S, D = x.shape
    h = x
    for l in range(DEPTH):
        hn = rms_norm(h, g_mix[l])
        proj = hn @ w_in[l]
        q_a, k_a, v_a, u_c, b_c, c_c, q_x, gates = jnp.split(proj, PROJ_OFFSETS, axis=-1)

        qa = rms_norm(split_heads(q_a, ATTN_HEADS), q_gain_a[l])
        ka = rms_norm(split_heads(k_a, ATTN_HEADS), k_gain_a[l])
        va = split_heads(v_a, ATTN_HEADS)
        y_a = merge_heads(moba_attention(qa, ka, va, rel_bias)) @ w_br_attn[l]

        y_c = (b_c * short_conv(c_c * u_c, conv_w[l])) @ w_br_conv[l]

        kv_m = rms_norm(mem, g_mem[l]) @ w_mem_kv[l]
        k_m, v_m = jnp.split(kv_m, 2, axis=-1)
        qx = rms_norm(split_heads(q_x, XATTN_HEADS), q_gain_x[l])
        kx = rms_norm(split_heads(k_m, XATTN_HEADS), k_gain_x[l])
        vx = split_heads(v_m, XATTN_HEADS)
        y_x = merge_heads(cross_attention(qx, kx, vx)) @ w_br_x[l]

        g = jax.nn.sigmoid(gates.reshape(B, S, N_BRANCH, D))
        merged = g[:, :, 0] * y_a + g[:, :, 1] * y_c + g[:, :, 2] * y_x
        h = h + merged @ w_out[l]

        h = h + peer(rms_norm(h, g_ffn[l]), peer_w_query[l], peer_sub_keys[l], peer_u[l], peer_v[l])
    return h
```

```python
import functools
import math

import numpy as np
import jax
import jax.numpy as jnp
from jax import lax
from jax.experimental import pallas as pl
from jax.experimental.pallas import tpu as pltpu

F32 = jnp.float32
BF16 = jnp.bfloat16

RMS_EPS = 1e-6
NEG = -1e30

MOBA_BLOCK = 256
MOBA_TOPK = 3
REL_MAX_DIST = 1024
PEER_TOPK = 16

SUBLANES = 8
LANES = 128
VMEM_LIMIT = 56 * 1024 * 1024

_NT = (((1,), (1,)), ((), ()))


def _rms(x, g):
    r = lax.rsqrt(jnp.mean(x * x, axis=-1, keepdims=True) + RMS_EPS)
    return x * r * g


def _norm_matmul_body(x_ref, g_ref, w_ref, o_ref, xn_ref):
    @pl.when(pl.program_id(1) == 0)
    def _():
        xn_ref[...] = _rms(x_ref[...], g_ref[...]).astype(BF16)

    o_ref[...] = jnp.dot(xn_ref[...], w_ref[...], preferred_element_type=F32)


def _norm_matmul(x2d, g, w, tm, tn, name):
    t, d = x2d.shape
    n = w.shape[1]
    return pl.pallas_call(
        _norm_matmul_body,
        out_shape=jax.ShapeDtypeStruct((t, n), F32),
        grid=(t // tm, n // tn),
        in_specs=[
            pl.BlockSpec((tm, d), lambda i, j: (i, 0)),
            pl.BlockSpec((1, d), lambda i, j: (0, 0)),
            pl.BlockSpec((d, tn), lambda i, j: (0, j)),
        ],
        out_specs=pl.BlockSpec((tm, tn), lambda i, j: (i, j)),
        scratch_shapes=[pltpu.VMEM((tm, d), BF16)],
        compiler_params=pltpu.CompilerParams(
            dimension_semantics=("parallel", "arbitrary"), vmem_limit_bytes=VMEM_LIMIT),
        name=name,
    )(x2d, g.reshape(1, d), w)


def _rel_bucket(dist, n_buckets):
    max_exact = n_buckets // 2
    d = jnp.maximum(dist, 0)
    df = jnp.maximum(d, 1).astype(F32)
    large = max_exact + (jnp.log(df / max_exact) / math.log(REL_MAX_DIST / max_exact)
                         * (n_buckets - max_exact)).astype(jnp.int32)
    large = jnp.minimum(large, n_buckets - 1)
    return jnp.where(d < max_exact, d, large)


def _bias_tiles_body(bucket_ref, rel_ref, o_ref, *, n_buckets, n_delta):
    h = pl.program_id(0)
    blk = MOBA_BLOCK
    for delta in range(n_delta):
        bk = bucket_ref[delta:delta + 1, :]
        w = jnp.zeros(bk.shape, F32)
        for b in range(n_buckets):
            w = jnp.where(bk == b, rel_ref[b, h], w)
        x = jnp.broadcast_to(w, (blk, 2 * blk))
        r = pltpu.roll(x, blk, 1, stride=1, stride_axis=0)
        tile = r[:, :blk]
        if delta == 0:
            krow = lax.broadcasted_iota(jnp.int32, (blk, blk), 0)
            qcol = lax.broadcasted_iota(jnp.int32, (blk, blk), 1)
            tile = jnp.where(qcol >= krow, tile, NEG)
        o_ref[0, delta] = tile


def _bias_tiles(rel_bias, n_delta):
    n_buckets, n_heads = rel_bias.shape
    blk = MOBA_BLOCK
    dist = (jnp.arange(n_delta, dtype=jnp.int32)[:, None] * blk - blk
            + jnp.arange(2 * blk, dtype=jnp.int32)[None, :])
    bucket = _rel_bucket(dist, n_buckets).astype(jnp.int32)
    return pl.pallas_call(
        functools.partial(_bias_tiles_body, n_buckets=n_buckets, n_delta=n_delta),
        out_shape=jax.ShapeDtypeStruct((n_heads, n_delta, blk, blk), F32),
        grid=(n_heads,),
        in_specs=[
            pl.BlockSpec((n_delta, 2 * blk), lambda h: (0, 0)),
            pl.BlockSpec(memory_space=pltpu.SMEM),
        ],
        out_specs=pl.BlockSpec((1, n_delta, blk, blk), lambda h: (h, 0, 0, 0)),
        compiler_params=pltpu.CompilerParams(dimension_semantics=("arbitrary",)),
        name="rel_bias_tiles",
    )(bucket, rel_bias)


def _moba_body(q_ref, k_ref, v_ref, qg_ref, kg_ref, bias_ref, o_ref, ot_ref, *, dh):
    s_len = q_ref.shape[1]
    blk = MOBA_BLOCK
    nb = s_len // blk
    scale = dh ** -0.5
    heads_per_step = q_ref.shape[2] // dh
    vt_all = v_ref[0].T
    for hh in range(heads_per_step):
        sl = slice(hh * dh, (hh + 1) * dh)
        qn = _rms(q_ref[0, :, sl], qg_ref[...])
        kn = _rms(k_ref[0, :, sl], kg_ref[...])
        kmean = jnp.mean(kn.reshape(nb, blk, dh), axis=1)
        qb16 = qn.astype(BF16)
        kb16 = kn.astype(BF16)
        vt = vt_all[sl, :].astype(BF16)
        gate_t = lax.dot_general(kmean.astype(BF16), qb16, _NT, preferred_element_type=F32)
        row = lax.broadcasted_iota(jnp.int32, (nb, blk), 0)
        for qb in range(nb):
            qs = slice(qb * blk, (qb + 1) * blk)
            qt = qb16[qs]
            g = gate_t[:, qs]
            rank = jnp.zeros((nb, blk), jnp.int32)
            for j in range(qb):
                gj = g[j:j + 1, :]
                beats = jnp.where(gj > g, 1, jnp.where((gj == g) & (j < row), 1, 0))
                rank = rank + beats
            negm = jnp.where((rank < MOBA_TOPK) & (row < qb), 0.0, NEG)
            parts = []
            for kb in range(qb + 1):
                st = lax.dot_general(kb16[kb * blk:(kb + 1) * blk], qt, _NT,
                                     preferred_element_type=F32)
                st = st * scale + bias_ref[hh, qb - kb]
                if kb < qb:
                    st = st + negm[kb:kb + 1, :]
                parts.append(st)
            logit = jnp.concatenate(parts, axis=0) if len(parts) > 1 else parts[0]
            m = jnp.max(logit, axis=0, keepdims=True)
            p = jnp.exp(logit - m)
            denom = jnp.sum(p, axis=0, keepdims=True)
            nk = (qb + 1) * blk
            o_t = jnp.dot(vt[:, :nk], p.astype(BF16), preferred_element_type=F32)
            ot_ref[sl, qs] = o_t / denom
    o_ref[0] = ot_ref[...].T.astype(o_ref.dtype)


def _moba(proj3, q_gain, k_gain, bias_t, n_heads, dh):
    b, s_len, _ = proj3.shape
    hps = LANES // dh
    width = n_heads * dh
    nblk = width // LANES
    n_delta = bias_t.shape[1]
    blk = MOBA_BLOCK
    return pl.pallas_call(
        functools.partial(_moba_body, dh=dh),
        out_shape=jax.ShapeDtypeStruct((b, s_len, width), BF16),
        grid=(n_heads // hps, b),
        in_specs=[
            pl.BlockSpec((1, s_len, LANES), lambda hp, bi: (bi, 0, hp)),
            pl.BlockSpec((1, s_len, LANES), lambda hp, bi: (bi, 0, nblk + hp)),
            pl.BlockSpec((1, s_len, LANES), lambda hp, bi: (bi, 0, 2 * nblk + hp)),
            pl.BlockSpec((1, dh), lambda hp, bi: (0, 0)),
            pl.BlockSpec((1, dh), lambda hp, bi: (0, 0)),
            pl.BlockSpec((hps, n_delta, blk, blk), lambda hp, bi: (hp, 0, 0, 0)),
        ],
        out_specs=pl.BlockSpec((1, s_len, LANES), lambda hp, bi: (bi, 0, hp)),
        scratch_shapes=[pltpu.VMEM((LANES, s_len), F32)],
        compiler_params=pltpu.CompilerParams(
            dimension_semantics=("parallel", "arbitrary"), vmem_limit_bytes=VMEM_LIMIT),
        name="moba_attention",
    )(proj3, proj3, proj3, q_gain.reshape(1, dh), k_gain.reshape(1, dh), bias_t)


def _xattn_body(q_ref, kv_ref, qg_ref, kg_ref, o_ref, *, dh, chunk):
    s_len = q_ref.shape[1]
    width = q_ref.shape[2]
    n_heads = width // dh
    scale = dh ** -0.5
    for h in range(n_heads):
        sl = slice(h * dh, (h + 1) * dh)
        kn = _rms(kv_ref[0, :, sl], kg_ref[...]).astype(BF16)
        v = kv_ref[0, :, width + h * dh:width + (h + 1) * dh].astype(BF16)
        for c in range(s_len // chunk):
            rs = slice(c * chunk, (c + 1) * chunk)
            qn = _rms(q_ref[0, rs, sl], qg_ref[...]).astype(BF16)
            s = lax.dot_general(qn, kn, _NT, preferred_element_type=F32) * scale
            m = jnp.max(s, axis=-1, keepdims=True)
            p = jnp.exp(s - m)
            denom = jnp.sum(p, axis=-1, keepdims=True)
            o = jnp.dot(p.astype(BF16), v, preferred_element_type=F32) / denom
            o_ref[0, rs, sl] = o.astype(o_ref.dtype)


def _xattn(proj3, q_col_block, kv3, q_gain, k_gain, dh):
    b, s_len, _ = proj3.shape
    m_len, kv_width = kv3.shape[1], kv3.shape[2]
    width = kv_width // 2
    return pl.pallas_call(
        functools.partial(_xattn_body, dh=dh, chunk=512),
        out_shape=jax.ShapeDtypeStruct((b, s_len, width), BF16),
        grid=(b,),
        in_specs=[
            pl.BlockSpec((1, s_len, width), lambda bi: (bi, 0, q_col_block)),
            pl.BlockSpec((1, m_len, kv_width), lambda bi: (bi, 0, 0)),
            pl.BlockSpec((1, dh), lambda bi: (0, 0)),
            pl.BlockSpec((1, dh), lambda bi: (0, 0)),
        ],
        out_specs=pl.BlockSpec((1, s_len, width), lambda bi: (bi, 0, 0)),
        compiler_params=pltpu.CompilerParams(
            dimension_semantics=("parallel",), vmem_limit_bytes=VMEM_LIMIT),
        name="cross_attention",
    )(proj3, kv3, q_gain.reshape(1, dh), k_gain.reshape(1, dh))


def _merge_body(x_ref, ya_ref, u_ref, b_ref, c_ref, uh_ref, ch_ref, yx_ref, gt_ref, cw_ref,
                wa_ref, wc_ref, wx_ref, wo_ref, o_ref, *, tiles_per_seq):
    i = pl.program_id(0)
    tm, d = x_ref.shape
    z = c_ref[...] * u_ref[...]
    zh = ch_ref[...] * uh_ref[...]
    zh = jnp.where(i % tiles_per_seq == 0, 0.0, zh)
    row = lax.broadcasted_iota(jnp.int32, z.shape, 0)
    z1 = pltpu.roll(z, 1, 0)
    z1 = jnp.where(row == 0, zh[7:8, :], z1)
    z2 = pltpu.roll(z, 2, 0)
    z2 = jnp.where(row == 0, zh[6:7, :], jnp.where(row == 1, zh[7:8, :], z2))
    conv = cw_ref[0:1, :] * z2 + cw_ref[1:2, :] * z1 + cw_ref[2:3, :] * z
    y_c = jnp.dot((b_ref[...] * conv).astype(BF16), wc_ref[...], preferred_element_type=F32)
    y_a = jnp.dot(ya_ref[...], wa_ref[...], preferred_element_type=F32)
    y_x = jnp.dot(yx_ref[...], wx_ref[...], preferred_element_type=F32)
    merged = (jax.nn.sigmoid(gt_ref[:, 0:d]) * y_a
              + jax.nn.sigmoid(gt_ref[:, d:2 * d]) * y_c
              + jax.nn.sigmoid(gt_ref[:, 2 * d:3 * d]) * y_x)
    o_ref[...] = x_ref[...] + jnp.dot(merged.astype(BF16), wo_ref[...], preferred_element_type=F32)


def _merge(x2d, y_a, proj, conv_cols, y_x, gates, conv_w, w_a, w_c, w_x, w_o, s_len, tm):
    t, d = x2d.shape
    cwid = conv_w.shape[1]
    cu, cb, cc = conv_cols
    halo = SUBLANES
    rows_per_tile = tm // halo

    def halo_map(col):
        return lambda i: (jnp.maximum(i * rows_per_tile - 1, 0), col)

    full = lambda shape: pl.BlockSpec(shape, lambda i: (0, 0))
    return pl.pallas_call(
        functools.partial(_merge_body, tiles_per_seq=s_len // tm),
        out_shape=jax.ShapeDtypeStruct((t, d), F32),
        grid=(t // tm,),
        in_specs=[
            pl.BlockSpec((tm, d), lambda i: (i, 0)),
            pl.BlockSpec((tm, y_a.shape[1]), lambda i: (i, 0)),
            pl.BlockSpec((tm, cwid), lambda i: (i, cu)),
            pl.BlockSpec((tm, cwid), lambda i: (i, cb)),
            pl.BlockSpec((tm, cwid), lambda i: (i, cc)),
            pl.BlockSpec((halo, cwid), halo_map(cu)),
            pl.BlockSpec((halo, cwid), halo_map(cc)),
            pl.BlockSpec((tm, y_x.shape[1]), lambda i: (i, 0)),
            pl.BlockSpec((tm, gates.shape[1]), lambda i: (i, 0)),
            full(conv_w.shape), full(w_a.shape), full(w_c.shape), full(w_x.shape), full(w_o.shape),
        ],
        out_specs=pl.BlockSpec((tm, d), lambda i: (i, 0)),
        compiler_params=pltpu.CompilerParams(
            dimension_semantics=("parallel",), vmem_limit_bytes=VMEM_LIMIT),
        name="branch_merge",
    )(x2d, y_a, proj, proj, proj, proj, proj, y_x, gates, conv_w, w_a, w_c, w_x, w_o)


def _oddeven_merge_sort_pairs(n):
    pairs = []
    p = 1
    while p < n:
        k = p
        while k >= 1:
            for j in range(k % p, n - k, 2 * k):
                for i in range(min(k, n - j - k)):
                    if (i + j) // (2 * p) == (i + j + k) // (2 * p):
                        pairs.append((i + j, i + j + k))
            k //= 2
        p *= 2
    return pairs


def _bitonic_merge_pairs(n):
    pairs = []
    d = n // 2
    while d >= 1:
        pairs += [(i, i + d) for i in range(n) if i & d == 0]
        d //= 2
    return pairs


def _compare_exchange(slabs, pairs):
    for a, b in pairs:
        hi = jnp.maximum(slabs[a], slabs[b])
        lo = jnp.minimum(slabs[a], slabs[b])
        slabs[a], slabs[b] = hi, lo
    return slabs


def _top_sorted(slabs, k):
    n = len(slabs)
    slabs = _compare_exchange(list(slabs), _oddeven_merge_sort_pairs(n))
    merge = _bitonic_merge_pairs(k)
    for shift in (4, 2, 1):
        rolled = [pltpu.roll(s, shift, 0) for s in slabs]
        if len(slabs) < k:
            slabs = slabs + rolled[::-1]
        else:
            slabs = [jnp.maximum(slabs[i], rolled[k - 1 - i]) for i in range(k)]
        slabs = _compare_exchange(slabs, merge)
    return slabs


def _hyperbola(k):
    return [(i, j) for i in range(k) for j in range(k) if (i + 1) * (j + 1) <= k]


def _peer_scores(h, keys_ref, qry_ref, s2_ref, e1_ref, e2_ref, thr_ref, *, tt, lane_w):
    k = PEER_TOPK
    nkeys = keys_ref.shape[2]
    pairs = _hyperbola(k)
    n_slab = -(-len(pairs) // SUBLANES)
    n_slab = k // 2 if n_slab <= k // 2 else k
    sub = lax.broadcasted_iota(jnp.int32, (SUBLANES, lane_w), 0)
    inf = float("inf")
    for ts in range(tt // lane_w):
        cols = slice(ts * lane_w, (ts + 1) * lane_w)
        q1 = qry_ref[cols, pl.ds(pl.multiple_of(2 * h * nkeys, nkeys), nkeys)]
        q2 = qry_ref[cols, pl.ds(pl.multiple_of((2 * h + 1) * nkeys, nkeys), nkeys)]
        s1 = lax.dot_general(keys_ref[h, 0], q1, _NT, preferred_element_type=F32)
        s2 = lax.dot_general(keys_ref[h, 1], q2, _NT, preferred_element_type=F32)
        a = _top_sorted([s1[SUBLANES * v:SUBLANES * (v + 1)] for v in range(nkeys // SUBLANES)], k)
        b = _top_sorted([s2[SUBLANES * v:SUBLANES * (v + 1)] for v in range(nkeys // SUBLANES)], k)
        sums = {(i, j): a[i] + b[j] for (i, j) in pairs}
        packed = []
        for v in range(n_slab):
            slab = jnp.full((SUBLANES, lane_w), -inf, F32)
            for s in range(SUBLANES):
                idx = v * SUBLANES + s
                if idx < len(pairs):
                    slab = jnp.where(sub == s, sums[pairs[idx]], slab)
            packed.append(slab)
        tau = _top_sorted(packed, k)[k - 1]
        top = a[0] + b[0]
        z = jnp.zeros((SUBLANES, lane_w), F32)
        for slab in packed:
            z = z + jnp.where(slab >= tau, jnp.exp(slab - top), 0.0)
        zsum = jnp.sum(z, axis=0, keepdims=True)
        thr_rank = []
        for i in range(k):
            t = jnp.full((SUBLANES, lane_w), inf, F32)
            for j in range(k):
                if (i, j) in sums:
                    t = jnp.minimum(t, jnp.where(sums[(i, j)] >= tau, b[j], inf))
            thr_rank.append(t[0:1, :])
        thr = jnp.full((nkeys, lane_w), inf, F32)
        for i in range(k - 1, -1, -1):
            thr = jnp.where(s1 >= a[i][0:1, :], thr_rank[i], thr)
        thr_ref[h, :, cols] = thr
        s2_ref[h, :, cols] = s2
        e1_ref[h, :, cols] = jnp.exp(s1 - a[0][0:1, :])
        e2_ref[h, :, cols] = jnp.exp(s2 - b[0][0:1, :]) / zsum


def _peer_body(h_ref, g_ref, wq_ref, keys_ref, u_ref, vt_ref, o_ref,
               xn_ref, qry_ref, s2_ref, e1_ref, e2_ref, thr_ref, act_ref, p_ref, acc_ref,
               *, lane_w):
    c = pl.program_id(1)
    n_chunks = pl.num_programs(1)
    tt = h_ref.shape[0]
    n_heads, _, nkeys, _ = keys_ref.shape
    ec = u_ref.shape[0]

    @pl.when(c == 0)
    def _():
        xn = _rms(h_ref[...], g_ref[...]).astype(BF16)
        xn_ref[...] = xn
        qry_ref[...] = jnp.dot(xn, wq_ref[...], preferred_element_type=F32).astype(BF16)

        def head(h, carry):
            _peer_scores(h, keys_ref, qry_ref, s2_ref, e1_ref, e2_ref, thr_ref, tt=tt, lane_w=lane_w)
            return carry

        lax.fori_loop(0, n_heads, head, 0)
        acc_ref[...] = jnp.zeros(acc_ref.shape, F32)

    act_ref[...] = lax.dot_general(u_ref[...], xn_ref[...], _NT, preferred_element_type=F32)
    wide = 2 * lane_w
    for il in range(ec // nkeys):
        i = c * (ec // nkeys) + il
        rows = slice(il * nkeys, (il + 1) * nkeys)
        for ts in range(tt // wide):
            cols = slice(ts * wide, (ts + 1) * wide)
            w = jnp.zeros((nkeys, wide), F32)
            for h in range(n_heads):
                thr = thr_ref[h, pl.ds(i, 1), cols]
                e1 = e1_ref[h, pl.ds(i, 1), cols]
                w = w + jnp.where(s2_ref[h, :, cols] >= thr, e2_ref[h, :, cols], 0.0) * e1
            a = act_ref[rows, cols]
            gelu = 0.5 * a * (1.0 + lax.erf(a * (2.0 ** -0.5)))
            p_ref[rows, cols] = (w * gelu).astype(BF16)
    acc_ref[...] += jnp.dot(vt_ref[...], p_ref[...], preferred_element_type=F32)

    @pl.when(c == n_chunks - 1)
    def _():
        o_ref[...] = h_ref[...] + acc_ref[...].T


def _peer(h2d, g, w_query, keys, u, v_t, tt, ec):
    t, d = h2d.shape
    n_heads, _, nkeys, dk = keys.shape
    n_exp = u.shape[0]
    qw = w_query.shape[1]
    stat = lambda: pltpu.VMEM((n_heads, nkeys, tt), F32)
    return pl.pallas_call(
        functools.partial(_peer_body, lane_w=LANES),
        out_shape=jax.ShapeDtypeStruct((t, d), F32),
        grid=(t // tt, n_exp // ec),
        in_specs=[
            pl.BlockSpec((tt, d), lambda i, c: (i, 0)),
            pl.BlockSpec((1, d), lambda i, c: (0, 0)),
            pl.BlockSpec((d, qw), lambda i, c: (0, 0)),
            pl.BlockSpec(keys.shape, lambda i, c: (0, 0, 0, 0)),
            pl.BlockSpec((ec, d), lambda i, c: (c, 0)),
            pl.BlockSpec((d, ec), lambda i, c: (0, c)),
        ],
        out_specs=pl.BlockSpec((tt, d), lambda i, c: (i, 0)),
        scratch_shapes=[
            pltpu.VMEM((tt, d), BF16),
            pltpu.VMEM((tt, qw), BF16),
            stat(), stat(), stat(), stat(),
            pltpu.VMEM((ec, tt), F32),
            pltpu.VMEM((ec, tt), BF16),
            pltpu.VMEM((d, tt), F32),
        ],
        compiler_params=pltpu.CompilerParams(
            dimension_semantics=("parallel", "arbitrary"), vmem_limit_bytes=VMEM_LIMIT),
        name="peer",
    )(h2d, g.reshape(1, d), w_query, keys, u, v_t)


def _layer(h2d, mem2d, batch, g_mix, w_in, q_gain_a, k_gain_a, bias_t, conv_w, g_mem, w_mem_kv,
           q_gain_x, k_gain_x, w_br_attn, w_br_conv, w_br_x, w_out, g_ffn,
           peer_w_query, peer_sub_keys, peer_u, peer_v):
    t, d = h2d.shape
    s_len = t // batch
    n_heads_a = bias_t.shape[0]
    dh_a = q_gain_a.shape[0]
    attn_w = n_heads_a * dh_a
    conv_wid = conv_w.shape[1]
    dh_x = q_gain_x.shape[0]
    xattn_w = w_mem_kv.shape[1] // 2
    assert attn_w == conv_wid == xattn_w, "column blocks of the projection must have equal width"
    branch_cols = 3 * attn_w + 3 * conv_wid + xattn_w

    w_in16 = w_in.astype(BF16)
    proj = _norm_matmul(h2d, g_mix, w_in16[:, :branch_cols], 512, branch_cols // 2, "proj_branches")
    gates = _norm_matmul(h2d, g_mix, w_in16[:, branch_cols:], 512, d, "proj_gates")
    proj3 = proj.reshape(batch, s_len, branch_cols)

    y_a = _moba(proj3, q_gain_a, k_gain_a, bias_t, n_heads_a, dh_a).reshape(t, attn_w)

    kv = _norm_matmul(mem2d, g_mem, w_mem_kv.astype(BF16), 512, w_mem_kv.shape[1], "mem_kv")
    kv3 = kv.reshape(batch, mem2d.shape[0] // batch, kv.shape[1])
    y_x = _xattn(proj3, 6, kv3, q_gain_x, k_gain_x, dh_x).reshape(t, xattn_w)

    h2d = _merge(h2d, y_a, proj, (3, 4, 5), y_x, gates, conv_w,
                 w_br_attn.astype(BF16), w_br_conv.astype(BF16), w_br_x.astype(BF16),
                 w_out.astype(BF16), s_len, 512)

    return _peer(h2d, g_ffn, peer_w_query.astype(BF16), peer_sub_keys.astype(BF16),
                 peer_u.astype(BF16), peer_v.astype(BF16).T, 512, 512)


def kernel(x, mem, g_mix, w_in, q_gain_a, k_gain_a, rel_bias, conv_w, g_mem, w_mem_kv, q_gain_x, k_gain_x, w_br_attn, w_br_conv, w_br_x, w_out, g_ffn, peer_w_query, peer_sub_keys, peer_u, peer_v):
    batch, s_len, d = x.shape
    depth = g_mix.shape[0]
    bias_t = _bias_tiles(rel_bias, s_len // MOBA_BLOCK)
    h2d = x.reshape(batch * s_len, d)
    mem2d = mem.reshape(batch * mem.shape[1], d)
    for l in range(depth):
        h2d = _layer(h2d, mem2d, batch, g_mix[l], w_in[l], q_gain_a[l], k_gain_a[l], bias_t, conv_w[l],
                     g_mem[l], w_mem_kv[l], q_gain_x[l], k_gain_x[l], w_br_attn[l], w_br_conv[l],
                     w_br_x[l], w_out[l], g_ffn[l], peer_w_query[l], peer_sub_keys[l],
                     peer_u[l], peer_v[l])
    return h2d.reshape(batch, s_len, d)
```

```python
import functools
import math

import numpy as np
import jax
import jax.numpy as jnp
from jax import lax
from jax.experimental import pallas as pl
from jax.experimental.pallas import tpu as pltpu

F32 = jnp.float32
BF16 = jnp.bfloat16

RMS_EPS = 1e-6
NEG = -1e30

MOBA_BLOCK = 256
MOBA_TOPK = 3
REL_MAX_DIST = 1024
PEER_TOPK = 16

SUBLANES = 8
LANES = 128
VMEM_LIMIT = 56 * 1024 * 1024

_NT = (((1,), (1,)), ((), ()))


def _rms(x, g):
    r = lax.rsqrt(jnp.mean(x * x, axis=-1, keepdims=True) + RMS_EPS)
    return x * r * g


def _norm_matmul_body(x_ref, g_ref, w_ref, o_ref, xn_ref):
    @pl.when(pl.program_id(1) == 0)
    def _():
        xn_ref[...] = _rms(x_ref[...], g_ref[...]).astype(BF16)

    o_ref[...] = jnp.dot(xn_ref[...], w_ref[...], preferred_element_type=F32)


def _norm_matmul(x2d, g, w, tm, tn, name):
    t, d = x2d.shape
    n = w.shape[1]
    return pl.pallas_call(
        _norm_matmul_body,
        out_shape=jax.ShapeDtypeStruct((t, n), F32),
        grid=(t // tm, n // tn),
        in_specs=[
            pl.BlockSpec((tm, d), lambda i, j: (i, 0)),
            pl.BlockSpec((1, d), lambda i, j: (0, 0)),
            pl.BlockSpec((d, tn), lambda i, j: (0, j)),
        ],
        out_specs=pl.BlockSpec((tm, tn), lambda i, j: (i, j)),
        scratch_shapes=[pltpu.VMEM((tm, d), BF16)],
        compiler_params=pltpu.CompilerParams(
            dimension_semantics=("parallel", "arbitrary"), vmem_limit_bytes=VMEM_LIMIT),
        name=name,
    )(x2d, g.reshape(1, d), w)


def _rel_bucket(dist, n_buckets):
    max_exact = n_buckets // 2
    d = jnp.maximum(dist, 0)
    df = jnp.maximum(d, 1).astype(F32)
    large = max_exact + (jnp.log(df / max_exact) / math.log(REL_MAX_DIST / max_exact)
                         * (n_buckets - max_exact)).astype(jnp.int32)
    large = jnp.minimum(large, n_buckets - 1)
    return jnp.where(d < max_exact, d, large)


def _bias_tiles_body(bucket_ref, rel_ref, o_ref, *, n_buckets, n_delta):
    h = pl.program_id(0)
    blk = MOBA_BLOCK
    for delta in range(n_delta):
        bk = bucket_ref[delta:delta + 1, :]
        w = jnp.zeros(bk.shape, F32)
        for b in range(n_buckets):
            w = jnp.where(bk == b, rel_ref[b, h], w)
        x = jnp.broadcast_to(w, (blk, 2 * blk))
        r = pltpu.roll(x, blk, 1, stride=1, stride_axis=0)
        tile = r[:, :blk]
        if delta == 0:
            krow = lax.broadcasted_iota(jnp.int32, (blk, blk), 0)
            qcol = lax.broadcasted_iota(jnp.int32, (blk, blk), 1)
            tile = jnp.where(qcol >= krow, tile, NEG)
        o_ref[0, delta] = tile


def _bias_tiles(rel_bias, n_delta):
    n_buckets, n_heads = rel_bias.shape
    blk = MOBA_BLOCK
    dist = (jnp.arange(n_delta, dtype=jnp.int32)[:, None] * blk - blk
            + jnp.arange(2 * blk, dtype=jnp.int32)[None, :])
    bucket = _rel_bucket(dist, n_buckets).astype(jnp.int32)
    return pl.pallas_call(
        functools.partial(_bias_tiles_body, n_buckets=n_buckets, n_delta=n_delta),
        out_shape=jax.ShapeDtypeStruct((n_heads, n_delta, blk, blk), F32),
        grid=(n_heads,),
        in_specs=[
            pl.BlockSpec((n_delta, 2 * blk), lambda h: (0, 0)),
            pl.BlockSpec(memory_space=pltpu.SMEM),
        ],
        out_specs=pl.BlockSpec((1, n_delta, blk, blk), lambda h: (h, 0, 0, 0)),
        compiler_params=pltpu.CompilerParams(dimension_semantics=("arbitrary",)),
        name="rel_bias_tiles",
    )(bucket, rel_bias)


def _moba_body(q_ref, k_ref, v_ref, qg_ref, kg_ref, bias_ref, o_ref, ot_ref, *, dh):
    s_len = q_ref.shape[1]
    blk = MOBA_BLOCK
    nb = s_len // blk
    scale = dh ** -0.5
    heads_per_step = q_ref.shape[2] // dh
    vt_all = v_ref[0].T
    for hh in range(heads_per_step):
        sl = slice(hh * dh, (hh + 1) * dh)
        qn = _rms(q_ref[0, :, sl], qg_ref[...])
        kn = _rms(k_ref[0, :, sl], kg_ref[...])
        kmean = jnp.mean(kn.reshape(nb, blk, dh), axis=1)
        qb16 = qn.astype(BF16)
        kb16 = kn.astype(BF16)
        vt = vt_all[sl, :].astype(BF16)
        gate_t = lax.dot_general(kmean.astype(BF16), qb16, _NT, preferred_element_type=F32)
        row = lax.broadcasted_iota(jnp.int32, (nb, blk), 0)
        for qb in range(nb):
            qs = slice(qb * blk, (qb + 1) * blk)
            qt = qb16[qs]
            g = gate_t[:, qs]
            rank = jnp.zeros((nb, blk), jnp.int32)
            for j in range(qb):
                gj = g[j:j + 1, :]
                beats = jnp.where(gj > g, 1, jnp.where((gj == g) & (j < row), 1, 0))
                rank = rank + beats
            negm = jnp.where((rank < MOBA_TOPK) & (row < qb), 0.0, NEG)
            parts = []
            for kb in range(qb + 1):
                st = lax.dot_general(kb16[kb * blk:(kb + 1) * blk], qt, _NT,
                                     preferred_element_type=F32)
                st = st * scale + bias_ref[hh, qb - kb]
                if kb < qb:
                    st = st + negm[kb:kb + 1, :]
                parts.append(st)
            logit = jnp.concatenate(parts, axis=0) if len(parts) > 1 else parts[0]
            m = jnp.max(logit, axis=0, keepdims=True)
            p = jnp.exp(logit - m)
            denom = jnp.sum(p, axis=0, keepdims=True)
            nk = (qb + 1) * blk
            o_t = jnp.dot(vt[:, :nk], p.astype(BF16), preferred_element_type=F32)
            ot_ref[sl, qs] = o_t / denom
    o_ref[0] = ot_ref[...].T.astype(o_ref.dtype)


def _moba(proj3, q_gain, k_gain, bias_t, n_heads, dh):
    b, s_len, _ = proj3.shape
    hps = LANES // dh
    width = n_heads * dh
    nblk = width // LANES
    n_delta = bias_t.shape[1]
    blk = MOBA_BLOCK
    return pl.pallas_call(
        functools.partial(_moba_body, dh=dh),
        out_shape=jax.ShapeDtypeStruct((b, s_len, width), BF16),
        grid=(n_heads // hps, b),
        in_specs=[
            pl.BlockSpec((1, s_len, LANES), lambda hp, bi: (bi, 0, hp)),
            pl.BlockSpec((1, s_len, LANES), lambda hp, bi: (bi, 0, nblk + hp)),
            pl.BlockSpec((1, s_len, LANES), lambda hp, bi: (bi, 0, 2 * nblk + hp)),
            pl.BlockSpec((1, dh), lambda hp, bi: (0, 0)),
            pl.BlockSpec((1, dh), lambda hp, bi: (0, 0)),
            pl.BlockSpec((hps, n_delta, blk, blk), lambda hp, bi: (hp, 0, 0, 0)),
        ],
        out_specs=pl.BlockSpec((1, s_len, LANES), lambda hp, bi: (bi, 0, hp)),
        scratch_shapes=[pltpu.VMEM((LANES, s_len), F32)],
        compiler_params=pltpu.CompilerParams(
            dimension_semantics=("parallel", "arbitrary"), vmem_limit_bytes=VMEM_LIMIT),
        name="moba_attention",
    )(proj3, proj3, proj3, q_gain.reshape(1, dh), k_gain.reshape(1, dh), bias_t)


def _xattn_body(q_ref, kv_ref, qg_ref, kg_ref, o_ref, *, dh, chunk):
    s_len = q_ref.shape[1]
    width = q_ref.shape[2]
    n_heads = width // dh
    scale = dh ** -0.5
    for h in range(n_heads):
        sl = slice(h * dh, (h + 1) * dh)
        kn = _rms(kv_ref[0, :, sl], kg_ref[...]).astype(BF16)
        v = kv_ref[0, :, width + h * dh:width + (h + 1) * dh].astype(BF16)
        for c in range(s_len // chunk):
            rs = slice(c * chunk, (c + 1) * chunk)
            qn = _rms(q_ref[0, rs, sl], qg_ref[...]).astype(BF16)
            s = lax.dot_general(qn, kn, _NT, preferred_element_type=F32) * scale
            m = jnp.max(s, axis=-1, keepdims=True)
            p = jnp.exp(s - m)
            denom = jnp.sum(p, axis=-1, keepdims=True)
            o = jnp.dot(p.astype(BF16), v, preferred_element_type=F32) / denom
            o_ref[0, rs, sl] = o.astype(o_ref.dtype)


def _xattn(proj3, q_col_block, kv3, q_gain, k_gain, dh):
    b, s_len, _ = proj3.shape
    m_len, kv_width = kv3.shape[1], kv3.shape[2]
    width = kv_width // 2
    return pl.pallas_call(
        functools.partial(_xattn_body, dh=dh, chunk=512),
        out_shape=jax.ShapeDtypeStruct((b, s_len, width), BF16),
        grid=(b,),
        in_specs=[
            pl.BlockSpec((1, s_len, width), lambda bi: (bi, 0, q_col_block)),
            pl.BlockSpec((1, m_len, kv_width), lambda bi: (bi, 0, 0)),
            pl.BlockSpec((1, dh), lambda bi: (0, 0)),
            pl.BlockSpec((1, dh), lambda bi: (0, 0)),
        ],
        out_specs=pl.BlockSpec((1, s_len, width), lambda bi: (bi, 0, 0)),
        compiler_params=pltpu.CompilerParams(
            dimension_semantics=("parallel",), vmem_limit_bytes=VMEM_LIMIT),
        name="cross_attention",
    )(proj3, kv3, q_gain.reshape(1, dh), k_gain.reshape(1, dh))


def _merge_body(x_ref, ya_ref, u_ref, b_ref, c_ref, uh_ref, ch_ref, yx_ref, gt_ref, cw_ref,
                wa_ref, wc_ref, wx_ref, wo_ref, o_ref, *, tiles_per_seq):
    i = pl.program_id(0)
    tm, d = x_ref.shape
    z = c_ref[...] * u_ref[...]
    zh = ch_ref[...] * uh_ref[...]
    zh = jnp.where(i % tiles_per_seq == 0, 0.0, zh)
    row = lax.broadcasted_iota(jnp.int32, z.shape, 0)
    z1 = pltpu.roll(z, 1, 0)
    z1 = jnp.where(row == 0, zh[7:8, :], z1)
    z2 = pltpu.roll(z, 2, 0)
    z2 = jnp.where(row == 0, zh[6:7, :], jnp.where(row == 1, zh[7:8, :], z2))
    conv = cw_ref[0:1, :] * z2 + cw_ref[1:2, :] * z1 + cw_ref[2:3, :] * z
    y_c = jnp.dot((b_ref[...] * conv).astype(BF16), wc_ref[...], preferred_element_type=F32)
    y_a = jnp.dot(ya_ref[...], wa_ref[...], preferred_element_type=F32)
    y_x = jnp.dot(yx_ref[...], wx_ref[...], preferred_element_type=F32)
    merged = (jax.nn.sigmoid(gt_ref[:, 0:d]) * y_a
              + jax.nn.sigmoid(gt_ref[:, d:2 * d]) * y_c
              + jax.nn.sigmoid(gt_ref[:, 2 * d:3 * d]) * y_x)
    o_ref[...] = x_ref[...] + jnp.dot(merged.astype(BF16), wo_ref[...], preferred_element_type=F32)


def _merge(x2d, y_a, proj, conv_cols, y_x, gates, conv_w, w_a, w_c, w_x, w_o, s_len, tm):
    t, d = x2d.shape
    cwid = conv_w.shape[1]
    cu, cb, cc = conv_cols
    halo = SUBLANES
    rows_per_tile = tm // halo

    def halo_map(col):
        return lambda i: (jnp.maximum(i * rows_per_tile - 1, 0), col)

    full = lambda shape: pl.BlockSpec(shape, lambda i: (0, 0))
    return pl.pallas_call(
        functools.partial(_merge_body, tiles_per_seq=s_len // tm),
        out_shape=jax.ShapeDtypeStruct((t, d), F32),
        grid=(t // tm,),
        in_specs=[
            pl.BlockSpec((tm, d), lambda i: (i, 0)),
            pl.BlockSpec((tm, y_a.shape[1]), lambda i: (i, 0)),
            pl.BlockSpec((tm, cwid), lambda i: (i, cu)),
            pl.BlockSpec((tm, cwid), lambda i: (i, cb)),
            pl.BlockSpec((tm, cwid), lambda i: (i, cc)),
            pl.BlockSpec((halo, cwid), halo_map(cu)),
            pl.BlockSpec((halo, cwid), halo_map(cc)),
            pl.BlockSpec((tm, y_x.shape[1]), lambda i: (i, 0)),
            pl.BlockSpec((tm, gates.shape[1]), lambda i: (i, 0)),
            full(conv_w.shape), full(w_a.shape), full(w_c.shape), full(w_x.shape), full(w_o.shape),
        ],
        out_specs=pl.BlockSpec((tm, d), lambda i: (i, 0)),
        compiler_params=pltpu.CompilerParams(
            dimension_semantics=("parallel",), vmem_limit_bytes=VMEM_LIMIT),
        name="branch_merge",
    )(x2d, y_a, proj, proj, proj, proj, proj, y_x, gates, conv_w, w_a, w_c, w_x, w_o)


def _oddeven_merge_sort_pairs(n):
    pairs = []
    p = 1
    while p < n:
        k = p
        while k >= 1:
            for j in range(k % p, n - k, 2 * k):
                for i in range(min(k, n - j - k)):
                    if (i + j) // (2 * p) == (i + j + k) // (2 * p):
                        pairs.append((i + j, i + j + k))
            k //= 2
        p *= 2
    return pairs


def _bitonic_merge_pairs(n):
    pairs = []
    d = n // 2
    while d >= 1:
        pairs += [(i, i + d) for i in range(n) if i & d == 0]
        d //= 2
    return pairs


def _compare_exchange(slabs, pairs):
    for a, b in pairs:
        hi = jnp.maximum(slabs[a], slabs[b])
        lo = jnp.minimum(slabs[a], slabs[b])
        slabs[a], slabs[b] = hi, lo
    return slabs


def _top_sorted(slabs, k):
    n = len(slabs)
    slabs = _compare_exchange(list(slabs), _oddeven_merge_sort_pairs(n))
    merge = _bitonic_merge_pairs(k)
    for shift in (4, 2, 1):
        rolled = [pltpu.roll(s, shift, 0) for s in slabs]
        if len(slabs) < k:
            slabs = slabs + rolled[::-1]
        else:
            slabs = [jnp.maximum(slabs[i], rolled[k - 1 - i]) for i in range(k)]
        slabs = _compare_exchange(slabs, merge)
    return slabs


def _hyperbola(k):
    return [(i, j) for i in range(k) for j in range(k) if (i + 1) * (j + 1) <= k]


def _peer_scores(h, keys_ref, qry_ref, s2_ref, e1_ref, e2_ref, thr_ref, *, tt, lane_w):
    k = PEER_TOPK
    nkeys = keys_ref.shape[2]
    pairs = _hyperbola(k)
    n_slab = -(-len(pairs) // SUBLANES)
    n_slab = k // 2 if n_slab <= k // 2 else k
    sub = lax.broadcasted_iota(jnp.int32, (SUBLANES, lane_w), 0)
    inf = float("inf")
    for ts in range(tt // lane_w):
        cols = slice(ts * lane_w, (ts + 1) * lane_w)
        q1 = qry_ref[cols, pl.ds(pl.multiple_of(2 * h * nkeys, nkeys), nkeys)]
        q2 = qry_ref[cols, pl.ds(pl.multiple_of((2 * h + 1) * nkeys, nkeys), nkeys)]
        s1 = lax.dot_general(keys_ref[h, 0], q1, _NT, preferred_element_type=F32)
        s2 = lax.dot_general(keys_ref[h, 1], q2, _NT, preferred_element_type=F32)
        a = _top_sorted([s1[SUBLANES * v:SUBLANES * (v + 1)] for v in range(nkeys // SUBLANES)], k)
        b = _top_sorted([s2[SUBLANES * v:SUBLANES * (v + 1)] for v in range(nkeys // SUBLANES)], k)
        sums = {(i, j): a[i] + b[j] for (i, j) in pairs}
        packed = []
        for v in range(n_slab):
            slab = jnp.full((SUBLANES, lane_w), -inf, F32)
            for s in range(SUBLANES):
                idx = v * SUBLANES + s
                if idx < len(pairs):
                    slab = jnp.where(sub == s, sums[pairs[idx]], slab)
            packed.append(slab)
        tau = _top_sorted(packed, k)[k - 1]
        top = a[0] + b[0]
        z = jnp.zeros((SUBLANES, lane_w), F32)
        for slab in packed:
            z = z + jnp.where(slab >= tau, jnp.exp(slab - top), 0.0)
        zsum = jnp.sum(z, axis=0, keepdims=True)
        thr_rank = []
        for i in range(k):
            t = jnp.full((SUBLANES, lane_w), inf, F32)
            for j in range(k):
                if (i, j) in sums:
                    t = jnp.minimum(t, jnp.where(sums[(i, j)] >= tau, b[j], inf))
            thr_rank.append(t[0:1, :])
        thr = jnp.full((nkeys, lane_w), inf, F32)
        for i in range(k - 1, -1, -1):
            thr = jnp.where(s1 >= a[i][0:1, :], thr_rank[i], thr)
        thr_ref[h, ts] = thr
        s2_ref[h, ts] = s2
        e1_ref[h, ts] = jnp.exp(s1 - a[0][0:1, :])
        e2_ref[h, ts] = jnp.exp(s2 - b[0][0:1, :]) / zsum


def _peer_weights(key_blk, key_off, col0, ncols, act_ref, p_ref, s2_ref, e1_ref, e2_ref, thr_ref, *, lane_w):
    n_heads, _, nkeys, _ = s2_ref.shape
    ec = act_ref.shape[0]
    key_rows = pl.ds(pl.multiple_of(key_blk * SUBLANES, SUBLANES), SUBLANES)

    def unit(il, ts):
        r = key_off + il
        rows = slice(il * nkeys, (il + 1) * nkeys)
        cols = slice(ts * lane_w, (ts + 1) * lane_w)
        w = jnp.zeros((nkeys, lane_w), F32)
        for h in range(n_heads):
            thr = thr_ref[h, ts, key_rows, :][r:r + 1, :]
            e1 = e1_ref[h, ts, key_rows, :][r:r + 1, :]
            w = w + jnp.where(s2_ref[h, ts] >= thr, e2_ref[h, ts], 0.0) * e1
        a = act_ref[rows, cols]
        gelu = 0.5 * a * (1.0 + lax.erf(a * (2.0 ** -0.5)))
        p_ref[rows, cols] = (w * gelu).astype(BF16)

    return [functools.partial(unit, il, ts) for il in range(ec // nkeys)
            for ts in range(col0 // lane_w, (col0 + ncols) // lane_w)]


def _peer_body(h_ref, g_ref, wq_ref, keys_ref, u_ref, vt_ref, o_ref,
               xn_ref, qry_ref, s2_ref, e1_ref, e2_ref, thr_ref,
               act0_ref, act1_ref, p0_ref, p1_ref, acc_ref, *, lane_w, n_chunks):
    s = pl.program_id(1)
    tt = h_ref.shape[0]
    n_heads = keys_ref.shape[0]

    @pl.when(s == 0)
    def _():
        xn = _rms(h_ref[...], g_ref[...]).astype(BF16)
        xn_ref[...] = xn
        qry_ref[...] = jnp.dot(xn, wq_ref[...], preferred_element_type=F32).astype(BF16)

        def head(h, carry):
            _peer_scores(h, keys_ref, qry_ref, s2_ref, e1_ref, e2_ref, thr_ref, tt=tt, lane_w=lane_w)
            return carry

        lax.fori_loop(0, n_heads, head, 0)
        acc_ref[...] = jnp.zeros(acc_ref.shape, F32)
        act1_ref[...] = jnp.zeros(act1_ref.shape, F32)
        p0_ref[...] = jnp.zeros(p0_ref.shape, BF16)

    keys_per_chunk = u_ref.shape[0] // keys_ref.shape[2]
    assert 2 * keys_per_chunk == SUBLANES
    key_blk = jnp.clip((s - 1) // 2, 0, n_chunks // 2 - 1)

    def step(act_w, act_r, p_w, p_r, key_off):
        half = tt // 2
        d = u_ref.shape[1]
        ec = u_ref.shape[0]
        kq = d // 4
        for c0 in (0, half):
            cols = slice(c0, c0 + half)
            units = _peer_weights(key_blk, key_off, c0, half, act_r, p_w, s2_ref, e1_ref, e2_ref, thr_ref,
                                  lane_w=lane_w)
            assert len(units) == 8
            a = None
            for q in range(4):
                units[q]()
                part = lax.dot_general(u_ref[:, q * kq:(q + 1) * kq], xn_ref[cols, q * kq:(q + 1) * kq],
                                       _NT, preferred_element_type=F32)
                a = part if a is None else a + part
            act_w[:, cols] = a
            units[4]()
            v = jnp.dot(vt_ref[:, :ec // 2], p_r[:ec // 2, cols], preferred_element_type=F32)
            units[5]()
            units[6]()
            v = v + jnp.dot(vt_ref[:, ec // 2:], p_r[ec // 2:, cols], preferred_element_type=F32)
            acc_ref[:, cols] += v
            units[7]()

    @pl.when(s % 2 == 0)
    def _():
        step(act0_ref, act1_ref, p1_ref, p0_ref, keys_per_chunk)

    @pl.when(s % 2 == 1)
    def _():
        step(act1_ref, act0_ref, p0_ref, p1_ref, 0)

    @pl.when(s == n_chunks + 1)
    def _():
        o_ref[...] = h_ref[...] + acc_ref[...].T


def _peer(h2d, g, w_query, keys, u, v_t, tt, ec):
    t, d = h2d.shape
    n_heads, _, nkeys, dk = keys.shape
    n_chunks = u.shape[0] // ec
    qw = w_query.shape[1]
    stat = lambda: pltpu.VMEM((n_heads, tt // LANES, nkeys, LANES), F32)
    return pl.pallas_call(
        functools.partial(_peer_body, lane_w=LANES, n_chunks=n_chunks),
        out_shape=jax.ShapeDtypeStruct((t, d), F32),
        grid=(t // tt, n_chunks + 2),
        in_specs=[
            pl.BlockSpec((tt, d), lambda i, s: (i, 0)),
            pl.BlockSpec((1, d), lambda i, s: (0, 0)),
            pl.BlockSpec((d, qw), lambda i, s: (0, 0)),
            pl.BlockSpec(keys.shape, lambda i, s: (0, 0, 0, 0)),
            pl.BlockSpec((ec, d), lambda i, s: (jnp.minimum(s, n_chunks - 1), 0)),
            pl.BlockSpec((d, ec), lambda i, s: (0, jnp.clip(s - 2, 0, n_chunks - 1))),
        ],
        out_specs=pl.BlockSpec((tt, d), lambda i, s: (i, 0)),
        scratch_shapes=[
            pltpu.VMEM((tt, d), BF16),
            pltpu.VMEM((tt, qw), BF16),
            stat(), stat(), stat(), stat(),
            pltpu.VMEM((ec, tt), F32), pltpu.VMEM((ec, tt), F32),
            pltpu.VMEM((ec, tt), BF16), pltpu.VMEM((ec, tt), BF16),
            pltpu.VMEM((d, tt), F32),
        ],
        compiler_params=pltpu.CompilerParams(
            dimension_semantics=("parallel", "arbitrary"), vmem_limit_bytes=VMEM_LIMIT),
        name="peer",
    )(h2d, g.reshape(1, d), w_query, keys, u, v_t)


def _layer(h2d, mem2d, batch, g_mix, w_in, q_gain_a, k_gain_a, bias_t, conv_w, g_mem, w_mem_kv,
           q_gain_x, k_gain_x, w_br_attn, w_br_conv, w_br_x, w_out, g_ffn,
           peer_w_query, peer_sub_keys, peer_u, peer_v):
    t, d = h2d.shape
    s_len = t // batch
    n_heads_a = bias_t.shape[0]
    dh_a = q_gain_a.shape[0]
    attn_w = n_heads_a * dh_a
    conv_wid = conv_w.shape[1]
    dh_x = q_gain_x.shape[0]
    xattn_w = w_mem_kv.shape[1] // 2
    assert attn_w == conv_wid == xattn_w, "column blocks of the projection must have equal width"
    branch_cols = 3 * attn_w + 3 * conv_wid + xattn_w

    w_in16 = w_in.astype(BF16)
    proj = _norm_matmul(h2d, g_mix, w_in16[:, :branch_cols], 512, branch_cols // 2, "proj_branches")
    gates = _norm_matmul(h2d, g_mix, w_in16[:, branch_cols:], 512, d, "proj_gates")
    proj3 = proj.reshape(batch, s_len, branch_cols)

    y_a = _moba(proj3, q_gain_a, k_gain_a, bias_t, n_heads_a, dh_a).reshape(t, attn_w)

    kv = _norm_matmul(mem2d, g_mem, w_mem_kv.astype(BF16), 512, w_mem_kv.shape[1], "mem_kv")
    kv3 = kv.reshape(batch, mem2d.shape[0] // batch, kv.shape[1])
    y_x = _xattn(proj3, 6, kv3, q_gain_x, k_gain_x, dh_x).reshape(t, xattn_w)

    h2d = _merge(h2d, y_a, proj, (3, 4, 5), y_x, gates, conv_w,
                 w_br_attn.astype(BF16), w_br_conv.astype(BF16), w_br_x.astype(BF16),
                 w_out.astype(BF16), s_len, 512)

    return _peer(h2d, g_ffn, peer_w_query.astype(BF16), peer_sub_keys.astype(BF16),
                 peer_u.astype(BF16), peer_v.astype(BF16).T, 512, 512)


def kernel(x, mem, g_mix, w_in, q_gain_a, k_gain_a, rel_bias, conv_w, g_mem, w_mem_kv, q_gain_x, k_gain_x, w_br_attn, w_br_conv, w_br_x, w_out, g_ffn, peer_w_query, peer_sub_keys, peer_u, peer_v):
    batch, s_len, d = x.shape
    depth = g_mix.shape[0]
    bias_t = _bias_tiles(rel_bias, s_len // MOBA_BLOCK)
    h2d = x.reshape(batch * s_len, d)
    mem2d = mem.reshape(batch * mem.shape[1], d)
    for l in range(depth):
        h2d = _layer(h2d, mem2d, batch, g_mix[l], w_in[l], q_gain_a[l], k_gain_a[l], bias_t, conv_w[l],
                     g_mem[l], w_mem_kv[l], q_gain_x[l], k_gain_x[l], w_br_attn[l], w_br_conv[l],
                     w_br_x[l], w_out[l], g_ffn[l], peer_w_query[l], peer_sub_keys[l],
                     peer_u[l], peer_v[l])
    return h2d.reshape(batch, s_len, d)
```

```python
import functools
import math

import numpy as np
import jax
import jax.numpy as jnp
from jax import lax
from jax.experimental import pallas as pl
from jax.experimental.pallas import tpu as pltpu

F32 = jnp.float32
BF16 = jnp.bfloat16

RMS_EPS = 1e-6
NEG = -1e30

MOBA_BLOCK = 256
MOBA_TOPK = 3
REL_MAX_DIST = 1024
PEER_TOPK = 16

SUBLANES = 8
LANES = 128
VMEM_LIMIT = 56 * 1024 * 1024

_NT = (((1,), (1,)), ((), ()))


def _rms(x, g):
    r = lax.rsqrt(jnp.mean(x * x, axis=-1, keepdims=True) + RMS_EPS)
    return x * r * g


def _norm_matmul_body(x_ref, g_ref, w_ref, o_ref, xn_ref):
    @pl.when(pl.program_id(1) == 0)
    def _():
        xn_ref[...] = _rms(x_ref[...], g_ref[...]).astype(BF16)

    o_ref[...] = jnp.dot(xn_ref[...], w_ref[...], preferred_element_type=F32)


def _norm_matmul(x2d, g, w, tm, tn, name):
    t, d = x2d.shape
    n = w.shape[1]
    return pl.pallas_call(
        _norm_matmul_body,
        out_shape=jax.ShapeDtypeStruct((t, n), F32),
        grid=(t // tm, n // tn),
        in_specs=[
            pl.BlockSpec((tm, d), lambda i, j: (i, 0)),
            pl.BlockSpec((1, d), lambda i, j: (0, 0)),
            pl.BlockSpec((d, tn), lambda i, j: (0, j)),
        ],
        out_specs=pl.BlockSpec((tm, tn), lambda i, j: (i, j)),
        scratch_shapes=[pltpu.VMEM((tm, d), BF16)],
        compiler_params=pltpu.CompilerParams(
            dimension_semantics=("parallel", "arbitrary"), vmem_limit_bytes=VMEM_LIMIT),
        name=name,
    )(x2d, g.reshape(1, d), w)


def _rel_bucket(dist, n_buckets):
    max_exact = n_buckets // 2
    d = jnp.maximum(dist, 0)
    df = jnp.maximum(d, 1).astype(F32)
    large = max_exact + (jnp.log(df / max_exact) / math.log(REL_MAX_DIST / max_exact)
                         * (n_buckets - max_exact)).astype(jnp.int32)
    large = jnp.minimum(large, n_buckets - 1)
    return jnp.where(d < max_exact, d, large)


def _bias_tiles_body(bucket_ref, rel_ref, o_ref, *, n_buckets, n_delta):
    h = pl.program_id(0)
    blk = MOBA_BLOCK
    for delta in range(n_delta):
        bk = bucket_ref[delta:delta + 1, :]
        w = jnp.zeros(bk.shape, F32)
        for b in range(n_buckets):
            w = jnp.where(bk == b, rel_ref[b, h], w)
        x = jnp.broadcast_to(w, (blk, 2 * blk))
        r = pltpu.roll(x, blk, 1, stride=1, stride_axis=0)
        tile = r[:, :blk]
        if delta == 0:
            krow = lax.broadcasted_iota(jnp.int32, (blk, blk), 0)
            qcol = lax.broadcasted_iota(jnp.int32, (blk, blk), 1)
            tile = jnp.where(qcol >= krow, tile, NEG)
        o_ref[0, delta] = tile


def _bias_tiles(rel_bias, n_delta):
    n_buckets, n_heads = rel_bias.shape
    blk = MOBA_BLOCK
    dist = (jnp.arange(n_delta, dtype=jnp.int32)[:, None] * blk - blk
            + jnp.arange(2 * blk, dtype=jnp.int32)[None, :])
    bucket = _rel_bucket(dist, n_buckets).astype(jnp.int32)
    return pl.pallas_call(
        functools.partial(_bias_tiles_body, n_buckets=n_buckets, n_delta=n_delta),
        out_shape=jax.ShapeDtypeStruct((n_heads, n_delta, blk, blk), F32),
        grid=(n_heads,),
        in_specs=[
            pl.BlockSpec((n_delta, 2 * blk), lambda h: (0, 0)),
            pl.BlockSpec(memory_space=pltpu.SMEM),
        ],
        out_specs=pl.BlockSpec((1, n_delta, blk, blk), lambda h: (h, 0, 0, 0)),
        compiler_params=pltpu.CompilerParams(dimension_semantics=("arbitrary",)),
        name="rel_bias_tiles",
    )(bucket, rel_bias)


def _moba_body(q_ref, k_ref, v_ref, qg_ref, kg_ref, bias_ref, o_ref, ot_ref, *, dh):
    s_len = q_ref.shape[1]
    blk = MOBA_BLOCK
    nb = s_len // blk
    scale = dh ** -0.5
    heads_per_step = q_ref.shape[2] // dh
    vt_all = v_ref[0].T
    for hh in range(heads_per_step):
        sl = slice(hh * dh, (hh + 1) * dh)
        qn = _rms(q_ref[0, :, sl], qg_ref[...])
        kn = _rms(k_ref[0, :, sl], kg_ref[...])
        kmean = jnp.mean(kn.reshape(nb, blk, dh), axis=1)
        qb16 = qn.astype(BF16)
        kb16 = kn.astype(BF16)
        vt = vt_all[sl, :].astype(BF16)
        gate_t = lax.dot_general(kmean.astype(BF16), qb16, _NT, preferred_element_type=F32)
        row = lax.broadcasted_iota(jnp.int32, (nb, blk), 0)
        for qb in range(nb):
            qs = slice(qb * blk, (qb + 1) * blk)
            qt = qb16[qs]
            g = gate_t[:, qs]
            rank = jnp.zeros((nb, blk), jnp.int32)
            for j in range(qb):
                gj = g[j:j + 1, :]
                beats = jnp.where(gj > g, 1, jnp.where((gj == g) & (j < row), 1, 0))
                rank = rank + beats
            negm = jnp.where((rank < MOBA_TOPK) & (row < qb), 0.0, NEG)
            parts = []
            for kb in range(qb + 1):
                st = lax.dot_general(kb16[kb * blk:(kb + 1) * blk], qt, _NT,
                                     preferred_element_type=F32)
                st = st * scale + bias_ref[hh, qb - kb]
                if kb < qb:
                    st = st + negm[kb:kb + 1, :]
                parts.append(st)
            logit = jnp.concatenate(parts, axis=0) if len(parts) > 1 else parts[0]
            m = jnp.max(logit, axis=0, keepdims=True)
            p = jnp.exp(logit - m)
            denom = jnp.sum(p, axis=0, keepdims=True)
            nk = (qb + 1) * blk
            o_t = jnp.dot(vt[:, :nk], p.astype(BF16), preferred_element_type=F32)
            ot_ref[sl, qs] = o_t / denom
    o_ref[0] = ot_ref[...].T.astype(o_ref.dtype)


def _moba(proj3, q_gain, k_gain, bias_t, n_heads, dh):
    b, s_len, _ = proj3.shape
    hps = LANES // dh
    width = n_heads * dh
    nblk = width // LANES
    n_delta = bias_t.shape[1]
    blk = MOBA_BLOCK
    return pl.pallas_call(
        functools.partial(_moba_body, dh=dh),
        out_shape=jax.ShapeDtypeStruct((b, s_len, width), BF16),
        grid=(n_heads // hps, b),
        in_specs=[
            pl.BlockSpec((1, s_len, LANES), lambda hp, bi: (bi, 0, hp)),
            pl.BlockSpec((1, s_len, LANES), lambda hp, bi: (bi, 0, nblk + hp)),
            pl.BlockSpec((1, s_len, LANES), lambda hp, bi: (bi, 0, 2 * nblk + hp)),
            pl.BlockSpec((1, dh), lambda hp, bi: (0, 0)),
            pl.BlockSpec((1, dh), lambda hp, bi: (0, 0)),
            pl.BlockSpec((hps, n_delta, blk, blk), lambda hp, bi: (hp, 0, 0, 0)),
        ],
        out_specs=pl.BlockSpec((1, s_len, LANES), lambda hp, bi: (bi, 0, hp)),
        scratch_shapes=[pltpu.VMEM((LANES, s_len), F32)],
        compiler_params=pltpu.CompilerParams(
            dimension_semantics=("parallel", "arbitrary"), vmem_limit_bytes=VMEM_LIMIT),
        name="moba_attention",
    )(proj3, proj3, proj3, q_gain.reshape(1, dh), k_gain.reshape(1, dh), bias_t)


def _xattn_body(q_ref, kv_ref, qg_ref, kg_ref, o_ref, *, dh, chunk):
    s_len = q_ref.shape[1]
    width = q_ref.shape[2]
    n_heads = width // dh
    scale = dh ** -0.5
    for h in range(n_heads):
        sl = slice(h * dh, (h + 1) * dh)
        kn = _rms(kv_ref[0, :, sl], kg_ref[...]).astype(BF16)
        v = kv_ref[0, :, width + h * dh:width + (h + 1) * dh].astype(BF16)
        for c in range(s_len // chunk):
            rs = slice(c * chunk, (c + 1) * chunk)
            qn = _rms(q_ref[0, rs, sl], qg_ref[...]).astype(BF16)
            s = lax.dot_general(qn, kn, _NT, preferred_element_type=F32) * scale
            m = jnp.max(s, axis=-1, keepdims=True)
            p = jnp.exp(s - m)
            denom = jnp.sum(p, axis=-1, keepdims=True)
            o = jnp.dot(p.astype(BF16), v, preferred_element_type=F32) / denom
            o_ref[0, rs, sl] = o.astype(o_ref.dtype)


def _xattn(proj3, q_col_block, kv3, q_gain, k_gain, dh):
    b, s_len, _ = proj3.shape
    m_len, kv_width = kv3.shape[1], kv3.shape[2]
    width = kv_width // 2
    return pl.pallas_call(
        functools.partial(_xattn_body, dh=dh, chunk=512),
        out_shape=jax.ShapeDtypeStruct((b, s_len, width), BF16),
        grid=(b,),
        in_specs=[
            pl.BlockSpec((1, s_len, width), lambda bi: (bi, 0, q_col_block)),
            pl.BlockSpec((1, m_len, kv_width), lambda bi: (bi, 0, 0)),
            pl.BlockSpec((1, dh), lambda bi: (0, 0)),
            pl.BlockSpec((1, dh), lambda bi: (0, 0)),
        ],
        out_specs=pl.BlockSpec((1, s_len, width), lambda bi: (bi, 0, 0)),
        compiler_params=pltpu.CompilerParams(
            dimension_semantics=("parallel",), vmem_limit_bytes=VMEM_LIMIT),
        name="cross_attention",
    )(proj3, kv3, q_gain.reshape(1, dh), k_gain.reshape(1, dh))


def _merge_body(x_ref, ya_ref, u_ref, b_ref, c_ref, uh_ref, ch_ref, yx_ref, gt_ref, cw_ref,
                wa_ref, wc_ref, wx_ref, wo_ref, o_ref, *, tiles_per_seq):
    i = pl.program_id(0)
    tm, d = x_ref.shape
    z = c_ref[...] * u_ref[...]
    zh = ch_ref[...] * uh_ref[...]
    zh = jnp.where(i % tiles_per_seq == 0, 0.0, zh)
    row = lax.broadcasted_iota(jnp.int32, z.shape, 0)
    z1 = pltpu.roll(z, 1, 0)
    z1 = jnp.where(row == 0, zh[7:8, :], z1)
    z2 = pltpu.roll(z, 2, 0)
    z2 = jnp.where(row == 0, zh[6:7, :], jnp.where(row == 1, zh[7:8, :], z2))
    conv = cw_ref[0:1, :] * z2 + cw_ref[1:2, :] * z1 + cw_ref[2:3, :] * z
    y_c = jnp.dot((b_ref[...] * conv).astype(BF16), wc_ref[...], preferred_element_type=F32)
    y_a = jnp.dot(ya_ref[...], wa_ref[...], preferred_element_type=F32)
    y_x = jnp.dot(yx_ref[...], wx_ref[...], preferred_element_type=F32)
    merged = (jax.nn.sigmoid(gt_ref[:, 0:d]) * y_a
              + jax.nn.sigmoid(gt_ref[:, d:2 * d]) * y_c
              + jax.nn.sigmoid(gt_ref[:, 2 * d:3 * d]) * y_x)
    o_ref[...] = x_ref[...] + jnp.dot(merged.astype(BF16), wo_ref[...], preferred_element_type=F32)


def _merge(x2d, y_a, proj, conv_cols, y_x, gates, conv_w, w_a, w_c, w_x, w_o, s_len, tm):
    t, d = x2d.shape
    cwid = conv_w.shape[1]
    cu, cb, cc = conv_cols
    halo = SUBLANES
    rows_per_tile = tm // halo

    def halo_map(col):
        return lambda i: (jnp.maximum(i * rows_per_tile - 1, 0), col)

    full = lambda shape: pl.BlockSpec(shape, lambda i: (0, 0))
    return pl.pallas_call(
        functools.partial(_merge_body, tiles_per_seq=s_len // tm),
        out_shape=jax.ShapeDtypeStruct((t, d), F32),
        grid=(t // tm,),
        in_specs=[
            pl.BlockSpec((tm, d), lambda i: (i, 0)),
            pl.BlockSpec((tm, y_a.shape[1]), lambda i: (i, 0)),
            pl.BlockSpec((tm, cwid), lambda i: (i, cu)),
            pl.BlockSpec((tm, cwid), lambda i: (i, cb)),
            pl.BlockSpec((tm, cwid), lambda i: (i, cc)),
            pl.BlockSpec((halo, cwid), halo_map(cu)),
            pl.BlockSpec((halo, cwid), halo_map(cc)),
            pl.BlockSpec((tm, y_x.shape[1]), lambda i: (i, 0)),
            pl.BlockSpec((tm, gates.shape[1]), lambda i: (i, 0)),
            full(conv_w.shape), full(w_a.shape), full(w_c.shape), full(w_x.shape), full(w_o.shape),
        ],
        out_specs=pl.BlockSpec((tm, d), lambda i: (i, 0)),
        compiler_params=pltpu.CompilerParams(
            dimension_semantics=("parallel",), vmem_limit_bytes=VMEM_LIMIT),
        name="branch_merge",
    )(x2d, y_a, proj, proj, proj, proj, proj, y_x, gates, conv_w, w_a, w_c, w_x, w_o)


def _oddeven_merge_sort_pairs(n):
    pairs = []
    p = 1
    while p < n:
        k = p
        while k >= 1:
            for j in range(k % p, n - k, 2 * k):
                for i in range(min(k, n - j - k)):
                    if (i + j) // (2 * p) == (i + j + k) // (2 * p):
                        pairs.append((i + j, i + j + k))
            k //= 2
        p *= 2
    return pairs


def _bitonic_merge_pairs(n):
    pairs = []
    d = n // 2
    while d >= 1:
        pairs += [(i, i + d) for i in range(n) if i & d == 0]
        d //= 2
    return pairs


def _compare_exchange(slabs, pairs):
    for a, b in pairs:
        hi = jnp.maximum(slabs[a], slabs[b])
        lo = jnp.minimum(slabs[a], slabs[b])
        slabs[a], slabs[b] = hi, lo
    return slabs


def _top_sorted(slabs, k):
    n = len(slabs)
    slabs = _compare_exchange(list(slabs), _oddeven_merge_sort_pairs(n))
    merge = _bitonic_merge_pairs(k)
    for shift in (4, 2, 1):
        rolled = [pltpu.roll(s, shift, 0) for s in slabs]
        if len(slabs) < k:
            slabs = slabs + rolled[::-1]
        else:
            slabs = [jnp.maximum(slabs[i], rolled[k - 1 - i]) for i in range(k)]
        slabs = _compare_exchange(slabs, merge)
    return slabs


def _hyperbola(k):
    return [(i, j) for i in range(k) for j in range(k) if (i + 1) * (j + 1) <= k]


def _peer_scores(h, keys_ref, qry_ref, s2_ref, e1_ref, e2_ref, thr_ref, *, tt, lane_w):
    k = PEER_TOPK
    nkeys = keys_ref.shape[2]
    pairs = _hyperbola(k)
    n_slab = -(-len(pairs) // SUBLANES)
    n_slab = k // 2 if n_slab <= k // 2 else k
    sub = lax.broadcasted_iota(jnp.int32, (SUBLANES, lane_w), 0)
    inf = float("inf")
    for ts in range(tt // lane_w):
        cols = slice(ts * lane_w, (ts + 1) * lane_w)
        q1 = qry_ref[cols, pl.ds(pl.multiple_of(2 * h * nkeys, nkeys), nkeys)]
        q2 = qry_ref[cols, pl.ds(pl.multiple_of((2 * h + 1) * nkeys, nkeys), nkeys)]
        s1 = lax.dot_general(keys_ref[h, 0], q1, _NT, preferred_element_type=F32)
        s2 = lax.dot_general(keys_ref[h, 1], q2, _NT, preferred_element_type=F32)
        a = _top_sorted([s1[SUBLANES * v:SUBLANES * (v + 1)] for v in range(nkeys // SUBLANES)], k)
        b = _top_sorted([s2[SUBLANES * v:SUBLANES * (v + 1)] for v in range(nkeys // SUBLANES)], k)
        sums = {(i, j): a[i] + b[j] for (i, j) in pairs}
        packed = []
        for v in range(n_slab):
            slab = jnp.full((SUBLANES, lane_w), -inf, F32)
            for s in range(SUBLANES):
                idx = v * SUBLANES + s
                if idx < len(pairs):
                    slab = jnp.where(sub == s, sums[pairs[idx]], slab)
            packed.append(slab)
        tau = _top_sorted(packed, k)[k - 1]
        top = a[0] + b[0]
        z = jnp.zeros((SUBLANES, lane_w), F32)
        for slab in packed:
            z = z + jnp.where(slab >= tau, jnp.exp(slab - top), 0.0)
        zsum = jnp.sum(z, axis=0, keepdims=True)
        thr_rank = []
        for i in range(k):
            t = jnp.full((SUBLANES, lane_w), inf, F32)
            for j in range(k):
                if (i, j) in sums:
                    t = jnp.minimum(t, jnp.where(sums[(i, j)] >= tau, b[j], inf))
            thr_rank.append(t[0:1, :])
        thr = jnp.full((nkeys, lane_w), inf, F32)
        for i in range(k - 1, -1, -1):
            thr = jnp.where(s1 >= a[i][0:1, :], thr_rank[i], thr)
        thr_ref[h, ts] = thr
        s2_ref[h, ts] = s2
        e1_ref[h, ts] = jnp.exp(s1 - a[0][0:1, :])
        e2_ref[h, ts] = jnp.exp(s2 - b[0][0:1, :]) / zsum


def _peer_weights(key_blk, key_off, col0, ncols, act_ref, p_ref, s2_ref, e1_ref, e2_ref, thr_ref, *, lane_w):
    n_heads, _, nkeys, _ = s2_ref.shape
    ec = act_ref.shape[0]
    key_rows = pl.ds(pl.multiple_of(key_blk * SUBLANES, SUBLANES), SUBLANES)

    def unit(il, ts):
        r = key_off + il
        rows = slice(il * nkeys, (il + 1) * nkeys)
        cols = slice(ts * lane_w, (ts + 1) * lane_w)
        w = jnp.zeros((nkeys, lane_w), F32)
        for h in range(n_heads):
            thr = thr_ref[h, ts, key_rows, :][r:r + 1, :]
            e1 = e1_ref[h, ts, key_rows, :][r:r + 1, :]
            w = w + jnp.where(s2_ref[h, ts] >= thr, e2_ref[h, ts], 0.0) * e1
        a = act_ref[rows, cols]
        gelu = 0.5 * a * (1.0 + lax.erf(a * (2.0 ** -0.5)))
        p_ref[rows, cols] = (w * gelu).astype(BF16)

    return [functools.partial(unit, il, ts) for il in range(ec // nkeys)
            for ts in range(col0 // lane_w, (col0 + ncols) // lane_w)]


def _peer_body(h_ref, g_ref, wq_ref, keys_ref, u_ref, vt_ref, o_ref,
               xn_ref, qry_ref, s2_ref, e1_ref, e2_ref, thr_ref,
               act0_ref, act1_ref, p0_ref, p1_ref, acc_ref, *, lane_w, n_chunks):
    s = pl.program_id(1)
    tt = h_ref.shape[0]
    n_heads = keys_ref.shape[0]

    @pl.when(s == 0)
    def _():
        xn = _rms(h_ref[...], g_ref[...]).astype(BF16)
        xn_ref[...] = xn
        qry_ref[...] = jnp.dot(xn, wq_ref[...], preferred_element_type=F32).astype(BF16)

        def head(h, carry):
            _peer_scores(h, keys_ref, qry_ref, s2_ref, e1_ref, e2_ref, thr_ref, tt=tt, lane_w=lane_w)
            return carry

        lax.fori_loop(0, n_heads, head, 0)
        acc_ref[...] = jnp.zeros(acc_ref.shape, F32)
        act1_ref[...] = jnp.zeros(act1_ref.shape, F32)
        p0_ref[...] = jnp.zeros(p0_ref.shape, BF16)

    keys_per_chunk = u_ref.shape[0] // keys_ref.shape[2]
    assert 2 * keys_per_chunk == SUBLANES
    key_blk = jnp.clip((s - 1) // 2, 0, n_chunks // 2 - 1)

    def step(act_w, act_r, p_w, p_r, key_off):
        half = tt // 2
        d = u_ref.shape[1]
        ec = u_ref.shape[0]
        kq = d // 4
        for c0 in (0, half):
            cols = slice(c0, c0 + half)
            units = _peer_weights(key_blk, key_off, c0, half, act_r, p_w, s2_ref, e1_ref, e2_ref, thr_ref,
                                  lane_w=lane_w)
            assert len(units) == 8
            a = None
            for q in range(4):
                units[q]()
                part = lax.dot_general(u_ref[:, q * kq:(q + 1) * kq], xn_ref[cols, q * kq:(q + 1) * kq],
                                       _NT, preferred_element_type=F32)
                a = part if a is None else a + part
            act_w[:, cols] = a
            units[4]()
            v = jnp.dot(vt_ref[0, :, :ec // 2], p_r[:ec // 2, cols], preferred_element_type=F32)
            units[5]()
            units[6]()
            v = v + jnp.dot(vt_ref[0, :, ec // 2:], p_r[ec // 2:, cols], preferred_element_type=F32)
            acc_ref[:, cols] += v
            units[7]()

    @pl.when(s % 2 == 0)
    def _():
        step(act0_ref, act1_ref, p1_ref, p0_ref, keys_per_chunk)

    @pl.when(s % 2 == 1)
    def _():
        step(act1_ref, act0_ref, p0_ref, p1_ref, 0)

    @pl.when(s == n_chunks + 1)
    def _():
        o_ref[...] = h_ref[...] + acc_ref[...].T


def _peer(h2d, g, w_query, keys, u, v, tt, ec):
    t, d = h2d.shape
    v_t = v.reshape(u.shape[0] // ec, ec, d).transpose(0, 2, 1)
    n_heads, _, nkeys, dk = keys.shape
    n_chunks = u.shape[0] // ec
    qw = w_query.shape[1]
    stat = lambda: pltpu.VMEM((n_heads, tt // LANES, nkeys, LANES), F32)
    return pl.pallas_call(
        functools.partial(_peer_body, lane_w=LANES, n_chunks=n_chunks),
        out_shape=jax.ShapeDtypeStruct((t, d), F32),
        grid=(t // tt, n_chunks + 2),
        in_specs=[
            pl.BlockSpec((tt, d), lambda i, s: (i, 0)),
            pl.BlockSpec((1, d), lambda i, s: (0, 0)),
            pl.BlockSpec((d, qw), lambda i, s: (0, 0)),
            pl.BlockSpec(keys.shape, lambda i, s: (0, 0, 0, 0)),
            pl.BlockSpec((ec, d), lambda i, s: (jnp.minimum(s, n_chunks - 1), 0)),
            pl.BlockSpec((1, d, ec), lambda i, s: (jnp.clip(s - 2, 0, n_chunks - 1), 0, 0)),
        ],
        out_specs=pl.BlockSpec((tt, d), lambda i, s: (i, 0)),
        scratch_shapes=[
            pltpu.VMEM((tt, d), BF16),
            pltpu.VMEM((tt, qw), BF16),
            stat(), stat(), stat(), stat(),
            pltpu.VMEM((ec, tt), F32), pltpu.VMEM((ec, tt), F32),
            pltpu.VMEM((ec, tt), BF16), pltpu.VMEM((ec, tt), BF16),
            pltpu.VMEM((d, tt), F32),
        ],
        compiler_params=pltpu.CompilerParams(
            dimension_semantics=("parallel", "arbitrary"), vmem_limit_bytes=VMEM_LIMIT),
        name="peer",
    )(h2d, g.reshape(1, d), w_query, keys, u, v_t)


def _layer(h2d, mem2d, batch, g_mix, w_in, q_gain_a, k_gain_a, bias_t, conv_w, g_mem, w_mem_kv,
           q_gain_x, k_gain_x, w_br_attn, w_br_conv, w_br_x, w_out, g_ffn,
           peer_w_query, peer_sub_keys, peer_u, peer_v):
    t, d = h2d.shape
    s_len = t // batch
    n_heads_a = bias_t.shape[0]
    dh_a = q_gain_a.shape[0]
    attn_w = n_heads_a * dh_a
    conv_wid = conv_w.shape[1]
    dh_x = q_gain_x.shape[0]
    xattn_w = w_mem_kv.shape[1] // 2
    assert attn_w == conv_wid == xattn_w, "column blocks of the projection must have equal width"
    branch_cols = 3 * attn_w + 3 * conv_wid + xattn_w

    w_in16 = w_in.astype(BF16)
    proj = _norm_matmul(h2d, g_mix, w_in16[:, :branch_cols], 512, branch_cols // 2, "proj_branches")
    gates = _norm_matmul(h2d, g_mix, w_in16[:, branch_cols:], 512, d, "proj_gates")
    proj3 = proj.reshape(batch, s_len, branch_cols)

    y_a = _moba(proj3, q_gain_a, k_gain_a, bias_t, n_heads_a, dh_a).reshape(t, attn_w)

    kv = _norm_matmul(mem2d, g_mem, w_mem_kv.astype(BF16), 512, w_mem_kv.shape[1], "mem_kv")
    kv3 = kv.reshape(batch, mem2d.shape[0] // batch, kv.shape[1])
    y_x = _xattn(proj3, 6, kv3, q_gain_x, k_gain_x, dh_x).reshape(t, xattn_w)

    h2d = _merge(h2d, y_a, proj, (3, 4, 5), y_x, gates, conv_w,
                 w_br_attn.astype(BF16), w_br_conv.astype(BF16), w_br_x.astype(BF16),
                 w_out.astype(BF16), s_len, 512)

    return _peer(h2d, g_ffn, peer_w_query.astype(BF16), peer_sub_keys.astype(BF16),
                 peer_u.astype(BF16), peer_v.astype(BF16), 512, 512)


def kernel(x, mem, g_mix, w_in, q_gain_a, k_gain_a, rel_bias, conv_w, g_mem, w_mem_kv, q_gain_x, k_gain_x, w_br_attn, w_br_conv, w_br_x, w_out, g_ffn, peer_w_query, peer_sub_keys, peer_u, peer_v):
    batch, s_len, d = x.shape
    depth = g_mix.shape[0]
    bias_t = _bias_tiles(rel_bias, s_len // MOBA_BLOCK)
    h2d = x.reshape(batch * s_len, d)
    mem2d = mem.reshape(batch * mem.shape[1], d)
    for l in range(depth):
        h2d = _layer(h2d, mem2d, batch, g_mix[l], w_in[l], q_gain_a[l], k_gain_a[l], bias_t, conv_w[l],
                     g_mem[l], w_mem_kv[l], q_gain_x[l], k_gain_x[l], w_br_attn[l], w_br_conv[l],
                     w_br_x[l], w_out[l], g_ffn[l], peer_w_query[l], peer_sub_keys[l],
                     peer_u[l], peer_v[l])
    return h2d.reshape(batch, s_len, d)
```

```python
import functools
import math

import numpy as np
import jax
import jax.numpy as jnp
from jax import lax
from jax.experimental import pallas as pl
from jax.experimental.pallas import tpu as pltpu

F32 = jnp.float32
BF16 = jnp.bfloat16

RMS_EPS = 1e-6
NEG = -1e30

MOBA_BLOCK = 256
MOBA_TOPK = 3
REL_MAX_DIST = 1024
PEER_TOPK = 16
PEER_KEYS_PER_UNIT = 1

SUBLANES = 8
LANES = 128
VMEM_LIMIT = 56 * 1024 * 1024

_NT = (((1,), (1,)), ((), ()))


def _rms(x, g):
    r = lax.rsqrt(jnp.mean(x * x, axis=-1, keepdims=True) + RMS_EPS)
    return x * r * g


def _norm_matmul_body(x_ref, g_ref, w_ref, o_ref, xn_ref):
    @pl.when(pl.program_id(1) == 0)
    def _():
        xn_ref[...] = _rms(x_ref[...], g_ref[...]).astype(BF16)

    o_ref[...] = jnp.dot(xn_ref[...], w_ref[...], preferred_element_type=F32)


def _norm_matmul(x2d, g, w, tm, tn, name):
    t, d = x2d.shape
    n = w.shape[1]
    return pl.pallas_call(
        _norm_matmul_body,
        out_shape=jax.ShapeDtypeStruct((t, n), F32),
        grid=(t // tm, n // tn),
        in_specs=[
            pl.BlockSpec((tm, d), lambda i, j: (i, 0)),
            pl.BlockSpec((1, d), lambda i, j: (0, 0)),
            pl.BlockSpec((d, tn), lambda i, j: (0, j)),
        ],
        out_specs=pl.BlockSpec((tm, tn), lambda i, j: (i, j)),
        scratch_shapes=[pltpu.VMEM((tm, d), BF16)],
        compiler_params=pltpu.CompilerParams(
            dimension_semantics=("parallel", "arbitrary"), vmem_limit_bytes=VMEM_LIMIT),
        name=name,
    )(x2d, g.reshape(1, d), w)


def _rel_bucket(dist, n_buckets):
    max_exact = n_buckets // 2
    d = jnp.maximum(dist, 0)
    df = jnp.maximum(d, 1).astype(F32)
    large = max_exact + (jnp.log(df / max_exact) / math.log(REL_MAX_DIST / max_exact)
                         * (n_buckets - max_exact)).astype(jnp.int32)
    large = jnp.minimum(large, n_buckets - 1)
    return jnp.where(d < max_exact, d, large)


def _bias_tiles_body(bucket_ref, rel_ref, o_ref, *, n_buckets, n_delta):
    h = pl.program_id(0)
    blk = MOBA_BLOCK
    for delta in range(n_delta):
        bk = bucket_ref[delta:delta + 1, :]
        w = jnp.zeros(bk.shape, F32)
        for b in range(n_buckets):
            w = jnp.where(bk == b, rel_ref[b, h], w)
        x = jnp.broadcast_to(w, (blk, 2 * blk))
        r = pltpu.roll(x, blk, 1, stride=1, stride_axis=0)
        tile = r[:, :blk]
        if delta == 0:
            krow = lax.broadcasted_iota(jnp.int32, (blk, blk), 0)
            qcol = lax.broadcasted_iota(jnp.int32, (blk, blk), 1)
            tile = jnp.where(qcol >= krow, tile, NEG)
        o_ref[0, delta] = tile


def _bias_tiles(rel_bias, n_delta):
    n_buckets, n_heads = rel_bias.shape
    blk = MOBA_BLOCK
    dist = (jnp.arange(n_delta, dtype=jnp.int32)[:, None] * blk - blk
            + jnp.arange(2 * blk, dtype=jnp.int32)[None, :])
    bucket = _rel_bucket(dist, n_buckets).astype(jnp.int32)
    return pl.pallas_call(
        functools.partial(_bias_tiles_body, n_buckets=n_buckets, n_delta=n_delta),
        out_shape=jax.ShapeDtypeStruct((n_heads, n_delta, blk, blk), F32),
        grid=(n_heads,),
        in_specs=[
            pl.BlockSpec((n_delta, 2 * blk), lambda h: (0, 0)),
            pl.BlockSpec(memory_space=pltpu.SMEM),
        ],
        out_specs=pl.BlockSpec((1, n_delta, blk, blk), lambda h: (h, 0, 0, 0)),
        compiler_params=pltpu.CompilerParams(dimension_semantics=("arbitrary",)),
        name="rel_bias_tiles",
    )(bucket, rel_bias)


def _moba_body(q_ref, k_ref, v_ref, qg_ref, kg_ref, bias_ref, o_ref, ot_ref, *, dh):
    s_len = q_ref.shape[1]
    blk = MOBA_BLOCK
    nb = s_len // blk
    scale = dh ** -0.5
    heads_per_step = q_ref.shape[2] // dh
    vt_all = v_ref[0].T
    for hh in range(heads_per_step):
        sl = slice(hh * dh, (hh + 1) * dh)
        qn = _rms(q_ref[0, :, sl], qg_ref[...])
        kn = _rms(k_ref[0, :, sl], kg_ref[...])
        kmean = jnp.mean(kn.reshape(nb, blk, dh), axis=1)
        qb16 = qn.astype(BF16)
        kb16 = kn.astype(BF16)
        vt = vt_all[sl, :].astype(BF16)
        gate_t = lax.dot_general(kmean.astype(BF16), qb16, _NT, preferred_element_type=F32)
        row = lax.broadcasted_iota(jnp.int32, (nb, blk), 0)
        for qb in range(nb):
            qs = slice(qb * blk, (qb + 1) * blk)
            qt = qb16[qs]
            g = gate_t[:, qs]
            rank = jnp.zeros((nb, blk), jnp.int32)
            for j in range(qb):
                gj = g[j:j + 1, :]
                beats = jnp.where(gj > g, 1, jnp.where((gj == g) & (j < row), 1, 0))
                rank = rank + beats
            negm = jnp.where((rank < MOBA_TOPK) & (row < qb), 0.0, NEG)
            parts = []
            for kb in range(qb + 1):
                st = lax.dot_general(kb16[kb * blk:(kb + 1) * blk], qt, _NT,
                                     preferred_element_type=F32)
                st = st * scale + bias_ref[hh, qb - kb]
                if kb < qb:
                    st = st + negm[kb:kb + 1, :]
                parts.append(st)
            logit = jnp.concatenate(parts, axis=0) if len(parts) > 1 else parts[0]
            m = jnp.max(logit, axis=0, keepdims=True)
            p = jnp.exp(logit - m)
            denom = jnp.sum(p, axis=0, keepdims=True)
            nk = (qb + 1) * blk
            o_t = jnp.dot(vt[:, :nk], p.astype(BF16), preferred_element_type=F32)
            ot_ref[sl, qs] = o_t / denom
    o_ref[0] = ot_ref[...].T.astype(o_ref.dtype)


def _moba(proj3, q_gain, k_gain, bias_t, n_heads, dh):
    b, s_len, _ = proj3.shape
    hps = LANES // dh
    width = n_heads * dh
    nblk = width // LANES
    n_delta = bias_t.shape[1]
    blk = MOBA_BLOCK
    return pl.pallas_call(
        functools.partial(_moba_body, dh=dh),
        out_shape=jax.ShapeDtypeStruct((b, s_len, width), BF16),
        grid=(n_heads // hps, b),
        in_specs=[
            pl.BlockSpec((1, s_len, LANES), lambda hp, bi: (bi, 0, hp)),
            pl.BlockSpec((1, s_len, LANES), lambda hp, bi: (bi, 0, nblk + hp)),
            pl.BlockSpec((1, s_len, LANES), lambda hp, bi: (bi, 0, 2 * nblk + hp)),
            pl.BlockSpec((1, dh), lambda hp, bi: (0, 0)),
            pl.BlockSpec((1, dh), lambda hp, bi: (0, 0)),
            pl.BlockSpec((hps, n_delta, blk, blk), lambda hp, bi: (hp, 0, 0, 0)),
        ],
        out_specs=pl.BlockSpec((1, s_len, LANES), lambda hp, bi: (bi, 0, hp)),
        scratch_shapes=[pltpu.VMEM((LANES, s_len), F32)],
        compiler_params=pltpu.CompilerParams(
            dimension_semantics=("parallel", "arbitrary"), vmem_limit_bytes=VMEM_LIMIT),
        name="moba_attention",
    )(proj3, proj3, proj3, q_gain.reshape(1, dh), k_gain.reshape(1, dh), bias_t)


def _xattn_body(q_ref, kv_ref, qg_ref, kg_ref, o_ref, *, dh, chunk):
    s_len = q_ref.shape[1]
    width = q_ref.shape[2]
    n_heads = width // dh
    scale = dh ** -0.5
    for h in range(n_heads):
        sl = slice(h * dh, (h + 1) * dh)
        kn = _rms(kv_ref[0, :, sl], kg_ref[...]).astype(BF16)
        v = kv_ref[0, :, width + h * dh:width + (h + 1) * dh].astype(BF16)
        for c in range(s_len // chunk):
            rs = slice(c * chunk, (c + 1) * chunk)
            qn = _rms(q_ref[0, rs, sl], qg_ref[...]).astype(BF16)
            s = lax.dot_general(qn, kn, _NT, preferred_element_type=F32) * scale
            m = jnp.max(s, axis=-1, keepdims=True)
            p = jnp.exp(s - m)
            denom = jnp.sum(p, axis=-1, keepdims=True)
            o = jnp.dot(p.astype(BF16), v, preferred_element_type=F32) / denom
            o_ref[0, rs, sl] = o.astype(o_ref.dtype)


def _xattn(proj3, q_col_block, kv3, q_gain, k_gain, dh):
    b, s_len, _ = proj3.shape
    m_len, kv_width = kv3.shape[1], kv3.shape[2]
    width = kv_width // 2
    return pl.pallas_call(
        functools.partial(_xattn_body, dh=dh, chunk=512),
        out_shape=jax.ShapeDtypeStruct((b, s_len, width), BF16),
        grid=(b,),
        in_specs=[
            pl.BlockSpec((1, s_len, width), lambda bi: (bi, 0, q_col_block)),
            pl.BlockSpec((1, m_len, kv_width), lambda bi: (bi, 0, 0)),
            pl.BlockSpec((1, dh), lambda bi: (0, 0)),
            pl.BlockSpec((1, dh), lambda bi: (0, 0)),
        ],
        out_specs=pl.BlockSpec((1, s_len, width), lambda bi: (bi, 0, 0)),
        compiler_params=pltpu.CompilerParams(
            dimension_semantics=("parallel",), vmem_limit_bytes=VMEM_LIMIT),
        name="cross_attention",
    )(proj3, kv3, q_gain.reshape(1, dh), k_gain.reshape(1, dh))


def _merge_body(x_ref, ya_ref, u_ref, b_ref, c_ref, uh_ref, ch_ref, yx_ref, gt_ref, cw_ref,
                wa_ref, wc_ref, wx_ref, wo_ref, o_ref, *, tiles_per_seq):
    i = pl.program_id(0)
    tm, d = x_ref.shape
    z = c_ref[...] * u_ref[...]
    zh = ch_ref[...] * uh_ref[...]
    zh = jnp.where(i % tiles_per_seq == 0, 0.0, zh)
    row = lax.broadcasted_iota(jnp.int32, z.shape, 0)
    z1 = pltpu.roll(z, 1, 0)
    z1 = jnp.where(row == 0, zh[7:8, :], z1)
    z2 = pltpu.roll(z, 2, 0)
    z2 = jnp.where(row == 0, zh[6:7, :], jnp.where(row == 1, zh[7:8, :], z2))
    conv = cw_ref[0:1, :] * z2 + cw_ref[1:2, :] * z1 + cw_ref[2:3, :] * z
    y_c = jnp.dot((b_ref[...] * conv).astype(BF16), wc_ref[...], preferred_element_type=F32)
    y_a = jnp.dot(ya_ref[...], wa_ref[...], preferred_element_type=F32)
    y_x = jnp.dot(yx_ref[...], wx_ref[...], preferred_element_type=F32)
    merged = (jax.nn.sigmoid(gt_ref[:, 0:d]) * y_a
              + jax.nn.sigmoid(gt_ref[:, d:2 * d]) * y_c
              + jax.nn.sigmoid(gt_ref[:, 2 * d:3 * d]) * y_x)
    o_ref[...] = x_ref[...] + jnp.dot(merged.astype(BF16), wo_ref[...], preferred_element_type=F32)


def _merge(x2d, y_a, proj, conv_cols, y_x, gates, conv_w, w_a, w_c, w_x, w_o, s_len, tm):
    t, d = x2d.shape
    cwid = conv_w.shape[1]
    cu, cb, cc = conv_cols
    halo = SUBLANES
    rows_per_tile = tm // halo

    def halo_map(col):
        return lambda i: (jnp.maximum(i * rows_per_tile - 1, 0), col)

    full = lambda shape: pl.BlockSpec(shape, lambda i: (0, 0))
    return pl.pallas_call(
        functools.partial(_merge_body, tiles_per_seq=s_len // tm),
        out_shape=jax.ShapeDtypeStruct((t, d), F32),
        grid=(t // tm,),
        in_specs=[
            pl.BlockSpec((tm, d), lambda i: (i, 0)),
            pl.BlockSpec((tm, y_a.shape[1]), lambda i: (i, 0)),
            pl.BlockSpec((tm, cwid), lambda i: (i, cu)),
            pl.BlockSpec((tm, cwid), lambda i: (i, cb)),
            pl.BlockSpec((tm, cwid), lambda i: (i, cc)),
            pl.BlockSpec((halo, cwid), halo_map(cu)),
            pl.BlockSpec((halo, cwid), halo_map(cc)),
            pl.BlockSpec((tm, y_x.shape[1]), lambda i: (i, 0)),
            pl.BlockSpec((tm, gates.shape[1]), lambda i: (i, 0)),
            full(conv_w.shape), full(w_a.shape), full(w_c.shape), full(w_x.shape), full(w_o.shape),
        ],
        out_specs=pl.BlockSpec((tm, d), lambda i: (i, 0)),
        compiler_params=pltpu.CompilerParams(
            dimension_semantics=("parallel",), vmem_limit_bytes=VMEM_LIMIT),
        name="branch_merge",
    )(x2d, y_a, proj, proj, proj, proj, proj, y_x, gates, conv_w, w_a, w_c, w_x, w_o)


def _oddeven_merge_sort_pairs(n):
    pairs = []
    p = 1
    while p < n:
        k = p
        while k >= 1:
            for j in range(k % p, n - k, 2 * k):
                for i in range(min(k, n - j - k)):
                    if (i + j) // (2 * p) == (i + j + k) // (2 * p):
                        pairs.append((i + j, i + j + k))
            k //= 2
        p *= 2
    return pairs


def _bitonic_merge_pairs(n):
    pairs = []
    d = n // 2
    while d >= 1:
        pairs += [(i, i + d) for i in range(n) if i & d == 0]
        d //= 2
    return pairs


def _compare_exchange(slabs, pairs):
    for a, b in pairs:
        hi = jnp.maximum(slabs[a], slabs[b])
        lo = jnp.minimum(slabs[a], slabs[b])
        slabs[a], slabs[b] = hi, lo
    return slabs


def _top_sorted(slabs, k):
    n = len(slabs)
    slabs = _compare_exchange(list(slabs), _oddeven_merge_sort_pairs(n))
    merge = _bitonic_merge_pairs(k)
    for shift in (4, 2, 1):
        rolled = [pltpu.roll(s, shift, 0) for s in slabs]
        if len(slabs) < k:
            slabs = slabs + rolled[::-1]
        else:
            slabs = [jnp.maximum(slabs[i], rolled[k - 1 - i]) for i in range(k)]
        slabs = _compare_exchange(slabs, merge)
    return slabs


def _hyperbola(k):
    return [(i, j) for i in range(k) for j in range(k) if (i + 1) * (j + 1) <= k]


def _peer_scores(h, keys_ref, qry_ref, r2_ref, e1_ref, e2_ref, cnt_ref, *, tt, lane_w):
    k = PEER_TOPK
    nkeys = keys_ref.shape[2]
    pairs = _hyperbola(k)
    n_slab = -(-len(pairs) // SUBLANES)
    n_slab = k // 2 if n_slab <= k // 2 else k
    sub = lax.broadcasted_iota(jnp.int32, (SUBLANES, lane_w), 0)
    inf = float("inf")
    for ts in range(tt // lane_w):
        cols = slice(ts * lane_w, (ts + 1) * lane_w)
        q1 = qry_ref[cols, pl.ds(pl.multiple_of(2 * h * nkeys, nkeys), nkeys)]
        q2 = qry_ref[cols, pl.ds(pl.multiple_of((2 * h + 1) * nkeys, nkeys), nkeys)]
        s1 = lax.dot_general(keys_ref[h, 0], q1, _NT, preferred_element_type=F32)
        s2 = lax.dot_general(keys_ref[h, 1], q2, _NT, preferred_element_type=F32)
        a = _top_sorted([s1[SUBLANES * v:SUBLANES * (v + 1)] for v in range(nkeys // SUBLANES)], k)
        b = _top_sorted([s2[SUBLANES * v:SUBLANES * (v + 1)] for v in range(nkeys // SUBLANES)], k)
        sums = {(i, j): a[i] + b[j] for (i, j) in pairs}
        packed = []
        for v in range(n_slab):
            slab = jnp.full((SUBLANES, lane_w), -inf, F32)
            for s in range(SUBLANES):
                idx = v * SUBLANES + s
                if idx < len(pairs):
                    slab = jnp.where(sub == s, sums[pairs[idx]], slab)
            packed.append(slab)
        tau = _top_sorted(packed, k)[k - 1]
        top = a[0] + b[0]
        z = jnp.zeros((SUBLANES, lane_w), F32)
        for slab in packed:
            z = z + jnp.where(slab >= tau, jnp.exp(slab - top), 0.0)
        zsum = jnp.sum(z, axis=0, keepdims=True)
        cnt = jnp.zeros((nkeys, lane_w), F32)
        for i in range(k - 1, -1, -1):
            c = jnp.zeros((SUBLANES, lane_w), F32)
            for j in range(k):
                if (i, j) in sums:
                    c = c + jnp.where(sums[(i, j)] >= tau, 1.0, 0.0)
            cnt = jnp.where(s1 >= a[i][0:1, :], c[0:1, :], cnt)
        rank2 = jnp.zeros((nkeys, lane_w), F32)
        for r in range(k):
            rank2 = rank2 + jnp.where(b[r][0:1, :] > s2, 1.0, 0.0)
        cnt_ref[h, ts] = cnt
        r2_ref[h, ts] = rank2.astype(BF16)
        e1_ref[h, ts] = jnp.exp(s1 - a[0][0:1, :])
        e2_ref[h, ts] = (jnp.exp(s2 - b[0][0:1, :]) / zsum).astype(BF16)


def _peer_weights(key_blk, key_off, col0, ncols, act_ref, p_ref, r2_ref, e1_ref, e2_ref, cnt_ref, *, lane_w):
    n_heads, _, nkeys, _ = r2_ref.shape
    ec = act_ref.shape[0]
    key_rows = pl.ds(pl.multiple_of(key_blk * SUBLANES, SUBLANES), SUBLANES)

    def unit(ts, ils):
        cols = slice(ts * lane_w, (ts + 1) * lane_w)
        w = [jnp.zeros((nkeys, lane_w), BF16) for _ in ils]
        for h in range(n_heads):
            r2 = r2_ref[h, ts]
            e2 = e2_ref[h, ts]
            cnt8 = cnt_ref[h, ts, key_rows, :]
            e18 = e1_ref[h, ts, key_rows, :]
            for n, il in enumerate(ils):
                r = key_off + il
                cnt = cnt8[r:r + 1, :].astype(BF16)
                e1 = e18[r:r + 1, :].astype(BF16)
                w[n] = w[n] + jnp.where(r2 < cnt, e2, jnp.zeros((), BF16)) * e1
        for n, il in enumerate(ils):
            rows = slice(il * nkeys, (il + 1) * nkeys)
            a = act_ref[rows, cols]
            gelu = 0.5 * a * (1.0 + lax.erf(a * (2.0 ** -0.5)))
            p_ref[rows, cols] = w[n] * gelu.astype(BF16)

    n_il = ec // nkeys
    groups = [tuple(range(g, g + PEER_KEYS_PER_UNIT)) for g in range(0, n_il, PEER_KEYS_PER_UNIT)]
    return [functools.partial(unit, ts, ils) for ts in range(col0 // lane_w, (col0 + ncols) // lane_w)
            for ils in groups]


def _peer_body(h_ref, g_ref, wq_ref, keys_ref, u_ref, vt_ref, o_ref,
               xn_ref, qry_ref, r2_ref, e1_ref, e2_ref, cnt_ref,
               act0_ref, act1_ref, p0_ref, p1_ref, acc_ref, *, lane_w, n_chunks):
    s = pl.program_id(1)
    tt = h_ref.shape[0]
    n_heads = keys_ref.shape[0]

    @pl.when(s == 0)
    def _():
        xn = _rms(h_ref[...], g_ref[...]).astype(BF16)
        xn_ref[...] = xn
        qry_ref[...] = jnp.dot(xn, wq_ref[...], preferred_element_type=F32).astype(BF16)

        def head(h, carry):
            _peer_scores(h, keys_ref, qry_ref, r2_ref, e1_ref, e2_ref, cnt_ref, tt=tt, lane_w=lane_w)
            return carry

        lax.fori_loop(0, n_heads, head, 0)
        acc_ref[...] = jnp.zeros(acc_ref.shape, F32)
        act1_ref[...] = jnp.zeros(act1_ref.shape, F32)
        p0_ref[...] = jnp.zeros(p0_ref.shape, BF16)

    keys_per_chunk = u_ref.shape[0] // keys_ref.shape[2]
    assert 2 * keys_per_chunk == SUBLANES
    key_blk = jnp.clip((s - 1) // 2, 0, n_chunks // 2 - 1)

    def step(act_w, act_r, p_w, p_r, key_off):
        half = tt // 2
        d = u_ref.shape[1]
        ec = u_ref.shape[0]
        kq = d // 4
        for c0 in (0, half):
            cols = slice(c0, c0 + half)
            units = _peer_weights(key_blk, key_off, c0, half, act_r, p_w, r2_ref, e1_ref, e2_ref, cnt_ref,
                                  lane_w=lane_w)
            def act_part(q):
                return lax.dot_general(u_ref[:, q * kq:(q + 1) * kq], xn_ref[cols, q * kq:(q + 1) * kq],
                                       _NT, preferred_element_type=F32)

            def out_part(q):
                ks = slice(q * (ec // 2), (q + 1) * (ec // 2))
                return jnp.dot(vt_ref[0, :, ks], p_r[ks, cols], preferred_element_type=F32)

            granules = [(act_part, 0), (out_part, 0), (act_part, 1), (act_part, 2), (out_part, 1), (act_part, 3)]
            sums = {act_part: None, out_part: None}
            done = 0
            for n, unit in enumerate(units + [None]):
                while done < len(granules) and done * len(units) <= n * len(granules):
                    fn, q = granules[done]
                    part = fn(q)
                    sums[fn] = part if sums[fn] is None else sums[fn] + part
                    done += 1
                if unit is not None:
                    unit()
            act_w[:, cols] = sums[act_part]
            acc_ref[:, cols] += sums[out_part]

    @pl.when(s % 2 == 0)
    def _():
        step(act0_ref, act1_ref, p1_ref, p0_ref, keys_per_chunk)

    @pl.when(s % 2 == 1)
    def _():
        step(act1_ref, act0_ref, p0_ref, p1_ref, 0)

    @pl.when(s == n_chunks + 1)
    def _():
        o_ref[...] = h_ref[...] + acc_ref[...].T


def _peer(h2d, g, w_query, keys, u, v, tt, ec):
    t, d = h2d.shape
    v_t = v.reshape(u.shape[0] // ec, ec, d).transpose(0, 2, 1)
    n_heads, _, nkeys, dk = keys.shape
    n_chunks = u.shape[0] // ec
    qw = w_query.shape[1]
    stat = lambda dt: pltpu.VMEM((n_heads, tt // LANES, nkeys, LANES), dt)
    return pl.pallas_call(
        functools.partial(_peer_body, lane_w=LANES, n_chunks=n_chunks),
        out_shape=jax.ShapeDtypeStruct((t, d), F32),
        grid=(t // tt, n_chunks + 2),
        in_specs=[
            pl.BlockSpec((tt, d), lambda i, s: (i, 0)),
            pl.BlockSpec((1, d), lambda i, s: (0, 0)),
            pl.BlockSpec((d, qw), lambda i, s: (0, 0)),
            pl.BlockSpec(keys.shape, lambda i, s: (0, 0, 0, 0)),
            pl.BlockSpec((ec, d), lambda i, s: (jnp.minimum(s, n_chunks - 1), 0)),
            pl.BlockSpec((1, d, ec), lambda i, s: (jnp.clip(s - 2, 0, n_chunks - 1), 0, 0)),
        ],
        out_specs=pl.BlockSpec((tt, d), lambda i, s: (i, 0)),
        scratch_shapes=[
            pltpu.VMEM((tt, d), BF16),
            pltpu.VMEM((tt, qw), BF16),
            stat(BF16), stat(F32), stat(BF16), stat(F32),
            pltpu.VMEM((ec, tt), F32), pltpu.VMEM((ec, tt), F32),
            pltpu.VMEM((ec, tt), BF16), pltpu.VMEM((ec, tt), BF16),
            pltpu.VMEM((d, tt), F32),
        ],
        compiler_params=pltpu.CompilerParams(
            dimension_semantics=("parallel", "arbitrary"), vmem_limit_bytes=VMEM_LIMIT),
        name="peer",
    )(h2d, g.reshape(1, d), w_query, keys, u, v_t)


def _layer(h2d, mem2d, batch, g_mix, w_in, q_gain_a, k_gain_a, bias_t, conv_w, g_mem, w_mem_kv,
           q_gain_x, k_gain_x, w_br_attn, w_br_conv, w_br_x, w_out, g_ffn,
           peer_w_query, peer_sub_keys, peer_u, peer_v):
    t, d = h2d.shape
    s_len = t // batch
    n_heads_a = bias_t.shape[0]
    dh_a = q_gain_a.shape[0]
    attn_w = n_heads_a * dh_a
    conv_wid = conv_w.shape[1]
    dh_x = q_gain_x.shape[0]
    xattn_w = w_mem_kv.shape[1] // 2
    assert attn_w == conv_wid == xattn_w, "column blocks of the projection must have equal width"
    branch_cols = 3 * attn_w + 3 * conv_wid + xattn_w

    w_in16 = w_in.astype(BF16)
    proj = _norm_matmul(h2d, g_mix, w_in16[:, :branch_cols], 512, branch_cols // 2, "proj_branches")
    gates = _norm_matmul(h2d, g_mix, w_in16[:, branch_cols:], 512, d, "proj_gates")
    proj3 = proj.reshape(batch, s_len, branch_cols)

    y_a = _moba(proj3, q_gain_a, k_gain_a, bias_t, n_heads_a, dh_a).reshape(t, attn_w)

    kv = _norm_matmul(mem2d, g_mem, w_mem_kv.astype(BF16), 512, w_mem_kv.shape[1], "mem_kv")
    kv3 = kv.reshape(batch, mem2d.shape[0] // batch, kv.shape[1])
    y_x = _xattn(proj3, 6, kv3, q_gain_x, k_gain_x, dh_x).reshape(t, xattn_w)

    h2d = _merge(h2d, y_a, proj, (3, 4, 5), y_x, gates, conv_w,
                 w_br_attn.astype(BF16), w_br_conv.astype(BF16), w_br_x.astype(BF16),
                 w_out.astype(BF16), s_len, 512)

    return _peer(h2d, g_ffn, peer_w_query.astype(BF16), peer_sub_keys.astype(BF16),
                 peer_u.astype(BF16), peer_v.astype(BF16), 512, 512)


def kernel(x, mem, g_mix, w_in, q_gain_a, k_gain_a, rel_bias, conv_w, g_mem, w_mem_kv, q_gain_x, k_gain_x, w_br_attn, w_br_conv, w_br_x, w_out, g_ffn, peer_w_query, peer_sub_keys, peer_u, peer_v):
    batch, s_len, d = x.shape
    depth = g_mix.shape[0]
    bias_t = _bias_tiles(rel_bias, s_len // MOBA_BLOCK)
    h2d = x.reshape(batch * s_len, d)
    mem2d = mem.reshape(batch * mem.shape[1], d)
    for l in range(depth):
        h2d = _layer(h2d, mem2d, batch, g_mix[l], w_in[l], q_gain_a[l], k_gain_a[l], bias_t, conv_w[l],
                     g_mem[l], w_mem_kv[l], q_gain_x[l], k_gain_x[l], w_br_attn[l], w_br_conv[l],
                     w_br_x[l], w_out[l], g_ffn[l], peer_w_query[l], peer_sub_keys[l],
                     peer_u[l], peer_v[l])
    return h2d.reshape(batch, s_len, d)
```

```python
import functools
import math

import numpy as np
import jax
import jax.numpy as jnp
from jax import lax
from jax.experimental import pallas as pl
from jax.experimental.pallas import tpu as pltpu

F32 = jnp.float32
BF16 = jnp.bfloat16

RMS_EPS = 1e-6
NEG = -1e30

MOBA_BLOCK = 256
MOBA_TOPK = 3
REL_MAX_DIST = 1024
PEER_TOPK = 16
PEER_KEYS_PER_UNIT = 1

SUBLANES = 8
LANES = 128
VMEM_LIMIT = 56 * 1024 * 1024

_NT = (((1,), (1,)), ((), ()))


def _rms(x, g):
    r = lax.rsqrt(jnp.mean(x * x, axis=-1, keepdims=True) + RMS_EPS)
    return x * r * g


def _norm_matmul_body(x_ref, g_ref, w_ref, o_ref, xn_ref):
    @pl.when(pl.program_id(1) == 0)
    def _():
        xn_ref[...] = _rms(x_ref[...], g_ref[...]).astype(BF16)

    o_ref[...] = jnp.dot(xn_ref[...], w_ref[...], preferred_element_type=F32)


def _norm_matmul(x2d, g, w, tm, tn, name):
    t, d = x2d.shape
    n = w.shape[1]
    return pl.pallas_call(
        _norm_matmul_body,
        out_shape=jax.ShapeDtypeStruct((t, n), F32),
        grid=(t // tm, n // tn),
        in_specs=[
            pl.BlockSpec((tm, d), lambda i, j: (i, 0)),
            pl.BlockSpec((1, d), lambda i, j: (0, 0)),
            pl.BlockSpec((d, tn), lambda i, j: (0, j)),
        ],
        out_specs=pl.BlockSpec((tm, tn), lambda i, j: (i, j)),
        scratch_shapes=[pltpu.VMEM((tm, d), BF16)],
        compiler_params=pltpu.CompilerParams(
            dimension_semantics=("parallel", "arbitrary"), vmem_limit_bytes=VMEM_LIMIT),
        name=name,
    )(x2d, g.reshape(1, d), w)


def _rel_bucket(dist, n_buckets):
    max_exact = n_buckets // 2
    d = jnp.maximum(dist, 0)
    df = jnp.maximum(d, 1).astype(F32)
    large = max_exact + (jnp.log(df / max_exact) / math.log(REL_MAX_DIST / max_exact)
                         * (n_buckets - max_exact)).astype(jnp.int32)
    large = jnp.minimum(large, n_buckets - 1)
    return jnp.where(d < max_exact, d, large)


def _bias_tiles_body(bucket_ref, rel_ref, o_ref, *, n_buckets, n_delta):
    h = pl.program_id(0)
    blk = MOBA_BLOCK
    for delta in range(n_delta):
        bk = bucket_ref[delta:delta + 1, :]
        w = jnp.zeros(bk.shape, F32)
        for b in range(n_buckets):
            w = jnp.where(bk == b, rel_ref[b, h], w)
        x = jnp.broadcast_to(w, (blk, 2 * blk))
        r = pltpu.roll(x, blk, 1, stride=1, stride_axis=0)
        tile = r[:, :blk]
        if delta == 0:
            krow = lax.broadcasted_iota(jnp.int32, (blk, blk), 0)
            qcol = lax.broadcasted_iota(jnp.int32, (blk, blk), 1)
            tile = jnp.where(qcol >= krow, tile, NEG)
        o_ref[0, delta] = tile


def _bias_tiles(rel_bias, n_delta):
    n_buckets, n_heads = rel_bias.shape
    blk = MOBA_BLOCK
    dist = (jnp.arange(n_delta, dtype=jnp.int32)[:, None] * blk - blk
            + jnp.arange(2 * blk, dtype=jnp.int32)[None, :])
    bucket = _rel_bucket(dist, n_buckets).astype(jnp.int32)
    return pl.pallas_call(
        functools.partial(_bias_tiles_body, n_buckets=n_buckets, n_delta=n_delta),
        out_shape=jax.ShapeDtypeStruct((n_heads, n_delta, blk, blk), F32),
        grid=(n_heads,),
        in_specs=[
            pl.BlockSpec((n_delta, 2 * blk), lambda h: (0, 0)),
            pl.BlockSpec(memory_space=pltpu.SMEM),
        ],
        out_specs=pl.BlockSpec((1, n_delta, blk, blk), lambda h: (h, 0, 0, 0)),
        compiler_params=pltpu.CompilerParams(dimension_semantics=("arbitrary",)),
        name="rel_bias_tiles",
    )(bucket, rel_bias)


def _moba_body(q_ref, k_ref, v_ref, qg_ref, kg_ref, bias_ref, o_ref, ot_ref, *, dh):
    s_len = q_ref.shape[1]
    blk = MOBA_BLOCK
    nb = s_len // blk
    scale = dh ** -0.5
    heads_per_step = q_ref.shape[2] // dh
    vt_all = v_ref[0].T
    for hh in range(heads_per_step):
        sl = slice(hh * dh, (hh + 1) * dh)
        qn = _rms(q_ref[0, :, sl], qg_ref[...])
        kn = _rms(k_ref[0, :, sl], kg_ref[...])
        kmean = jnp.mean(kn.reshape(nb, blk, dh), axis=1)
        qb16 = qn.astype(BF16)
        kb16 = kn.astype(BF16)
        vt = vt_all[sl, :].astype(BF16)
        gate_t = lax.dot_general(kmean.astype(BF16), qb16, _NT, preferred_element_type=F32)
        row = lax.broadcasted_iota(jnp.int32, (nb, blk), 0)
        for qb in range(nb):
            qs = slice(qb * blk, (qb + 1) * blk)
            qt = qb16[qs]
            g = gate_t[:, qs]
            rank = jnp.zeros((nb, blk), jnp.int32)
            for j in range(qb):
                gj = g[j:j + 1, :]
                beats = jnp.where(gj > g, 1, jnp.where((gj == g) & (j < row), 1, 0))
                rank = rank + beats
            negm = jnp.where((rank < MOBA_TOPK) & (row < qb), 0.0, NEG)
            parts = []
            for kb in range(qb + 1):
                st = lax.dot_general(kb16[kb * blk:(kb + 1) * blk], qt, _NT,
                                     preferred_element_type=F32)
                st = st * scale + bias_ref[hh, qb - kb]
                if kb < qb:
                    st = st + negm[kb:kb + 1, :]
                parts.append(st)
            logit = jnp.concatenate(parts, axis=0) if len(parts) > 1 else parts[0]
            m = jnp.max(logit, axis=0, keepdims=True)
            p = jnp.exp(logit - m)
            denom = jnp.sum(p, axis=0, keepdims=True)
            nk = (qb + 1) * blk
            o_t = jnp.dot(vt[:, :nk], p.astype(BF16), preferred_element_type=F32)
            ot_ref[sl, qs] = o_t / denom
    o_ref[0] = ot_ref[...].T.astype(o_ref.dtype)


def _moba(proj3, q_gain, k_gain, bias_t, n_heads, dh):
    b, s_len, _ = proj3.shape
    hps = LANES // dh
    width = n_heads * dh
    nblk = width // LANES
    n_delta = bias_t.shape[1]
    blk = MOBA_BLOCK
    return pl.pallas_call(
        functools.partial(_moba_body, dh=dh),
        out_shape=jax.ShapeDtypeStruct((b, s_len, width), BF16),
        grid=(n_heads // hps, b),
        in_specs=[
            pl.BlockSpec((1, s_len, LANES), lambda hp, bi: (bi, 0, hp)),
            pl.BlockSpec((1, s_len, LANES), lambda hp, bi: (bi, 0, nblk + hp)),
            pl.BlockSpec((1, s_len, LANES), lambda hp, bi: (bi, 0, 2 * nblk + hp)),
            pl.BlockSpec((1, dh), lambda hp, bi: (0, 0)),
            pl.BlockSpec((1, dh), lambda hp, bi: (0, 0)),
            pl.BlockSpec((hps, n_delta, blk, blk), lambda hp, bi: (hp, 0, 0, 0)),
        ],
        out_specs=pl.BlockSpec((1, s_len, LANES), lambda hp, bi: (bi, 0, hp)),
        scratch_shapes=[pltpu.VMEM((LANES, s_len), F32)],
        compiler_params=pltpu.CompilerParams(
            dimension_semantics=("parallel", "arbitrary"), vmem_limit_bytes=VMEM_LIMIT),
        name="moba_attention",
    )(proj3, proj3, proj3, q_gain.reshape(1, dh), k_gain.reshape(1, dh), bias_t)


def _xattn_body(q_ref, kv_ref, qg_ref, kg_ref, o_ref, *, dh, chunk):
    s_len = q_ref.shape[1]
    width = q_ref.shape[2]
    n_heads = width // dh
    scale = dh ** -0.5
    for h in range(n_heads):
        sl = slice(h * dh, (h + 1) * dh)
        kn = _rms(kv_ref[0, :, sl], kg_ref[...]).astype(BF16)
        v = kv_ref[0, :, width + h * dh:width + (h + 1) * dh].astype(BF16)
        for c in range(s_len // chunk):
            rs = slice(c * chunk, (c + 1) * chunk)
            qn = _rms(q_ref[0, rs, sl], qg_ref[...]).astype(BF16)
            s = lax.dot_general(qn, kn, _NT, preferred_element_type=F32) * scale
            m = jnp.max(s, axis=-1, keepdims=True)
            p = jnp.exp(s - m)
            denom = jnp.sum(p, axis=-1, keepdims=True)
            o = jnp.dot(p.astype(BF16), v, preferred_element_type=F32) / denom
            o_ref[0, rs, sl] = o.astype(o_ref.dtype)


def _xattn(proj3, q_col_block, kv3, q_gain, k_gain, dh):
    b, s_len, _ = proj3.shape
    m_len, kv_width = kv3.shape[1], kv3.shape[2]
    width = kv_width // 2
    return pl.pallas_call(
        functools.partial(_xattn_body, dh=dh, chunk=512),
        out_shape=jax.ShapeDtypeStruct((b, s_len, width), BF16),
        grid=(b,),
        in_specs=[
            pl.BlockSpec((1, s_len, width), lambda bi: (bi, 0, q_col_block)),
            pl.BlockSpec((1, m_len, kv_width), lambda bi: (bi, 0, 0)),
            pl.BlockSpec((1, dh), lambda bi: (0, 0)),
            pl.BlockSpec((1, dh), lambda bi: (0, 0)),
        ],
        out_specs=pl.BlockSpec((1, s_len, width), lambda bi: (bi, 0, 0)),
        compiler_params=pltpu.CompilerParams(
            dimension_semantics=("parallel",), vmem_limit_bytes=VMEM_LIMIT),
        name="cross_attention",
    )(proj3, kv3, q_gain.reshape(1, dh), k_gain.reshape(1, dh))


def _merge_body(x_ref, ya_ref, u_ref, b_ref, c_ref, uh_ref, ch_ref, yx_ref, gt_ref, cw_ref,
                wa_ref, wc_ref, wx_ref, wo_ref, o_ref, *, tiles_per_seq):
    i = pl.program_id(0)
    tm, d = x_ref.shape
    z = c_ref[...] * u_ref[...]
    zh = ch_ref[...] * uh_ref[...]
    zh = jnp.where(i % tiles_per_seq == 0, 0.0, zh)
    row = lax.broadcasted_iota(jnp.int32, z.shape, 0)
    z1 = pltpu.roll(z, 1, 0)
    z1 = jnp.where(row == 0, zh[7:8, :], z1)
    z2 = pltpu.roll(z, 2, 0)
    z2 = jnp.where(row == 0, zh[6:7, :], jnp.where(row == 1, zh[7:8, :], z2))
    conv = cw_ref[0:1, :] * z2 + cw_ref[1:2, :] * z1 + cw_ref[2:3, :] * z
    y_c = jnp.dot((b_ref[...] * conv).astype(BF16), wc_ref[...], preferred_element_type=F32)
    y_a = jnp.dot(ya_ref[...], wa_ref[...], preferred_element_type=F32)
    y_x = jnp.dot(yx_ref[...], wx_ref[...], preferred_element_type=F32)
    merged = (jax.nn.sigmoid(gt_ref[:, 0:d]) * y_a
              + jax.nn.sigmoid(gt_ref[:, d:2 * d]) * y_c
              + jax.nn.sigmoid(gt_ref[:, 2 * d:3 * d]) * y_x)
    o_ref[...] = x_ref[...] + jnp.dot(merged.astype(BF16), wo_ref[...], preferred_element_type=F32)


def _merge(x2d, y_a, proj, conv_cols, y_x, gates, conv_w, w_a, w_c, w_x, w_o, s_len, tm):
    t, d = x2d.shape
    cwid = conv_w.shape[1]
    cu, cb, cc = conv_cols
    halo = SUBLANES
    rows_per_tile = tm // halo

    def halo_map(col):
        return lambda i: (jnp.maximum(i * rows_per_tile - 1, 0), col)

    full = lambda shape: pl.BlockSpec(shape, lambda i: (0, 0))
    return pl.pallas_call(
        functools.partial(_merge_body, tiles_per_seq=s_len // tm),
        out_shape=jax.ShapeDtypeStruct((t, d), F32),
        grid=(t // tm,),
        in_specs=[
            pl.BlockSpec((tm, d), lambda i: (i, 0)),
            pl.BlockSpec((tm, y_a.shape[1]), lambda i: (i, 0)),
            pl.BlockSpec((tm, cwid), lambda i: (i, cu)),
            pl.BlockSpec((tm, cwid), lambda i: (i, cb)),
            pl.BlockSpec((tm, cwid), lambda i: (i, cc)),
            pl.BlockSpec((halo, cwid), halo_map(cu)),
            pl.BlockSpec((halo, cwid), halo_map(cc)),
            pl.BlockSpec((tm, y_x.shape[1]), lambda i: (i, 0)),
            pl.BlockSpec((tm, gates.shape[1]), lambda i: (i, 0)),
            full(conv_w.shape), full(w_a.shape), full(w_c.shape), full(w_x.shape), full(w_o.shape),
        ],
        out_specs=pl.BlockSpec((tm, d), lambda i: (i, 0)),
        compiler_params=pltpu.CompilerParams(
            dimension_semantics=("parallel",), vmem_limit_bytes=VMEM_LIMIT),
        name="branch_merge",
    )(x2d, y_a, proj, proj, proj, proj, proj, y_x, gates, conv_w, w_a, w_c, w_x, w_o)


def _oddeven_merge_sort_pairs(n):
    pairs = []
    p = 1
    while p < n:
        k = p
        while k >= 1:
            for j in range(k % p, n - k, 2 * k):
                for i in range(min(k, n - j - k)):
                    if (i + j) // (2 * p) == (i + j + k) // (2 * p):
                        pairs.append((i + j, i + j + k))
            k //= 2
        p *= 2
    return pairs


def _bitonic_merge_pairs(n):
    pairs = []
    d = n // 2
    while d >= 1:
        pairs += [(i, i + d) for i in range(n) if i & d == 0]
        d //= 2
    return pairs


def _compare_exchange(slabs, pairs):
    for a, b in pairs:
        hi = jnp.maximum(slabs[a], slabs[b])
        lo = jnp.minimum(slabs[a], slabs[b])
        slabs[a], slabs[b] = hi, lo
    return slabs


def _top_sorted(slabs, k):
    n = len(slabs)
    slabs = _compare_exchange(list(slabs), _oddeven_merge_sort_pairs(n))
    merge = _bitonic_merge_pairs(k)
    for shift in (4, 2, 1):
        rolled = [pltpu.roll(s, shift, 0) for s in slabs]
        if len(slabs) < k:
            slabs = slabs + rolled[::-1]
        else:
            slabs = [jnp.maximum(slabs[i], rolled[k - 1 - i]) for i in range(k)]
        slabs = _compare_exchange(slabs, merge)
    return slabs


def _hyperbola(k):
    return [(i, j) for i in range(k) for j in range(k) if (i + 1) * (j + 1) <= k]


def _peer_scores(h, keys_ref, qry_ref, s2_ref, e1_ref, e2_ref, thr_ref, *, tt, lane_w):
    k = PEER_TOPK
    nkeys = keys_ref.shape[2]
    pairs = _hyperbola(k)
    n_slab = -(-len(pairs) // SUBLANES)
    n_slab = k // 2 if n_slab <= k // 2 else k
    sub = lax.broadcasted_iota(jnp.int32, (SUBLANES, lane_w), 0)
    inf = float("inf")
    for ts in range(tt // lane_w):
        cols = slice(ts * lane_w, (ts + 1) * lane_w)
        q1 = qry_ref[cols, pl.ds(pl.multiple_of(2 * h * nkeys, nkeys), nkeys)]
        q2 = qry_ref[cols, pl.ds(pl.multiple_of((2 * h + 1) * nkeys, nkeys), nkeys)]
        s1 = lax.dot_general(keys_ref[h, 0], q1, _NT, preferred_element_type=F32)
        s2 = lax.dot_general(keys_ref[h, 1], q2, _NT, preferred_element_type=F32)
        a = _top_sorted([s1[SUBLANES * v:SUBLANES * (v + 1)] for v in range(nkeys // SUBLANES)], k)
        b = _top_sorted([s2[SUBLANES * v:SUBLANES * (v + 1)] for v in range(nkeys // SUBLANES)], k)
        sums = {(i, j): a[i] + b[j] for (i, j) in pairs}
        packed = []
        for v in range(n_slab):
            slab = jnp.full((SUBLANES, lane_w), -inf, F32)
            for s in range(SUBLANES):
                idx = v * SUBLANES + s
                if idx < len(pairs):
                    slab = jnp.where(sub == s, sums[pairs[idx]], slab)
            packed.append(slab)
        tau = _top_sorted(packed, k)[k - 1]
        top = a[0] + b[0]
        z = jnp.zeros((SUBLANES, lane_w), F32)
        for slab in packed:
            z = z + jnp.where(slab >= tau, jnp.exp(slab - top), 0.0)
        zsum = jnp.sum(z, axis=0, keepdims=True)
        thr = jnp.full((nkeys, lane_w), inf, F32)
        for i in range(k - 1, -1, -1):
            t = jnp.full((SUBLANES, lane_w), inf, F32)
            for j in range(k):
                if (i, j) in sums:
                    t = jnp.minimum(t, jnp.where(sums[(i, j)] >= tau, b[j], inf))
            thr = jnp.where(s1 >= a[i][0:1, :], t[0:1, :], thr)
        thr_ref[h, ts] = thr
        s2_ref[h, ts] = s2
        e1_ref[h, ts] = jnp.exp(s1 - a[0][0:1, :])
        e2_ref[h, ts] = jnp.exp(s2 - b[0][0:1, :]) / zsum


def _peer_weights(key_blk, key_off, col0, ncols, act_ref, p_ref, s2_ref, e1_ref, e2_ref, thr_ref, *, lane_w):
    n_heads, _, nkeys, _ = s2_ref.shape
    ec = act_ref.shape[0]
    key_rows = pl.ds(pl.multiple_of(key_blk * SUBLANES, SUBLANES), SUBLANES)

    def unit(ts, ils):
        cols = slice(ts * lane_w, (ts + 1) * lane_w)
        w = [jnp.zeros((nkeys, lane_w), F32) for _ in ils]
        for h in range(n_heads):
            s2 = s2_ref[h, ts]
            e2 = e2_ref[h, ts]
            thr8 = thr_ref[h, ts, key_rows, :]
            e18 = e1_ref[h, ts, key_rows, :]
            for n, il in enumerate(ils):
                r = key_off + il
                w[n] = w[n] + jnp.where(s2 >= thr8[r:r + 1, :], e2, 0.0) * e18[r:r + 1, :]
        for n, il in enumerate(ils):
            rows = slice(il * nkeys, (il + 1) * nkeys)
            a = act_ref[rows, cols]
            gelu = 0.5 * a * (1.0 + lax.erf(a * (2.0 ** -0.5)))
            p_ref[rows, cols] = (w[n] * gelu).astype(BF16)

    n_il = ec // nkeys
    groups = [tuple(range(g, g + PEER_KEYS_PER_UNIT)) for g in range(0, n_il, PEER_KEYS_PER_UNIT)]
    return [functools.partial(unit, ts, ils) for ts in range(col0 // lane_w, (col0 + ncols) // lane_w)
            for ils in groups]


def _peer_body(h_ref, g_ref, wq_ref, keys_ref, u_ref, vt_ref, o_ref,
               xn_ref, qry_ref, s2_ref, e1_ref, e2_ref, thr_ref,
               act0_ref, act1_ref, p0_ref, p1_ref, acc_ref, *, lane_w, n_chunks):
    s = pl.program_id(1)
    tt = h_ref.shape[0]
    n_heads = keys_ref.shape[0]

    @pl.when(s == 0)
    def _():
        xn = _rms(h_ref[...], g_ref[...]).astype(BF16)
        xn_ref[...] = xn
        qry_ref[...] = jnp.dot(xn, wq_ref[...], preferred_element_type=F32).astype(BF16)

        def head(h, carry):
            _peer_scores(h, keys_ref, qry_ref, s2_ref, e1_ref, e2_ref, thr_ref, tt=tt, lane_w=lane_w)
            return carry

        lax.fori_loop(0, n_heads, head, 0)
        acc_ref[...] = jnp.zeros(acc_ref.shape, F32)
        act1_ref[...] = jnp.zeros(act1_ref.shape, F32)
        p0_ref[...] = jnp.zeros(p0_ref.shape, BF16)

    keys_per_chunk = u_ref.shape[0] // keys_ref.shape[2]
    assert 2 * keys_per_chunk == SUBLANES
    key_blk = jnp.clip((s - 1) // 2, 0, n_chunks // 2 - 1)

    def step(act_w, act_r, p_w, p_r, key_off):
        group = 2 * lane_w
        d = u_ref.shape[1]
        ec = u_ref.shape[0]
        kq = d // 4
        for c0 in range(0, tt, group):
            cols = slice(c0, c0 + group)
            units = _peer_weights(key_blk, key_off, c0, group, act_r, p_w, s2_ref, e1_ref, e2_ref, thr_ref,
                                  lane_w=lane_w)
            def act_part(q):
                return lax.dot_general(u_ref[:, q * kq:(q + 1) * kq], xn_ref[cols, q * kq:(q + 1) * kq],
                                       _NT, preferred_element_type=F32)

            def out_part(q):
                ks = slice(q * (ec // 2), (q + 1) * (ec // 2))
                return jnp.dot(vt_ref[0, :, ks], p_r[ks, cols], preferred_element_type=F32)

            granules = [(act_part, 0), (out_part, 0), (act_part, 1), (act_part, 2), (out_part, 1), (act_part, 3)]
            sums = {act_part: None, out_part: None}
            done = 0
            for n, unit in enumerate(units + [None]):
                while done < len(granules) and done * len(units) <= n * len(granules):
                    fn, q = granules[done]
                    part = fn(q)
                    sums[fn] = part if sums[fn] is None else sums[fn] + part
                    done += 1
                if unit is not None:
                    unit()
            act_w[:, cols] = sums[act_part]
            acc_ref[:, cols] += sums[out_part]

    @pl.when(s % 2 == 0)
    def _():
        step(act0_ref, act1_ref, p1_ref, p0_ref, keys_per_chunk)

    @pl.when(s % 2 == 1)
    def _():
        step(act1_ref, act0_ref, p0_ref, p1_ref, 0)

    @pl.when(s == n_chunks + 1)
    def _():
        o_ref[...] = h_ref[...] + acc_ref[...].T


def _peer(h2d, g, w_query, keys, u, v, tt, ec):
    t, d = h2d.shape
    v_t = v.reshape(u.shape[0] // ec, ec, d).transpose(0, 2, 1)
    n_heads, _, nkeys, dk = keys.shape
    n_chunks = u.shape[0] // ec
    qw = w_query.shape[1]
    stat = lambda dt: pltpu.VMEM((n_heads, tt // LANES, nkeys, LANES), dt)
    once = pl.Buffered(1)
    return pl.pallas_call(
        functools.partial(_peer_body, lane_w=LANES, n_chunks=n_chunks),
        out_shape=jax.ShapeDtypeStruct((t, d), F32),
        grid=(t // tt, n_chunks + 2),
        in_specs=[
            pl.BlockSpec((tt, d), lambda i, s: (i, 0), pipeline_mode=once),
            pl.BlockSpec((1, d), lambda i, s: (0, 0)),
            pl.BlockSpec((d, qw), lambda i, s: (0, 0), pipeline_mode=once),
            pl.BlockSpec(keys.shape, lambda i, s: (0, 0, 0, 0), pipeline_mode=once),
            pl.BlockSpec((ec, d), lambda i, s: (jnp.minimum(s, n_chunks - 1), 0)),
            pl.BlockSpec((1, d, ec), lambda i, s: (jnp.clip(s - 2, 0, n_chunks - 1), 0, 0)),
        ],
        out_specs=pl.BlockSpec((tt, d), lambda i, s: (i, 0)),
        scratch_shapes=[
            pltpu.VMEM((tt, d), BF16),
            pltpu.VMEM((tt, qw), BF16),
            stat(F32), stat(F32), stat(F32), stat(F32),
            pltpu.VMEM((ec, tt), F32), pltpu.VMEM((ec, tt), F32),
            pltpu.VMEM((ec, tt), BF16), pltpu.VMEM((ec, tt), BF16),
            pltpu.VMEM((d, tt), F32),
        ],
        compiler_params=pltpu.CompilerParams(
            dimension_semantics=("parallel", "arbitrary"), vmem_limit_bytes=VMEM_LIMIT),
        name="peer",
    )(h2d, g.reshape(1, d), w_query, keys, u, v_t)


def _layer(h2d, mem2d, batch, g_mix, w_in, q_gain_a, k_gain_a, bias_t, conv_w, g_mem, w_mem_kv,
           q_gain_x, k_gain_x, w_br_attn, w_br_conv, w_br_x, w_out, g_ffn,
           peer_w_query, peer_sub_keys, peer_u, peer_v):
    t, d = h2d.shape
    s_len = t // batch
    n_heads_a = bias_t.shape[0]
    dh_a = q_gain_a.shape[0]
    attn_w = n_heads_a * dh_a
    conv_wid = conv_w.shape[1]
    dh_x = q_gain_x.shape[0]
    xattn_w = w_mem_kv.shape[1] // 2
    assert attn_w == conv_wid == xattn_w, "column blocks of the projection must have equal width"
    branch_cols = 3 * attn_w + 3 * conv_wid + xattn_w

    w_in16 = w_in.astype(BF16)
    proj = _norm_matmul(h2d, g_mix, w_in16[:, :branch_cols], 512, branch_cols // 2, "proj_branches")
    gates = _norm_matmul(h2d, g_mix, w_in16[:, branch_cols:], 512, d, "proj_gates")
    proj3 = proj.reshape(batch, s_len, branch_cols)

    y_a = _moba(proj3, q_gain_a, k_gain_a, bias_t, n_heads_a, dh_a).reshape(t, attn_w)

    kv = _norm_matmul(mem2d, g_mem, w_mem_kv.astype(BF16), 512, w_mem_kv.shape[1], "mem_kv")
    kv3 = kv.reshape(batch, mem2d.shape[0] // batch, kv.shape[1])
    y_x = _xattn(proj3, 6, kv3, q_gain_x, k_gain_x, dh_x).reshape(t, xattn_w)

    h2d = _merge(h2d, y_a, proj, (3, 4, 5), y_x, gates, conv_w,
                 w_br_attn.astype(BF16), w_br_conv.astype(BF16), w_br_x.astype(BF16),
                 w_out.astype(BF16), s_len, 512)

    return _peer(h2d, g_ffn, peer_w_query.astype(BF16), peer_sub_keys.astype(BF16),
                 peer_u.astype(BF16), peer_v.astype(BF16), 1024, 512)


def kernel(x, mem, g_mix, w_in, q_gain_a, k_gain_a, rel_bias, conv_w, g_mem, w_mem_kv, q_gain_x, k_gain_x, w_br_attn, w_br_conv, w_br_x, w_out, g_ffn, peer_w_query, peer_sub_keys, peer_u, peer_v):
    batch, s_len, d = x.shape
    depth = g_mix.shape[0]
    bias_t = _bias_tiles(rel_bias, s_len // MOBA_BLOCK)
    h2d = x.reshape(batch * s_len, d)
    mem2d = mem.reshape(batch * mem.shape[1], d)
    for l in range(depth):
        h2d = _layer(h2d, mem2d, batch, g_mix[l], w_in[l], q_gain_a[l], k_gain_a[l], bias_t, conv_w[l],
                     g_mem[l], w_mem_kv[l], q_gain_x[l], k_gain_x[l], w_br_attn[l], w_br_conv[l],
                     w_br_x[l], w_out[l], g_ffn[l], peer_w_query[l], peer_sub_keys[l],
                     peer_u[l], peer_v[l])
    return h2d.reshape(batch, s_len, d)
```

```python
import functools
import math

import numpy as np
import jax
import jax.numpy as jnp
from jax import lax
from jax.experimental import pallas as pl
from jax.experimental.pallas import tpu as pltpu

F32 = jnp.float32
BF16 = jnp.bfloat16

RMS_EPS = 1e-6
NEG = -1e30

MOBA_BLOCK = 256
MOBA_TOPK = 3
REL_MAX_DIST = 1024
PEER_TOPK = 16
PEER_KEYS_PER_UNIT = 1
PEER_MXU_LEAD_NUM, PEER_MXU_LEAD_DEN = 1, 1
PEER_ACT_SLICES, PEER_OUT_SLICES = 4, 2
PEER_MXU_LAG = 2

SUBLANES = 8
LANES = 128
VMEM_LIMIT = 56 * 1024 * 1024

_NT = (((1,), (1,)), ((), ()))


def _rms(x, g):
    r = lax.rsqrt(jnp.mean(x * x, axis=-1, keepdims=True) + RMS_EPS)
    return x * r * g


def _norm_matmul_body(x_ref, g_ref, w_ref, o_ref, xn_ref):
    @pl.when(pl.program_id(1) == 0)
    def _():
        xn_ref[...] = _rms(x_ref[...], g_ref[...]).astype(BF16)

    o_ref[...] = jnp.dot(xn_ref[...], w_ref[...], preferred_element_type=F32)


def _norm_matmul(x2d, g, w, tm, tn, name):
    t, d = x2d.shape
    n = w.shape[1]
    return pl.pallas_call(
        _norm_matmul_body,
        out_shape=jax.ShapeDtypeStruct((t, n), F32),
        grid=(t // tm, n // tn),
        in_specs=[
            pl.BlockSpec((tm, d), lambda i, j: (i, 0)),
            pl.BlockSpec((1, d), lambda i, j: (0, 0)),
            pl.BlockSpec((d, tn), lambda i, j: (0, j)),
        ],
        out_specs=pl.BlockSpec((tm, tn), lambda i, j: (i, j)),
        scratch_shapes=[pltpu.VMEM((tm, d), BF16)],
        compiler_params=pltpu.CompilerParams(
            dimension_semantics=("parallel", "arbitrary"), vmem_limit_bytes=VMEM_LIMIT),
        name=name,
    )(x2d, g.reshape(1, d), w)


def _rel_bucket(dist, n_buckets):
    max_exact = n_buckets // 2
    d = jnp.maximum(dist, 0)
    df = jnp.maximum(d, 1).astype(F32)
    large = max_exact + (jnp.log(df / max_exact) / math.log(REL_MAX_DIST / max_exact)
                         * (n_buckets - max_exact)).astype(jnp.int32)
    large = jnp.minimum(large, n_buckets - 1)
    return jnp.where(d < max_exact, d, large)


def _bias_tiles_body(bucket_ref, rel_ref, o_ref, *, n_buckets, n_delta):
    h = pl.program_id(0)
    blk = MOBA_BLOCK
    for delta in range(n_delta):
        bk = bucket_ref[delta:delta + 1, :]
        w = jnp.zeros(bk.shape, F32)
        for b in range(n_buckets):
            w = jnp.where(bk == b, rel_ref[b, h], w)
        x = jnp.broadcast_to(w, (blk, 2 * blk))
        r = pltpu.roll(x, blk, 1, stride=1, stride_axis=0)
        tile = r[:, :blk]
        if delta == 0:
            krow = lax.broadcasted_iota(jnp.int32, (blk, blk), 0)
            qcol = lax.broadcasted_iota(jnp.int32, (blk, blk), 1)
            tile = jnp.where(qcol >= krow, tile, NEG)
        o_ref[0, delta] = tile


def _bias_tiles(rel_bias, n_delta):
    n_buckets, n_heads = rel_bias.shape
    blk = MOBA_BLOCK
    dist = (jnp.arange(n_delta, dtype=jnp.int32)[:, None] * blk - blk
            + jnp.arange(2 * blk, dtype=jnp.int32)[None, :])
    bucket = _rel_bucket(dist, n_buckets).astype(jnp.int32)
    return pl.pallas_call(
        functools.partial(_bias_tiles_body, n_buckets=n_buckets, n_delta=n_delta),
        out_shape=jax.ShapeDtypeStruct((n_heads, n_delta, blk, blk), F32),
        grid=(n_heads,),
        in_specs=[
            pl.BlockSpec((n_delta, 2 * blk), lambda h: (0, 0)),
            pl.BlockSpec(memory_space=pltpu.SMEM),
        ],
        out_specs=pl.BlockSpec((1, n_delta, blk, blk), lambda h: (h, 0, 0, 0)),
        compiler_params=pltpu.CompilerParams(dimension_semantics=("arbitrary",)),
        name="rel_bias_tiles",
    )(bucket, rel_bias)


def _moba_body(q_ref, k_ref, v_ref, qg_ref, kg_ref, bias_ref, o_ref, ot_ref, *, dh):
    s_len = q_ref.shape[1]
    blk = MOBA_BLOCK
    nb = s_len // blk
    scale = dh ** -0.5
    heads_per_step = q_ref.shape[2] // dh
    vt_all = v_ref[0].T
    for hh in range(heads_per_step):
        sl = slice(hh * dh, (hh + 1) * dh)
        qn = _rms(q_ref[0, :, sl], qg_ref[...])
        kn = _rms(k_ref[0, :, sl], kg_ref[...])
        kmean = jnp.mean(kn.reshape(nb, blk, dh), axis=1)
        qb16 = qn.astype(BF16)
        kb16 = kn.astype(BF16)
        vt = vt_all[sl, :].astype(BF16)
        gate_t = lax.dot_general(kmean.astype(BF16), qb16, _NT, preferred_element_type=F32)
        row = lax.broadcasted_iota(jnp.int32, (nb, blk), 0)
        for qb in range(nb):
            qs = slice(qb * blk, (qb + 1) * blk)
            qt = qb16[qs]
            g = gate_t[:, qs]
            rank = jnp.zeros((nb, blk), jnp.int32)
            for j in range(qb):
                gj = g[j:j + 1, :]
                beats = jnp.where(gj > g, 1, jnp.where((gj == g) & (j < row), 1, 0))
                rank = rank + beats
            negm = jnp.where((rank < MOBA_TOPK) & (row < qb), 0.0, NEG)
            parts = []
            for kb in range(qb + 1):
                st = lax.dot_general(kb16[kb * blk:(kb + 1) * blk], qt, _NT,
                                     preferred_element_type=F32)
                st = st * scale + bias_ref[hh, qb - kb]
                if kb < qb:
                    st = st + negm[kb:kb + 1, :]
                parts.append(st)
            logit = jnp.concatenate(parts, axis=0) if len(parts) > 1 else parts[0]
            m = jnp.max(logit, axis=0, keepdims=True)
            p = jnp.exp(logit - m)
            denom = jnp.sum(p, axis=0, keepdims=True)
            nk = (qb + 1) * blk
            o_t = jnp.dot(vt[:, :nk], p.astype(BF16), preferred_element_type=F32)
            ot_ref[sl, qs] = o_t / denom
    o_ref[0] = ot_ref[...].T.astype(o_ref.dtype)


def _moba(proj3, q_gain, k_gain, bias_t, n_heads, dh):
    b, s_len, _ = proj3.shape
    hps = LANES // dh
    width = n_heads * dh
    nblk = width // LANES
    n_delta = bias_t.shape[1]
    blk = MOBA_BLOCK
    return pl.pallas_call(
        functools.partial(_moba_body, dh=dh),
        out_shape=jax.ShapeDtypeStruct((b, s_len, width), BF16),
        grid=(n_heads // hps, b),
        in_specs=[
            pl.BlockSpec((1, s_len, LANES), lambda hp, bi: (bi, 0, hp)),
            pl.BlockSpec((1, s_len, LANES), lambda hp, bi: (bi, 0, nblk + hp)),
            pl.BlockSpec((1, s_len, LANES), lambda hp, bi: (bi, 0, 2 * nblk + hp)),
            pl.BlockSpec((1, dh), lambda hp, bi: (0, 0)),
            pl.BlockSpec((1, dh), lambda hp, bi: (0, 0)),
            pl.BlockSpec((hps, n_delta, blk, blk), lambda hp, bi: (hp, 0, 0, 0)),
        ],
        out_specs=pl.BlockSpec((1, s_len, LANES), lambda hp, bi: (bi, 0, hp)),
        scratch_shapes=[pltpu.VMEM((LANES, s_len), F32)],
        compiler_params=pltpu.CompilerParams(
            dimension_semantics=("parallel", "arbitrary"), vmem_limit_bytes=VMEM_LIMIT),
        name="moba_attention",
    )(proj3, proj3, proj3, q_gain.reshape(1, dh), k_gain.reshape(1, dh), bias_t)


def _xattn_body(q_ref, kv_ref, qg_ref, kg_ref, o_ref, *, dh, chunk):
    s_len = q_ref.shape[1]
    width = q_ref.shape[2]
    n_heads = width // dh
    scale = dh ** -0.5
    for h in range(n_heads):
        sl = slice(h * dh, (h + 1) * dh)
        kn = _rms(kv_ref[0, :, sl], kg_ref[...]).astype(BF16)
        v = kv_ref[0, :, width + h * dh:width + (h + 1) * dh].astype(BF16)
        for c in range(s_len // chunk):
            rs = slice(c * chunk, (c + 1) * chunk)
            qn = _rms(q_ref[0, rs, sl], qg_ref[...]).astype(BF16)
            s = lax.dot_general(qn, kn, _NT, preferred_element_type=F32) * scale
            m = jnp.max(s, axis=-1, keepdims=True)
            p = jnp.exp(s - m)
            denom = jnp.sum(p, axis=-1, keepdims=True)
            o = jnp.dot(p.astype(BF16), v, preferred_element_type=F32) / denom
            o_ref[0, rs, sl] = o.astype(o_ref.dtype)


def _xattn(proj3, q_col_block, kv3, q_gain, k_gain, dh):
    b, s_len, _ = proj3.shape
    m_len, kv_width = kv3.shape[1], kv3.shape[2]
    width = kv_width // 2
    return pl.pallas_call(
        functools.partial(_xattn_body, dh=dh, chunk=512),
        out_shape=jax.ShapeDtypeStruct((b, s_len, width), BF16),
        grid=(b,),
        in_specs=[
            pl.BlockSpec((1, s_len, width), lambda bi: (bi, 0, q_col_block)),
            pl.BlockSpec((1, m_len, kv_width), lambda bi: (bi, 0, 0)),
            pl.BlockSpec((1, dh), lambda bi: (0, 0)),
            pl.BlockSpec((1, dh), lambda bi: (0, 0)),
        ],
        out_specs=pl.BlockSpec((1, s_len, width), lambda bi: (bi, 0, 0)),
        compiler_params=pltpu.CompilerParams(
            dimension_semantics=("parallel",), vmem_limit_bytes=VMEM_LIMIT),
        name="cross_attention",
    )(proj3, kv3, q_gain.reshape(1, dh), k_gain.reshape(1, dh))


def _merge_body(x_ref, ya_ref, u_ref, b_ref, c_ref, uh_ref, ch_ref, yx_ref, gt_ref, cw_ref,
                wa_ref, wc_ref, wx_ref, wo_ref, o_ref, *, tiles_per_seq):
    i = pl.program_id(0)
    tm, d = x_ref.shape
    z = c_ref[...] * u_ref[...]
    zh = ch_ref[...] * uh_ref[...]
    zh = jnp.where(i % tiles_per_seq == 0, 0.0, zh)
    row = lax.broadcasted_iota(jnp.int32, z.shape, 0)
    z1 = pltpu.roll(z, 1, 0)
    z1 = jnp.where(row == 0, zh[7:8, :], z1)
    z2 = pltpu.roll(z, 2, 0)
    z2 = jnp.where(row == 0, zh[6:7, :], jnp.where(row == 1, zh[7:8, :], z2))
    conv = cw_ref[0:1, :] * z2 + cw_ref[1:2, :] * z1 + cw_ref[2:3, :] * z
    y_c = jnp.dot((b_ref[...] * conv).astype(BF16), wc_ref[...], preferred_element_type=F32)
    y_a = jnp.dot(ya_ref[...], wa_ref[...], preferred_element_type=F32)
    y_x = jnp.dot(yx_ref[...], wx_ref[...], preferred_element_type=F32)
    merged = (jax.nn.sigmoid(gt_ref[:, 0:d]) * y_a
              + jax.nn.sigmoid(gt_ref[:, d:2 * d]) * y_c
              + jax.nn.sigmoid(gt_ref[:, 2 * d:3 * d]) * y_x)
    o_ref[...] = x_ref[...] + jnp.dot(merged.astype(BF16), wo_ref[...], preferred_element_type=F32)


def _merge(x2d, y_a, proj, conv_cols, y_x, gates, conv_w, w_a, w_c, w_x, w_o, s_len, tm):
    t, d = x2d.shape
    cwid = conv_w.shape[1]
    cu, cb, cc = conv_cols
    halo = SUBLANES
    rows_per_tile = tm // halo

    def halo_map(col):
        return lambda i: (jnp.maximum(i * rows_per_tile - 1, 0), col)

    full = lambda shape: pl.BlockSpec(shape, lambda i: (0, 0))
    return pl.pallas_call(
        functools.partial(_merge_body, tiles_per_seq=s_len // tm),
        out_shape=jax.ShapeDtypeStruct((t, d), F32),
        grid=(t // tm,),
        in_specs=[
            pl.BlockSpec((tm, d), lambda i: (i, 0)),
            pl.BlockSpec((tm, y_a.shape[1]), lambda i: (i, 0)),
            pl.BlockSpec((tm, cwid), lambda i: (i, cu)),
            pl.BlockSpec((tm, cwid), lambda i: (i, cb)),
            pl.BlockSpec((tm, cwid), lambda i: (i, cc)),
            pl.BlockSpec((halo, cwid), halo_map(cu)),
            pl.BlockSpec((halo, cwid), halo_map(cc)),
            pl.BlockSpec((tm, y_x.shape[1]), lambda i: (i, 0)),
            pl.BlockSpec((tm, gates.shape[1]), lambda i: (i, 0)),
            full(conv_w.shape), full(w_a.shape), full(w_c.shape), full(w_x.shape), full(w_o.shape),
        ],
        out_specs=pl.BlockSpec((tm, d), lambda i: (i, 0)),
        compiler_params=pltpu.CompilerParams(
            dimension_semantics=("parallel",), vmem_limit_bytes=VMEM_LIMIT),
        name="branch_merge",
    )(x2d, y_a, proj, proj, proj, proj, proj, y_x, gates, conv_w, w_a, w_c, w_x, w_o)


def _oddeven_merge_sort_pairs(n):
    pairs = []
    p = 1
    while p < n:
        k = p
        while k >= 1:
            for j in range(k % p, n - k, 2 * k):
                for i in range(min(k, n - j - k)):
                    if (i + j) // (2 * p) == (i + j + k) // (2 * p):
                        pairs.append((i + j, i + j + k))
            k //= 2
        p *= 2
    return pairs


def _bitonic_merge_pairs(n):
    pairs = []
    d = n // 2
    while d >= 1:
        pairs += [(i, i + d) for i in range(n) if i & d == 0]
        d //= 2
    return pairs


def _compare_exchange(slabs, pairs):
    for a, b in pairs:
        hi = jnp.maximum(slabs[a], slabs[b])
        lo = jnp.minimum(slabs[a], slabs[b])
        slabs[a], slabs[b] = hi, lo
    return slabs


def _top_sorted(slabs, k):
    n = len(slabs)
    slabs = _compare_exchange(list(slabs), _oddeven_merge_sort_pairs(n))
    merge = _bitonic_merge_pairs(k)
    for shift in (4, 2, 1):
        rolled = [pltpu.roll(s, shift, 0) for s in slabs]
        if len(slabs) < k:
            slabs = slabs + rolled[::-1]
        else:
            slabs = [jnp.maximum(slabs[i], rolled[k - 1 - i]) for i in range(k)]
        slabs = _compare_exchange(slabs, merge)
    return slabs


def _hyperbola(k):
    return [(i, j) for i in range(k) for j in range(k) if (i + 1) * (j + 1) <= k]


def _peer_scores(h, keys_ref, qry_ref, s2_ref, e1_ref, e2_ref, thr_ref, *, tt, lane_w):
    k = PEER_TOPK
    nkeys = keys_ref.shape[2]
    pairs = _hyperbola(k)
    n_slab = -(-len(pairs) // SUBLANES)
    n_slab = k // 2 if n_slab <= k // 2 else k
    sub = lax.broadcasted_iota(jnp.int32, (SUBLANES, lane_w), 0)
    inf = float("inf")
    for ts in range(tt // lane_w):
        cols = slice(ts * lane_w, (ts + 1) * lane_w)
        q1 = qry_ref[cols, pl.ds(pl.multiple_of(2 * h * nkeys, nkeys), nkeys)]
        q2 = qry_ref[cols, pl.ds(pl.multiple_of((2 * h + 1) * nkeys, nkeys), nkeys)]
        s1 = lax.dot_general(keys_ref[h, 0], q1, _NT, preferred_element_type=F32)
        s2 = lax.dot_general(keys_ref[h, 1], q2, _NT, preferred_element_type=F32)
        a = _top_sorted([s1[SUBLANES * v:SUBLANES * (v + 1)] for v in range(nkeys // SUBLANES)], k)
        b = _top_sorted([s2[SUBLANES * v:SUBLANES * (v + 1)] for v in range(nkeys // SUBLANES)], k)
        sums = {(i, j): a[i] + b[j] for (i, j) in pairs}
        packed = []
        for v in range(n_slab):
            slab = jnp.full((SUBLANES, lane_w), -inf, F32)
            for s in range(SUBLANES):
                idx = v * SUBLANES + s
                if idx < len(pairs):
                    slab = jnp.where(sub == s, sums[pairs[idx]], slab)
            packed.append(slab)
        tau = _top_sorted(packed, k)[k - 1]
        top = a[0] + b[0]
        z = jnp.zeros((SUBLANES, lane_w), F32)
        for slab in packed:
            z = z + jnp.where(slab >= tau, jnp.exp(slab - top), 0.0)
        zsum = jnp.sum(z, axis=0, keepdims=True)
        thr = jnp.full((nkeys, lane_w), inf, F32)
        for i in range(k - 1, -1, -1):
            t = jnp.full((SUBLANES, lane_w), inf, F32)
            for j in range(k):
                if (i, j) in sums:
                    t = jnp.minimum(t, jnp.where(sums[(i, j)] >= tau, b[j], inf))
            thr = jnp.where(s1 >= a[i][0:1, :], t[0:1, :], thr)
        thr_ref[h, ts] = thr
        s2_ref[h, ts] = s2
        e1_ref[h, ts] = jnp.exp(s1 - a[0][0:1, :])
        e2_ref[h, ts] = jnp.exp(s2 - b[0][0:1, :]) / zsum


def _peer_weights(key_blk, key_off, col0, ncols, act_ref, p_ref, s2_ref, e1_ref, e2_ref, thr_ref, *, lane_w):
    n_heads, _, nkeys, _ = s2_ref.shape
    ec = act_ref.shape[0]
    key_rows = pl.ds(pl.multiple_of(key_blk * SUBLANES, SUBLANES), SUBLANES)

    def unit(ts, ils, after=None):
        cols = slice(ts * lane_w, (ts + 1) * lane_w)
        w = [jnp.zeros((nkeys, lane_w), F32) for _ in ils]
        for h in range(n_heads):
            s2 = s2_ref[h, ts]
            e2 = e2_ref[h, ts]
            thr8 = thr_ref[h, ts, key_rows, :]
            if h == 0 and after is not None:
                thr8 = thr8 + after
            e18 = e1_ref[h, ts, key_rows, :]
            for n, il in enumerate(ils):
                r = key_off + il
                w[n] = w[n] + jnp.where(s2 >= thr8[r:r + 1, :], e2, 0.0) * e18[r:r + 1, :]
        for n, il in enumerate(ils):
            rows = slice(il * nkeys, (il + 1) * nkeys)
            a = act_ref[rows, cols]
            gelu = 0.5 * a * (1.0 + lax.erf(a * (2.0 ** -0.5)))
            p_ref[rows, cols] = (w[n] * gelu).astype(BF16)

    n_il = ec // nkeys
    groups = [tuple(range(g, g + PEER_KEYS_PER_UNIT)) for g in range(0, n_il, PEER_KEYS_PER_UNIT)]
    return [functools.partial(unit, ts, ils) for ts in range(col0 // lane_w, (col0 + ncols) // lane_w)
            for ils in groups]


def _peer_body(h_ref, g_ref, wq_ref, keys_ref, u_hbm, vt_hbm, o_ref,
               xn_ref, qry_ref, s2_ref, e1_ref, e2_ref, thr_ref,
               act0_ref, act1_ref, p0_ref, p1_ref, acc_ref,
               ubuf0_ref, ubuf1_ref, vbuf0_ref, vbuf1_ref, sem_ref, *, lane_w, n_chunks):
    tile = pl.program_id(0)
    n_tiles = pl.num_programs(0)
    s = pl.program_id(1)
    tt = h_ref.shape[0]
    n_heads, _, nkeys, _ = keys_ref.shape
    ec, d = ubuf0_ref.shape
    acts, ps = (act0_ref, act1_ref), (p0_ref, p1_ref)
    ubufs, vbufs = (ubuf0_ref, ubuf1_ref), (vbuf0_ref, vbuf1_ref)
    keys_per_chunk = ec // nkeys
    assert 2 * keys_per_chunk == SUBLANES and n_chunks % 2 == 0

    def u_copy(chunk, slot):
        rows = pl.ds(pl.multiple_of(chunk * ec, ec), ec)
        return pltpu.make_async_copy(u_hbm.at[rows, :], ubufs[slot], sem_ref.at[slot])

    def v_copy(chunk, slot):
        return pltpu.make_async_copy(vt_hbm.at[chunk], vbufs[slot], sem_ref.at[2 + slot])

    def compute(par, do_act, do_gate, do_out):
        act_w, act_r, p_w, p_r = acts[par], acts[1 - par], ps[1 - par], ps[par]
        ub, vb = ubufs[par], vbufs[par]
        key_blk = (s - 1) // 2
        key_off = keys_per_chunk * (1 - par)
        group = 2 * lane_w
        kq = d // PEER_ACT_SLICES
        units, granules = [], []
        for c0 in range(0, tt, group):
            cols = slice(c0, c0 + group)
            if do_gate:
                units += _peer_weights(key_blk, key_off, c0, group, act_r, p_w, s2_ref, e1_ref, e2_ref,
                                       thr_ref, lane_w=lane_w)

            def act_part(q, cols=cols):
                return lax.dot_general(ub[:, q * kq:(q + 1) * kq], xn_ref[cols, q * kq:(q + 1) * kq],
                                       _NT, preferred_element_type=F32)

            def out_part(q, cols=cols):
                ks = slice(q * (ec // PEER_OUT_SLICES), (q + 1) * (ec // PEER_OUT_SLICES))
                return jnp.dot(vb[:, ks], p_r[ks, cols], preferred_element_type=F32)

            def act_store(val, cols=cols):
                act_w[:, cols] = val

            def out_store(val, cols=cols):
                acc_ref[:, cols] += val

            a_parts = ([(act_part, q, act_store if q == PEER_ACT_SLICES - 1 else None)
                        for q in range(PEER_ACT_SLICES)] if do_act else [])
            o_parts = ([(out_part, q, out_store if q == PEER_OUT_SLICES - 1 else None)
                        for q in range(PEER_OUT_SLICES)] if do_out else [])
            while a_parts or o_parts:
                granules += a_parts[:1] + o_parts[:1] + a_parts[1:2]
                a_parts, o_parts = a_parts[2:], o_parts[1:]
        sums, done, afters = {}, 0, []
        for n, unit in enumerate(units + [None]):
            while done < len(granules) and (unit is None or done * PEER_MXU_LEAD_DEN * len(units)
                                            <= n * PEER_MXU_LEAD_NUM * len(granules)):
                fn, q, store = granules[done]
                part = fn(q)
                key = fn.__name__
                sums[key] = part if q == 0 else sums[key] + part
                if store is not None:
                    store(sums[key])
                done += 1
                bits = pltpu.bitcast(part[0:SUBLANES, 0:lane_w], jnp.uint32)
                sixteen = jnp.uint32(16)
                zero = lax.shift_right_logical(lax.shift_right_logical(bits, sixteen), sixteen)
                afters.append(pltpu.bitcast(zero, F32))
            if unit is not None:
                unit(after=afters[done - 1 - PEER_MXU_LAG] if done > PEER_MXU_LAG else None)

    @pl.when(s == 0)
    def _():
        @pl.when(tile == 0)
        def _():
            u_copy(0, 0).start()

        u_copy(0, 0).wait()
        u_copy(1, 1).start()
        xn = _rms(h_ref[...], g_ref[...]).astype(BF16)
        xn_ref[...] = xn
        qry_ref[...] = jnp.dot(xn, wq_ref[...], preferred_element_type=F32).astype(BF16)

        def head(h, carry):
            _peer_scores(h, keys_ref, qry_ref, s2_ref, e1_ref, e2_ref, thr_ref, tt=tt, lane_w=lane_w)
            return carry

        lax.fori_loop(0, n_heads, head, 0)
        acc_ref[...] = jnp.zeros(acc_ref.shape, F32)
        compute(0, True, False, False)

    @pl.when(s == 1)
    def _():
        u_copy(0, 1).wait()
        u_copy(2, 0).start()
        v_copy(0, 0).start()
        compute(1, True, True, False)

    @pl.when((s >= 2) & (s < n_chunks) & (s % 2 == 0))
    def _():
        u_copy(0, 0).wait()
        v_copy(0, 0).wait()
        u_copy(s + 1, 1).start()
        v_copy(s - 1, 1).start()
        compute(0, True, True, True)

    @pl.when((s >= 2) & (s < n_chunks) & (s % 2 == 1))
    def _():
        u_copy(0, 1).wait()
        v_copy(0, 1).wait()

        @pl.when(s + 1 < n_chunks)
        def _():
            u_copy(s + 1, 0).start()

        v_copy(s - 1, 0).start()
        compute(1, True, True, True)

    @pl.when(s == n_chunks)
    def _():
        v_copy(0, 0).wait()
        v_copy(n_chunks - 1, 1).start()
        compute(0, False, True, True)

    @pl.when(s == n_chunks + 1)
    def _():
        v_copy(0, 1).wait()

        @pl.when(tile + 1 < n_tiles)
        def _():
            u_copy(0, 0).start()

        compute(1, False, False, True)
        o_ref[...] = h_ref[...] + acc_ref[...].T


def _peer(h2d, g, w_query, keys, u, v, tt, ec):
    t, d = h2d.shape
    v_t = v.reshape(u.shape[0] // ec, ec, d).transpose(0, 2, 1)
    n_heads, _, nkeys, dk = keys.shape
    n_chunks = u.shape[0] // ec
    qw = w_query.shape[1]
    stat = lambda: pltpu.VMEM((n_heads, tt // LANES, nkeys, LANES), F32)
    return pl.pallas_call(
        functools.partial(_peer_body, lane_w=LANES, n_chunks=n_chunks),
        out_shape=jax.ShapeDtypeStruct((t, d), F32),
        grid=(t // tt, n_chunks + 2),
        in_specs=[
            pl.BlockSpec((tt, d), lambda i, s: (i, 0)),
            pl.BlockSpec((1, d), lambda i, s: (0, 0)),
            pl.BlockSpec((d, qw), lambda i, s: (0, 0)),
            pl.BlockSpec(keys.shape, lambda i, s: (0, 0, 0, 0)),
            pl.BlockSpec(memory_space=pl.ANY),
            pl.BlockSpec(memory_space=pl.ANY),
        ],
        out_specs=pl.BlockSpec((tt, d), lambda i, s: (i, 0)),
        scratch_shapes=[
            pltpu.VMEM((tt, d), BF16),
            pltpu.VMEM((tt, qw), BF16),
            stat(), stat(), stat(), stat(),
            pltpu.VMEM((ec, tt), F32), pltpu.VMEM((ec, tt), F32),
            pltpu.VMEM((ec, tt), BF16), pltpu.VMEM((ec, tt), BF16),
            pltpu.VMEM((d, tt), F32),
            pltpu.VMEM((ec, d), BF16), pltpu.VMEM((ec, d), BF16),
            pltpu.VMEM((d, ec), BF16), pltpu.VMEM((d, ec), BF16),
            pltpu.SemaphoreType.DMA((4,)),
        ],
        compiler_params=pltpu.CompilerParams(
            dimension_semantics=("arbitrary", "arbitrary"), vmem_limit_bytes=VMEM_LIMIT),
        name="peer",
    )(h2d, g.reshape(1, d), w_query, keys, u, v_t)


def _layer(h2d, mem2d, batch, g_mix, w_in, q_gain_a, k_gain_a, bias_t, conv_w, g_mem, w_mem_kv,
           q_gain_x, k_gain_x, w_br_attn, w_br_conv, w_br_x, w_out, g_ffn,
           peer_w_query, peer_sub_keys, peer_u, peer_v):
    t, d = h2d.shape
    s_len = t // batch
    n_heads_a = bias_t.shape[0]
    dh_a = q_gain_a.shape[0]
    attn_w = n_heads_a * dh_a
    conv_wid = conv_w.shape[1]
    dh_x = q_gain_x.shape[0]
    xattn_w = w_mem_kv.shape[1] // 2
    assert attn_w == conv_wid == xattn_w, "column blocks of the projection must have equal width"
    branch_cols = 3 * attn_w + 3 * conv_wid + xattn_w

    w_in16 = w_in.astype(BF16)
    proj = _norm_matmul(h2d, g_mix, w_in16[:, :branch_cols], 512, branch_cols // 2, "proj_branches")
    gates = _norm_matmul(h2d, g_mix, w_in16[:, branch_cols:], 512, d, "proj_gates")
    proj3 = proj.reshape(batch, s_len, branch_cols)

    y_a = _moba(proj3, q_gain_a, k_gain_a, bias_t, n_heads_a, dh_a).reshape(t, attn_w)

    kv = _norm_matmul(mem2d, g_mem, w_mem_kv.astype(BF16), 512, w_mem_kv.shape[1], "mem_kv")
    kv3 = kv.reshape(batch, mem2d.shape[0] // batch, kv.shape[1])
    y_x = _xattn(proj3, 6, kv3, q_gain_x, k_gain_x, dh_x).reshape(t, xattn_w)

    h2d = _merge(h2d, y_a, proj, (3, 4, 5), y_x, gates, conv_w,
                 w_br_attn.astype(BF16), w_br_conv.astype(BF16), w_br_x.astype(BF16),
                 w_out.astype(BF16), s_len, 512)

    return _peer(h2d, g_ffn, peer_w_query.astype(BF16), peer_sub_keys.astype(BF16),
                 peer_u.astype(BF16), peer_v.astype(BF16), 512, 512)


def kernel(x, mem, g_mix, w_in, q_gain_a, k_gain_a, rel_bias, conv_w, g_mem, w_mem_kv, q_gain_x, k_gain_x, w_br_attn, w_br_conv, w_br_x, w_out, g_ffn, peer_w_query, peer_sub_keys, peer_u, peer_v):
    batch, s_len, d = x.shape
    depth = g_mix.shape[0]
    bias_t = _bias_tiles(rel_bias, s_len // MOBA_BLOCK)
    h2d = x.reshape(batch * s_len, d)
    mem2d = mem.reshape(batch * mem.shape[1], d)
    for l in range(depth):
        h2d = _layer(h2d, mem2d, batch, g_mix[l], w_in[l], q_gain_a[l], k_gain_a[l], bias_t, conv_w[l],
                     g_mem[l], w_mem_kv[l], q_gain_x[l], k_gain_x[l], w_br_attn[l], w_br_conv[l],
                     w_br_x[l], w_out[l], g_ffn[l], peer_w_query[l], peer_sub_keys[l],
                     peer_u[l], peer_v[l])
    return h2d.reshape(batch, s_len, d)
```

```python
import functools
import math

import numpy as np
import jax
import jax.numpy as jnp
from jax import lax
from jax.experimental import pallas as pl
from jax.experimental.pallas import tpu as pltpu

F32 = jnp.float32
BF16 = jnp.bfloat16

RMS_EPS = 1e-6
NEG = -1e30

MOBA_BLOCK = 256
MOBA_TOPK = 3
REL_MAX_DIST = 1024
PEER_TOPK = 16
PEER_ACT_SLICES, PEER_OUT_SLICES = 4, 2

SUBLANES = 8
BF16_SUBLANES = 16
LANES = 128
VMEM_LIMIT = 56 * 1024 * 1024

_NT = (((1,), (1,)), ((), ()))


def _rms(x, g):
    r = lax.rsqrt(jnp.mean(x * x, axis=-1, keepdims=True) + RMS_EPS)
    return x * r * g


def _norm_matmul_body(x_ref, g_ref, *refs):
    n_out = len(refs) // 2
    xn = _rms(x_ref[...], g_ref[...]).astype(BF16)
    for w_ref, o_ref in zip(refs[:n_out], refs[n_out:]):
        o_ref[...] = jnp.dot(xn, w_ref[...], preferred_element_type=F32).astype(o_ref.dtype)


def _norm_matmul(x2d, g, weights, out_dtype, tm, name):
    t, d = x2d.shape
    return pl.pallas_call(
        _norm_matmul_body,
        out_shape=[jax.ShapeDtypeStruct((t, w.shape[1]), out_dtype) for w in weights],
        grid=(t // tm,),
        in_specs=[pl.BlockSpec((tm, d), lambda i: (i, 0)), pl.BlockSpec((1, d), lambda i: (0, 0))]
        + [pl.BlockSpec(w.shape, lambda i: (0, 0)) for w in weights],
        out_specs=[pl.BlockSpec((tm, w.shape[1]), lambda i: (i, 0)) for w in weights],
        compiler_params=pltpu.CompilerParams(
            dimension_semantics=("parallel",), vmem_limit_bytes=VMEM_LIMIT),
        name=name,
    )(x2d, g.reshape(1, d), *weights)


def _rel_bucket(dist, n_buckets):
    max_exact = n_buckets // 2
    d = jnp.maximum(dist, 0)
    df = jnp.maximum(d, 1).astype(F32)
    large = max_exact + (jnp.log(df / max_exact) / math.log(REL_MAX_DIST / max_exact)
                         * (n_buckets - max_exact)).astype(jnp.int32)
    large = jnp.minimum(large, n_buckets - 1)
    return jnp.where(d < max_exact, d, large)


def _bias_tiles_body(bucket_ref, rel_ref, o_ref, *, n_buckets, n_delta):
    h = pl.program_id(0)
    blk = MOBA_BLOCK
    for delta in range(n_delta):
        bk = bucket_ref[delta:delta + 1, :]
        w = jnp.zeros(bk.shape, F32)
        for b in range(n_buckets):
            w = jnp.where(bk == b, rel_ref[b, h], w)
        x = jnp.broadcast_to(w, (blk, 2 * blk))
        r = pltpu.roll(x, blk, 1, stride=1, stride_axis=0)
        tile = r[:, :blk]
        if delta == 0:
            krow = lax.broadcasted_iota(jnp.int32, (blk, blk), 0)
            qcol = lax.broadcasted_iota(jnp.int32, (blk, blk), 1)
            tile = jnp.where(qcol >= krow, tile, NEG)
        o_ref[0, delta] = tile


def _bias_tiles(rel_bias, n_delta):
    n_buckets, n_heads = rel_bias.shape
    blk = MOBA_BLOCK
    dist = (jnp.arange(n_delta, dtype=jnp.int32)[:, None] * blk - blk
            + jnp.arange(2 * blk, dtype=jnp.int32)[None, :])
    bucket = _rel_bucket(dist, n_buckets).astype(jnp.int32)
    return pl.pallas_call(
        functools.partial(_bias_tiles_body, n_buckets=n_buckets, n_delta=n_delta),
        out_shape=jax.ShapeDtypeStruct((n_heads, n_delta, blk, blk), F32),
        grid=(n_heads,),
        in_specs=[
            pl.BlockSpec((n_delta, 2 * blk), lambda h: (0, 0)),
            pl.BlockSpec(memory_space=pltpu.SMEM),
        ],
        out_specs=pl.BlockSpec((1, n_delta, blk, blk), lambda h: (h, 0, 0, 0)),
        compiler_params=pltpu.CompilerParams(dimension_semantics=("arbitrary",)),
        name="rel_bias_tiles",
    )(bucket, rel_bias)


def _moba_body(q_ref, k_ref, v_ref, qg_ref, kg_ref, bias_ref, o_ref, ot_ref, *, dh):
    s_len = q_ref.shape[1]
    blk = MOBA_BLOCK
    nb = s_len // blk
    scale = dh ** -0.5
    heads_per_step = q_ref.shape[2] // dh
    q_all = q_ref[0].astype(F32)
    k_all = k_ref[0].astype(F32)
    vt_all = v_ref[0].astype(F32).T
    for hh in range(heads_per_step):
        sl = slice(hh * dh, (hh + 1) * dh)
        qn = _rms(q_all[:, sl], qg_ref[...])
        kn = _rms(k_all[:, sl], kg_ref[...])
        kmean = jnp.mean(kn.reshape(nb, blk, dh), axis=1)
        qb16 = qn.astype(BF16)
        kb16 = kn.astype(BF16)
        vt = vt_all[sl, :].astype(BF16)
        gate_t = lax.dot_general(kmean.astype(BF16), qb16, _NT, preferred_element_type=F32)
        row = lax.broadcasted_iota(jnp.int32, (nb, blk), 0)
        for qb in range(nb):
            qs = slice(qb * blk, (qb + 1) * blk)
            qt = qb16[qs]
            g = gate_t[:, qs]
            rank = jnp.zeros((nb, blk), jnp.int32)
            for j in range(qb):
                gj = g[j:j + 1, :]
                beats = jnp.where(gj > g, 1, jnp.where((gj == g) & (j < row), 1, 0))
                rank = rank + beats
            negm = jnp.where((rank < MOBA_TOPK) & (row < qb), 0.0, NEG)
            parts = []
            for kb in range(qb + 1):
                st = lax.dot_general(kb16[kb * blk:(kb + 1) * blk], qt, _NT,
                                     preferred_element_type=F32)
                st = st * scale + bias_ref[hh, qb - kb]
                if kb < qb:
                    st = st + negm[kb:kb + 1, :]
                parts.append(st)
            logit = jnp.concatenate(parts, axis=0) if len(parts) > 1 else parts[0]
            m = jnp.max(logit, axis=0, keepdims=True)
            p = jnp.exp(logit - m)
            denom = jnp.sum(p, axis=0, keepdims=True)
            nk = (qb + 1) * blk
            o_t = jnp.dot(vt[:, :nk], p.astype(BF16), preferred_element_type=F32)
            ot_ref[sl, qs] = o_t / denom
    o_ref[0] = ot_ref[...].T.astype(o_ref.dtype)


def _moba(proj3, q_gain, k_gain, bias_t, n_heads, dh):
    b, s_len, _ = proj3.shape
    hps = LANES // dh
    width = n_heads * dh
    nblk = width // LANES
    n_delta = bias_t.shape[1]
    blk = MOBA_BLOCK
    return pl.pallas_call(
        functools.partial(_moba_body, dh=dh),
        out_shape=jax.ShapeDtypeStruct((b, s_len, width), BF16),
        grid=(n_heads // hps, b),
        in_specs=[
            pl.BlockSpec((1, s_len, LANES), lambda hp, bi: (bi, 0, hp)),
            pl.BlockSpec((1, s_len, LANES), lambda hp, bi: (bi, 0, nblk + hp)),
            pl.BlockSpec((1, s_len, LANES), lambda hp, bi: (bi, 0, 2 * nblk + hp)),
            pl.BlockSpec((1, dh), lambda hp, bi: (0, 0)),
            pl.BlockSpec((1, dh), lambda hp, bi: (0, 0)),
            pl.BlockSpec((hps, n_delta, blk, blk), lambda hp, bi: (hp, 0, 0, 0)),
        ],
        out_specs=pl.BlockSpec((1, s_len, LANES), lambda hp, bi: (bi, 0, hp)),
        scratch_shapes=[pltpu.VMEM((LANES, s_len), F32)],
        compiler_params=pltpu.CompilerParams(
            dimension_semantics=("parallel", "arbitrary"), vmem_limit_bytes=VMEM_LIMIT),
        name="moba_attention",
    )(proj3, proj3, proj3, q_gain.reshape(1, dh), k_gain.reshape(1, dh), bias_t)


def _xattn_body(q_ref, kv_ref, qg_ref, kg_ref, o_ref, *, dh, chunk):
    s_len = q_ref.shape[1]
    width = q_ref.shape[2]
    n_heads = width // dh
    scale = dh ** -0.5
    for h in range(n_heads):
        sl = slice(h * dh, (h + 1) * dh)
        kn = _rms(kv_ref[0, :, sl], kg_ref[...]).astype(BF16)
        v = kv_ref[0, :, width + h * dh:width + (h + 1) * dh].astype(BF16)
        for c in range(s_len // chunk):
            rs = slice(c * chunk, (c + 1) * chunk)
            qn = _rms(q_ref[0, rs, sl].astype(F32), qg_ref[...]).astype(BF16)
            s = lax.dot_general(qn, kn, _NT, preferred_element_type=F32) * scale
            m = jnp.max(s, axis=-1, keepdims=True)
            p = jnp.exp(s - m)
            denom = jnp.sum(p, axis=-1, keepdims=True)
            o = jnp.dot(p.astype(BF16), v, preferred_element_type=F32) / denom
            o_ref[0, rs, sl] = o.astype(o_ref.dtype)


def _xattn(proj3, q_col_block, kv3, q_gain, k_gain, dh):
    b, s_len, _ = proj3.shape
    m_len, kv_width = kv3.shape[1], kv3.shape[2]
    width = kv_width // 2
    return pl.pallas_call(
        functools.partial(_xattn_body, dh=dh, chunk=512),
        out_shape=jax.ShapeDtypeStruct((b, s_len, width), BF16),
        grid=(b,),
        in_specs=[
            pl.BlockSpec((1, s_len, width), lambda bi: (bi, 0, q_col_block)),
            pl.BlockSpec((1, m_len, kv_width), lambda bi: (bi, 0, 0)),
            pl.BlockSpec((1, dh), lambda bi: (0, 0)),
            pl.BlockSpec((1, dh), lambda bi: (0, 0)),
        ],
        out_specs=pl.BlockSpec((1, s_len, width), lambda bi: (bi, 0, 0)),
        compiler_params=pltpu.CompilerParams(
            dimension_semantics=("parallel",), vmem_limit_bytes=VMEM_LIMIT),
        name="cross_attention",
    )(proj3, kv3, q_gain.reshape(1, dh), k_gain.reshape(1, dh))


def _merge_body(x_ref, ya_ref, u_ref, b_ref, c_ref, uh_ref, ch_ref, yx_ref, gt_ref, cw_ref,
                wa_ref, wc_ref, wx_ref, wo_ref, o_ref, *, tiles_per_seq):
    i = pl.program_id(0)
    tm, d = x_ref.shape
    halo = uh_ref.shape[0]
    z = c_ref[...].astype(F32) * u_ref[...].astype(F32)
    zh = ch_ref[...].astype(F32) * uh_ref[...].astype(F32)
    zh = jnp.where(i % tiles_per_seq == 0, 0.0, zh)
    row = lax.broadcasted_iota(jnp.int32, z.shape, 0)
    z1 = pltpu.roll(z, 1, 0)
    z1 = jnp.where(row == 0, zh[halo - 1:halo, :], z1)
    z2 = pltpu.roll(z, 2, 0)
    z2 = jnp.where(row == 0, zh[halo - 2:halo - 1, :], jnp.where(row == 1, zh[halo - 1:halo, :], z2))
    conv = cw_ref[0:1, :] * z2 + cw_ref[1:2, :] * z1 + cw_ref[2:3, :] * z
    y_c = jnp.dot((b_ref[...].astype(F32) * conv).astype(BF16), wc_ref[...], preferred_element_type=F32)
    y_a = jnp.dot(ya_ref[...], wa_ref[...], preferred_element_type=F32)
    y_x = jnp.dot(yx_ref[...], wx_ref[...], preferred_element_type=F32)
    merged = (jax.nn.sigmoid(gt_ref[:, 0:d].astype(F32)) * y_a
              + jax.nn.sigmoid(gt_ref[:, d:2 * d].astype(F32)) * y_c
              + jax.nn.sigmoid(gt_ref[:, 2 * d:3 * d].astype(F32)) * y_x)
    o_ref[...] = x_ref[...] + jnp.dot(merged.astype(BF16), wo_ref[...], preferred_element_type=F32)


def _merge(x2d, y_a, proj, conv_cols, y_x, gates, conv_w, w_a, w_c, w_x, w_o, s_len, tm):
    t, d = x2d.shape
    cwid = conv_w.shape[1]
    cu, cb, cc = conv_cols
    halo = BF16_SUBLANES
    rows_per_tile = tm // halo

    def halo_map(col):
        return lambda i: (jnp.maximum(i * rows_per_tile - 1, 0), col)

    full = lambda shape: pl.BlockSpec(shape, lambda i: (0, 0))
    return pl.pallas_call(
        functools.partial(_merge_body, tiles_per_seq=s_len // tm),
        out_shape=jax.ShapeDtypeStruct((t, d), F32),
        grid=(t // tm,),
        in_specs=[
            pl.BlockSpec((tm, d), lambda i: (i, 0)),
            pl.BlockSpec((tm, y_a.shape[1]), lambda i: (i, 0)),
            pl.BlockSpec((tm, cwid), lambda i: (i, cu)),
            pl.BlockSpec((tm, cwid), lambda i: (i, cb)),
            pl.BlockSpec((tm, cwid), lambda i: (i, cc)),
            pl.BlockSpec((halo, cwid), halo_map(cu)),
            pl.BlockSpec((halo, cwid), halo_map(cc)),
            pl.BlockSpec((tm, y_x.shape[1]), lambda i: (i, 0)),
            pl.BlockSpec((tm, gates.shape[1]), lambda i: (i, 0)),
            full(conv_w.shape), full(w_a.shape), full(w_c.shape), full(w_x.shape), full(w_o.shape),
        ],
        out_specs=pl.BlockSpec((tm, d), lambda i: (i, 0)),
        compiler_params=pltpu.CompilerParams(
            dimension_semantics=("parallel",), vmem_limit_bytes=VMEM_LIMIT),
        name="branch_merge",
    )(x2d, y_a, proj, proj, proj, proj, proj, y_x, gates, conv_w, w_a, w_c, w_x, w_o)


def _oddeven_merge_sort_pairs(n):
    pairs = []
    p = 1
    while p < n:
        k = p
        while k >= 1:
            for j in range(k % p, n - k, 2 * k):
                for i in range(min(k, n - j - k)):
                    if (i + j) // (2 * p) == (i + j + k) // (2 * p):
                        pairs.append((i + j, i + j + k))
            k //= 2
        p *= 2
    return pairs


def _bitonic_merge_pairs(n):
    pairs = []
    d = n // 2
    while d >= 1:
        pairs += [(i, i + d) for i in range(n) if i & d == 0]
        d //= 2
    return pairs


def _compare_exchange(slabs, pairs):
    for a, b in pairs:
        hi = jnp.maximum(slabs[a], slabs[b])
        lo = jnp.minimum(slabs[a], slabs[b])
        slabs[a], slabs[b] = hi, lo
    return slabs


def _top_sorted(slabs, k):
    n = len(slabs)
    slabs = _compare_exchange(list(slabs), _oddeven_merge_sort_pairs(n))
    merge = _bitonic_merge_pairs(k)
    for shift in (4, 2, 1):
        rolled = [pltpu.roll(s, shift, 0) for s in slabs]
        if len(slabs) < k:
            slabs = slabs + rolled[::-1]
        else:
            slabs = [jnp.maximum(slabs[i], rolled[k - 1 - i]) for i in range(k)]
        slabs = _compare_exchange(slabs, merge)
    return slabs


def _hyperbola(k):
    return [(i, j) for i in range(k) for j in range(k) if (i + 1) * (j + 1) <= k]


def _peer_scores(h, keys_ref, qry_ref, s2_ref, e1_ref, e2_ref, thr_ref, *, tt, lane_w):
    k = PEER_TOPK
    nkeys = keys_ref.shape[2]
    pairs = _hyperbola(k)
    n_slab = -(-len(pairs) // SUBLANES)
    n_slab = k // 2 if n_slab <= k // 2 else k
    sub = lax.broadcasted_iota(jnp.int32, (SUBLANES, lane_w), 0)
    inf = float("inf")
    for ts in range(tt // lane_w):
        cols = slice(ts * lane_w, (ts + 1) * lane_w)
        q1 = qry_ref[cols, pl.ds(pl.multiple_of(2 * h * nkeys, nkeys), nkeys)]
        q2 = qry_ref[cols, pl.ds(pl.multiple_of((2 * h + 1) * nkeys, nkeys), nkeys)]
        s1 = lax.dot_general(keys_ref[h, 0], q1, _NT, preferred_element_type=F32)
        s2 = lax.dot_general(keys_ref[h, 1], q2, _NT, preferred_element_type=F32)
        a = _top_sorted([s1[SUBLANES * v:SUBLANES * (v + 1)] for v in range(nkeys // SUBLANES)], k)
        b = _top_sorted([s2[SUBLANES * v:SUBLANES * (v + 1)] for v in range(nkeys // SUBLANES)], k)
        sums = {(i, j): a[i] + b[j] for (i, j) in pairs}
        packed = []
        for v in range(n_slab):
            slab = jnp.full((SUBLANES, lane_w), -inf, F32)
            for s in range(SUBLANES):
                idx = v * SUBLANES + s
                if idx < len(pairs):
                    slab = jnp.where(sub == s, sums[pairs[idx]], slab)
            packed.append(slab)
        tau = _top_sorted(packed, k)[k - 1]
        top = a[0] + b[0]
        z = jnp.zeros((SUBLANES, lane_w), F32)
        for slab in packed:
            z = z + jnp.where(slab >= tau, jnp.exp(slab - top), 0.0)
        zsum = jnp.sum(z, axis=0, keepdims=True)
        thr = jnp.full((nkeys, lane_w), inf, F32)
        for i in range(k - 1, -1, -1):
            t = jnp.full((SUBLANES, lane_w), inf, F32)
            for j in range(k):
                if (i, j) in sums:
                    t = jnp.minimum(t, jnp.where(sums[(i, j)] >= tau, b[j], inf))
            thr = jnp.where(s1 >= a[i][0:1, :], t[0:1, :], thr)
        thr_ref[h, ts] = thr
        s2_ref[h, ts] = s2
        e1_ref[h, ts] = jnp.exp(s1 - a[0][0:1, :])
        e2_ref[h, ts] = jnp.exp(s2 - b[0][0:1, :]) / zsum


def _peer_weights(key_blk, key_off, col0, ncols, act_ref, p_ref, s2_ref, e1_ref, e2_ref, thr_ref, *, lane_w):
    n_heads, _, nkeys, _ = s2_ref.shape
    ec = act_ref.shape[0]
    key_rows = pl.ds(pl.multiple_of(key_blk * SUBLANES, SUBLANES), SUBLANES)

    def unit(ts, il):
        r = key_off + il
        rows = slice(il * nkeys, (il + 1) * nkeys)
        cols = slice(ts * lane_w, (ts + 1) * lane_w)
        w = jnp.zeros((nkeys, lane_w), F32)
        for h in range(n_heads):
            thr = thr_ref[h, ts, key_rows, :][r:r + 1, :]
            e1 = e1_ref[h, ts, key_rows, :][r:r + 1, :]
            w = w + jnp.where(s2_ref[h, ts] >= thr, e2_ref[h, ts], 0.0) * e1
        a = act_ref[rows, cols]
        gelu = 0.5 * a * (1.0 + lax.erf(a * (2.0 ** -0.5)))
        p_ref[rows, cols] = (w * gelu).astype(BF16)

    return [functools.partial(unit, ts, il) for ts in range(col0 // lane_w, (col0 + ncols) // lane_w)
            for il in range(ec // nkeys)]


def _peer_body(h_ref, g_ref, wq_ref, keys_ref, u_ref, vt_ref, o_ref,
               xn_ref, qry_ref, s2_ref, e1_ref, e2_ref, thr_ref,
               act0_ref, act1_ref, p0_ref, p1_ref, acc_ref, *, lane_w, n_chunks):
    s = pl.program_id(1)
    tt = h_ref.shape[0]
    n_heads, _, nkeys, _ = keys_ref.shape
    ec, d = u_ref.shape
    acts, ps = (act0_ref, act1_ref), (p0_ref, p1_ref)
    keys_per_chunk = ec // nkeys
    assert 2 * keys_per_chunk == SUBLANES and n_chunks % 2 == 0

    def compute(par, do_act, do_gate, do_out):
        act_w, act_r, p_w, p_r = acts[par], acts[1 - par], ps[1 - par], ps[par]
        key_blk = (s - 1) // 2
        key_off = keys_per_chunk * (1 - par)
        group = 2 * lane_w
        kq = d // PEER_ACT_SLICES
        kv = ec // PEER_OUT_SLICES
        for c0 in range(0, tt, group):
            cols = slice(c0, c0 + group)
            units = []
            if do_gate:
                units = _peer_weights(key_blk, key_off, c0, group, act_r, p_w, s2_ref, e1_ref, e2_ref,
                                      thr_ref, lane_w=lane_w)

            def act_part(q):
                return lax.dot_general(u_ref[:, q * kq:(q + 1) * kq], xn_ref[cols, q * kq:(q + 1) * kq],
                                       _NT, preferred_element_type=F32)

            def out_part(q):
                return jnp.dot(vt_ref[0, :, q * kv:(q + 1) * kv], p_r[q * kv:(q + 1) * kv, cols],
                               preferred_element_type=F32)

            a_parts = [(act_part, q) for q in range(PEER_ACT_SLICES)] if do_act else []
            o_parts = [(out_part, q) for q in range(PEER_OUT_SLICES)] if do_out else []
            granules = []
            while a_parts or o_parts:
                granules += a_parts[:1] + o_parts[:1] + a_parts[1:2]
                a_parts, o_parts = a_parts[2:], o_parts[1:]
            sums = {act_part: None, out_part: None}
            done = 0
            for n, unit in enumerate(units + [None]):
                while done < len(granules) and done * len(units) <= n * len(granules):
                    fn, q = granules[done]
                    part = fn(q)
                    sums[fn] = part if sums[fn] is None else sums[fn] + part
                    done += 1
                if unit is not None:
                    unit()
            if do_act:
                act_w[:, cols] = sums[act_part]
            if do_out:
                acc_ref[:, cols] += sums[out_part]

    @pl.when(s == 0)
    def _():
        xn = _rms(h_ref[...], g_ref[...]).astype(BF16)
        xn_ref[...] = xn
        qry_ref[...] = jnp.dot(xn, wq_ref[...], preferred_element_type=F32).astype(BF16)

        def head(h, carry):
            _peer_scores(h, keys_ref, qry_ref, s2_ref, e1_ref, e2_ref, thr_ref, tt=tt, lane_w=lane_w)
            return carry

        lax.fori_loop(0, n_heads, head, 0)
        acc_ref[...] = jnp.zeros(acc_ref.shape, F32)
        compute(0, True, False, False)

    @pl.when(s == 1)
    def _():
        compute(1, True, True, False)

    @pl.when((s >= 2) & (s < n_chunks) & (s % 2 == 0))
    def _():
        compute(0, True, True, True)

    @pl.when((s >= 2) & (s < n_chunks) & (s % 2 == 1))
    def _():
        compute(1, True, True, True)

    @pl.when(s == n_chunks)
    def _():
        compute(0, False, True, True)

    @pl.when(s == n_chunks + 1)
    def _():
        compute(1, False, False, True)
        o_ref[...] = h_ref[...] + acc_ref[...].T


def _peer(h2d, g, w_query, keys, u, v, tt, ec):
    t, d = h2d.shape
    v_t = v.reshape(u.shape[0] // ec, ec, d).transpose(0, 2, 1)
    n_heads, _, nkeys, dk = keys.shape
    n_chunks = u.shape[0] // ec
    qw = w_query.shape[1]
    stat = lambda: pltpu.VMEM((n_heads, tt // LANES, nkeys, LANES), F32)
    return pl.pallas_call(
        functools.partial(_peer_body, lane_w=LANES, n_chunks=n_chunks),
        out_shape=jax.ShapeDtypeStruct((t, d), F32),
        grid=(t // tt, n_chunks + 2),
        in_specs=[
            pl.BlockSpec((tt, d), lambda i, s: (i, 0)),
            pl.BlockSpec((1, d), lambda i, s: (0, 0)),
            pl.BlockSpec((d, qw), lambda i, s: (0, 0)),
            pl.BlockSpec(keys.shape, lambda i, s: (0, 0, 0, 0)),
            pl.BlockSpec((ec, d), lambda i, s: (jnp.minimum(s, n_chunks - 1), 0)),
            pl.BlockSpec((1, d, ec), lambda i, s: (jnp.clip(s - 2, 0, n_chunks - 1), 0, 0)),
        ],
        out_specs=pl.BlockSpec((tt, d), lambda i, s: (i, 0)),
        scratch_shapes=[
            pltpu.VMEM((tt, d), BF16),
            pltpu.VMEM((tt, qw), BF16),
            stat(), stat(), stat(), stat(),
            pltpu.VMEM((ec, tt), F32), pltpu.VMEM((ec, tt), F32),
            pltpu.VMEM((ec, tt), BF16), pltpu.VMEM((ec, tt), BF16),
            pltpu.VMEM((d, tt), F32),
        ],
        compiler_params=pltpu.CompilerParams(
            dimension_semantics=("parallel", "arbitrary"), vmem_limit_bytes=VMEM_LIMIT),
        name="peer",
    )(h2d, g.reshape(1, d), w_query, keys, u, v_t)


def _layer(h2d, mem2d, batch, g_mix, w_in, q_gain_a, k_gain_a, bias_t, conv_w, g_mem, w_mem_kv,
           q_gain_x, k_gain_x, w_br_attn, w_br_conv, w_br_x, w_out, g_ffn,
           peer_w_query, peer_sub_keys, peer_u, peer_v):
    t, d = h2d.shape
    s_len = t // batch
    n_heads_a = bias_t.shape[0]
    dh_a = q_gain_a.shape[0]
    attn_w = n_heads_a * dh_a
    conv_wid = conv_w.shape[1]
    dh_x = q_gain_x.shape[0]
    xattn_w = w_mem_kv.shape[1] // 2
    assert attn_w == conv_wid == xattn_w, "column blocks of the projection must have equal width"
    branch_cols = 3 * attn_w + 3 * conv_wid + xattn_w

    w_in16 = w_in.astype(BF16)
    proj, gates = _norm_matmul(h2d, g_mix, [w_in16[:, :branch_cols], w_in16[:, branch_cols:]], BF16, 512,
                               "in_proj")
    proj3 = proj.reshape(batch, s_len, branch_cols)

    y_a = _moba(proj3, q_gain_a, k_gain_a, bias_t, n_heads_a, dh_a).reshape(t, attn_w)

    (kv,) = _norm_matmul(mem2d, g_mem, [w_mem_kv.astype(BF16)], F32, 512, "mem_kv")
    kv3 = kv.reshape(batch, mem2d.shape[0] // batch, kv.shape[1])
    y_x = _xattn(proj3, 6, kv3, q_gain_x, k_gain_x, dh_x).reshape(t, xattn_w)

    h2d = _merge(h2d, y_a, proj, (3, 4, 5), y_x, gates, conv_w,
                 w_br_attn.astype(BF16), w_br_conv.astype(BF16), w_br_x.astype(BF16),
                 w_out.astype(BF16), s_len, 512)

    return _peer(h2d, g_ffn, peer_w_query.astype(BF16), peer_sub_keys.astype(BF16),
                 peer_u.astype(BF16), peer_v.astype(BF16), 512, 512)


def kernel(x, mem, g_mix, w_in, q_gain_a, k_gain_a, rel_bias, conv_w, g_mem, w_mem_kv, q_gain_x, k_gain_x, w_br_attn, w_br_conv, w_br_x, w_out, g_ffn, peer_w_query, peer_sub_keys, peer_u, peer_v):
    batch, s_len, d = x.shape
    depth = g_mix.shape[0]
    bias_t = _bias_tiles(rel_bias, s_len // MOBA_BLOCK)
    h2d = x.reshape(batch * s_len, d)
    mem2d = mem.reshape(batch * mem.shape[1], d)
    for l in range(depth):
        h2d = _layer(h2d, mem2d, batch, g_mix[l], w_in[l], q_gain_a[l], k_gain_a[l], bias_t, conv_w[l],
                     g_mem[l], w_mem_kv[l], q_gain_x[l], k_gain_x[l], w_br_attn[l], w_br_conv[l],
                     w_br_x[l], w_out[l], g_ffn[l], peer_w_query[l], peer_sub_keys[l],
                     peer_u[l], peer_v[l])
    return h2d.reshape(batch, s_len, d)
```

```python
import functools
import math

import numpy as np
import jax
import jax.numpy as jnp
from jax import lax
from jax.experimental import pallas as pl
from jax.experimental.pallas import tpu as pltpu

F32 = jnp.float32
BF16 = jnp.bfloat16

RMS_EPS = 1e-6
NEG = -1e30
LOG2E = math.log2(math.e)

MOBA_BLOCK = 256
MOBA_TOPK = 3
REL_MAX_DIST = 1024
PEER_TOPK = 16
PEER_ACT_SLICES, PEER_OUT_SLICES = 4, 2

SUBLANES = 8
BF16_SUBLANES = 16
LANES = 128
VMEM_LIMIT = 56 * 1024 * 1024

_NT = (((1,), (1,)), ((), ()))


def _rms(x, g):
    r = lax.rsqrt(jnp.mean(x * x, axis=-1, keepdims=True) + RMS_EPS)
    return x * r * g


def _norm_matmul_body(x_ref, g_ref, *refs):
    n_out = len(refs) // 2
    xn = _rms(x_ref[...], g_ref[...]).astype(BF16)
    for w_ref, o_ref in zip(refs[:n_out], refs[n_out:]):
        o_ref[...] = jnp.dot(xn, w_ref[...], preferred_element_type=F32).astype(o_ref.dtype)


def _norm_matmul(x2d, g, weights, out_dtype, tm, name):
    t, d = x2d.shape
    return pl.pallas_call(
        _norm_matmul_body,
        out_shape=[jax.ShapeDtypeStruct((t, w.shape[1]), out_dtype) for w in weights],
        grid=(t // tm,),
        in_specs=[pl.BlockSpec((tm, d), lambda i: (i, 0)), pl.BlockSpec((1, d), lambda i: (0, 0))]
        + [pl.BlockSpec(w.shape, lambda i: (0, 0)) for w in weights],
        out_specs=[pl.BlockSpec((tm, w.shape[1]), lambda i: (i, 0)) for w in weights],
        compiler_params=pltpu.CompilerParams(
            dimension_semantics=("parallel",), vmem_limit_bytes=VMEM_LIMIT),
        name=name,
    )(x2d, g.reshape(1, d), *weights)


def _rel_bucket(dist, n_buckets):
    max_exact = n_buckets // 2
    d = jnp.maximum(dist, 0)
    df = jnp.maximum(d, 1).astype(F32)
    large = max_exact + (jnp.log(df / max_exact) / math.log(REL_MAX_DIST / max_exact)
                         * (n_buckets - max_exact)).astype(jnp.int32)
    large = jnp.minimum(large, n_buckets - 1)
    return jnp.where(d < max_exact, d, large)


def _bias_tiles_body(bucket_ref, rel_ref, o_ref, *, n_buckets, n_delta):
    h = pl.program_id(0)
    blk = MOBA_BLOCK
    for delta in range(n_delta):
        bk = bucket_ref[delta:delta + 1, :]
        w = jnp.zeros(bk.shape, F32)
        for b in range(n_buckets):
            w = jnp.where(bk == b, rel_ref[b, h], w)
        x = jnp.broadcast_to(w, (blk, 2 * blk))
        r = pltpu.roll(x, blk, 1, stride=1, stride_axis=0)
        tile = r[:, :blk]
        if delta == 0:
            krow = lax.broadcasted_iota(jnp.int32, (blk, blk), 0)
            qcol = lax.broadcasted_iota(jnp.int32, (blk, blk), 1)
            tile = jnp.where(qcol >= krow, tile, NEG)
        o_ref[0, delta] = tile * LOG2E


def _bias_tiles(rel_bias, n_delta):
    n_buckets, n_heads = rel_bias.shape
    blk = MOBA_BLOCK
    dist = (jnp.arange(n_delta, dtype=jnp.int32)[:, None] * blk - blk
            + jnp.arange(2 * blk, dtype=jnp.int32)[None, :])
    bucket = _rel_bucket(dist, n_buckets).astype(jnp.int32)
    return pl.pallas_call(
        functools.partial(_bias_tiles_body, n_buckets=n_buckets, n_delta=n_delta),
        out_shape=jax.ShapeDtypeStruct((n_heads, n_delta, blk, blk), F32),
        grid=(n_heads,),
        in_specs=[
            pl.BlockSpec((n_delta, 2 * blk), lambda h: (0, 0)),
            pl.BlockSpec(memory_space=pltpu.SMEM),
        ],
        out_specs=pl.BlockSpec((1, n_delta, blk, blk), lambda h: (h, 0, 0, 0)),
        compiler_params=pltpu.CompilerParams(dimension_semantics=("arbitrary",)),
        name="rel_bias_tiles",
    )(bucket, rel_bias)


def _moba_body(q_ref, k_ref, v_ref, qg_ref, kg_ref, bias_ref, o_ref, ot_ref, logit_ref, p_ref, *, dh):
    s_len = q_ref.shape[1]
    blk = MOBA_BLOCK
    nb = s_len // blk
    scale = dh ** -0.5
    heads_per_step = q_ref.shape[2] // dh
    q_all = q_ref[0].astype(F32)
    k_all = k_ref[0].astype(F32)
    vt_all = v_ref[0].astype(F32).T
    opaque0 = pl.multiple_of(jnp.minimum(pl.program_id(1), 0) * blk, blk)
    for hh in range(heads_per_step):
        sl = slice(hh * dh, (hh + 1) * dh)
        qn = _rms(q_all[:, sl], qg_ref[...])
        kn = _rms(k_all[:, sl], kg_ref[...])
        kmean = jnp.mean(kn.reshape(nb, blk, dh), axis=1)
        qb16 = qn.astype(BF16)
        qs16 = (qn * (scale * LOG2E)).astype(BF16)
        kb16 = kn.astype(BF16)
        vt = jnp.concatenate([vt_all[sl, :], jnp.ones((BF16_SUBLANES, s_len), F32)], axis=0).astype(BF16)
        gate_t = lax.dot_general(kmean.astype(BF16), qb16, _NT, preferred_element_type=F32)
        row = lax.broadcasted_iota(jnp.int32, (nb, blk), 0)
        for qb in range(nb):
            qs = slice(qb * blk, (qb + 1) * blk)
            qt = qs16[qs]
            negm = None
            if qb > MOBA_TOPK:
                g = gate_t[:, qs]
                rank = jnp.zeros((nb, blk), jnp.int32)
                for j in range(qb):
                    gj = g[j:j + 1, :]
                    beats = jnp.where(gj > g, 1, jnp.where((gj == g) & (j < row), 1, 0))
                    rank = rank + beats
                negm = jnp.where((rank < MOBA_TOPK) & (row < qb), 0.0, NEG)
            m = None
            for kb in range(qb + 1):
                ks = slice(kb * blk, (kb + 1) * blk)
                st = lax.dot_general(kb16[ks], qt, _NT, preferred_element_type=F32)
                st = st + bias_ref[hh, qb - kb]
                if negm is not None and kb < qb:
                    st = st + negm[kb:kb + 1, :]
                logit_ref[pl.ds(opaque0 + kb * blk, blk), :] = st
                mt = jnp.max(st, axis=0, keepdims=True)
                m = mt if m is None else jnp.maximum(m, mt)
            for kb in range(qb + 1):
                ks = slice(kb * blk, (kb + 1) * blk)
                p_ref[ks, :] = jnp.exp2(logit_ref[pl.ds(opaque0 + kb * blk, blk), :] - m).astype(BF16)
            nk = (qb + 1) * blk
            o_t = jnp.dot(vt[:, :nk], p_ref[0:nk, :], preferred_element_type=F32)
            ot_ref[sl, qs] = o_t[:dh] / o_t[dh:dh + 1]
    o_ref[0] = ot_ref[...].T.astype(o_ref.dtype)


def _moba(proj3, q_gain, k_gain, bias_t, n_heads, dh):
    b, s_len, _ = proj3.shape
    hps = LANES // dh
    width = n_heads * dh
    nblk = width // LANES
    n_delta = bias_t.shape[1]
    blk = MOBA_BLOCK
    return pl.pallas_call(
        functools.partial(_moba_body, dh=dh),
        out_shape=jax.ShapeDtypeStruct((b, s_len, width), BF16),
        grid=(n_heads // hps, b),
        in_specs=[
            pl.BlockSpec((1, s_len, LANES), lambda hp, bi: (bi, 0, hp)),
            pl.BlockSpec((1, s_len, LANES), lambda hp, bi: (bi, 0, nblk + hp)),
            pl.BlockSpec((1, s_len, LANES), lambda hp, bi: (bi, 0, 2 * nblk + hp)),
            pl.BlockSpec((1, dh), lambda hp, bi: (0, 0)),
            pl.BlockSpec((1, dh), lambda hp, bi: (0, 0)),
            pl.BlockSpec((hps, n_delta, blk, blk), lambda hp, bi: (hp, 0, 0, 0)),
        ],
        out_specs=pl.BlockSpec((1, s_len, LANES), lambda hp, bi: (bi, 0, hp)),
        scratch_shapes=[pltpu.VMEM((LANES, s_len), F32),
                        pltpu.VMEM((s_len, blk), F32),
                        pltpu.VMEM((s_len, blk), BF16)],
        compiler_params=pltpu.CompilerParams(
            dimension_semantics=("parallel", "arbitrary"), vmem_limit_bytes=VMEM_LIMIT),
        name="moba_attention",
    )(proj3, proj3, proj3, q_gain.reshape(1, dh), k_gain.reshape(1, dh), bias_t)


def _xattn_body(q_ref, kv_ref, qg_ref, kg_ref, o_ref, *, dh, chunk):
    s_len = q_ref.shape[1]
    width = q_ref.shape[2]
    n_heads = width // dh
    scale = dh ** -0.5
    for h in range(n_heads):
        sl = slice(h * dh, (h + 1) * dh)
        kn = _rms(kv_ref[0, :, sl], kg_ref[...]).astype(BF16)
        v = kv_ref[0, :, width + h * dh:width + (h + 1) * dh].astype(BF16)
        for c in range(s_len // chunk):
            rs = slice(c * chunk, (c + 1) * chunk)
            qn = _rms(q_ref[0, rs, sl].astype(F32), qg_ref[...]).astype(BF16)
            s = lax.dot_general(qn, kn, _NT, preferred_element_type=F32) * scale
            m = jnp.max(s, axis=-1, keepdims=True)
            p = jnp.exp(s - m)
            denom = jnp.sum(p, axis=-1, keepdims=True)
            o = jnp.dot(p.astype(BF16), v, preferred_element_type=F32) / denom
            o_ref[0, rs, sl] = o.astype(o_ref.dtype)


def _xattn(proj3, q_col_block, kv3, q_gain, k_gain, dh):
    b, s_len, _ = proj3.shape
    m_len, kv_width = kv3.shape[1], kv3.shape[2]
    width = kv_width // 2
    return pl.pallas_call(
        functools.partial(_xattn_body, dh=dh, chunk=512),
        out_shape=jax.ShapeDtypeStruct((b, s_len, width), BF16),
        grid=(b,),
        in_specs=[
            pl.BlockSpec((1, s_len, width), lambda bi: (bi, 0, q_col_block)),
            pl.BlockSpec((1, m_len, kv_width), lambda bi: (bi, 0, 0)),
            pl.BlockSpec((1, dh), lambda bi: (0, 0)),
            pl.BlockSpec((1, dh), lambda bi: (0, 0)),
        ],
        out_specs=pl.BlockSpec((1, s_len, width), lambda bi: (bi, 0, 0)),
        compiler_params=pltpu.CompilerParams(
            dimension_semantics=("parallel",), vmem_limit_bytes=VMEM_LIMIT),
        name="cross_attention",
    )(proj3, kv3, q_gain.reshape(1, dh), k_gain.reshape(1, dh))


def _merge_body(x_ref, ya_ref, u_ref, b_ref, c_ref, uh_ref, ch_ref, yx_ref, gt_ref, cw_ref,
                wa_ref, wc_ref, wx_ref, wo_ref, o_ref, *, tiles_per_seq):
    i = pl.program_id(0)
    tm, d = x_ref.shape
    halo = uh_ref.shape[0]
    z = c_ref[...].astype(F32) * u_ref[...].astype(F32)
    zh = ch_ref[...].astype(F32) * uh_ref[...].astype(F32)
    zh = jnp.where(i % tiles_per_seq == 0, 0.0, zh)
    row = lax.broadcasted_iota(jnp.int32, z.shape, 0)
    z1 = pltpu.roll(z, 1, 0)
    z1 = jnp.where(row == 0, zh[halo - 1:halo, :], z1)
    z2 = pltpu.roll(z, 2, 0)
    z2 = jnp.where(row == 0, zh[halo - 2:halo - 1, :], jnp.where(row == 1, zh[halo - 1:halo, :], z2))
    conv = cw_ref[0:1, :] * z2 + cw_ref[1:2, :] * z1 + cw_ref[2:3, :] * z
    y_c = jnp.dot((b_ref[...].astype(F32) * conv).astype(BF16), wc_ref[...], preferred_element_type=F32)
    y_a = jnp.dot(ya_ref[...], wa_ref[...], preferred_element_type=F32)
    y_x = jnp.dot(yx_ref[...], wx_ref[...], preferred_element_type=F32)
    merged = (jax.nn.sigmoid(gt_ref[:, 0:d].astype(F32)) * y_a
              + jax.nn.sigmoid(gt_ref[:, d:2 * d].astype(F32)) * y_c
              + jax.nn.sigmoid(gt_ref[:, 2 * d:3 * d].astype(F32)) * y_x)
    o_ref[...] = x_ref[...] + jnp.dot(merged.astype(BF16), wo_ref[...], preferred_element_type=F32)


def _merge(x2d, y_a, proj, conv_cols, y_x, gates, conv_w, w_a, w_c, w_x, w_o, s_len, tm):
    t, d = x2d.shape
    cwid = conv_w.shape[1]
    cu, cb, cc = conv_cols
    halo = BF16_SUBLANES
    rows_per_tile = tm // halo

    def halo_map(col):
        return lambda i: (jnp.maximum(i * rows_per_tile - 1, 0), col)

    full = lambda shape: pl.BlockSpec(shape, lambda i: (0, 0))
    return pl.pallas_call(
        functools.partial(_merge_body, tiles_per_seq=s_len // tm),
        out_shape=jax.ShapeDtypeStruct((t, d), F32),
        grid=(t // tm,),
        in_specs=[
            pl.BlockSpec((tm, d), lambda i: (i, 0)),
            pl.BlockSpec((tm, y_a.shape[1]), lambda i: (i, 0)),
            pl.BlockSpec((tm, cwid), lambda i: (i, cu)),
            pl.BlockSpec((tm, cwid), lambda i: (i, cb)),
            pl.BlockSpec((tm, cwid), lambda i: (i, cc)),
            pl.BlockSpec((halo, cwid), halo_map(cu)),
            pl.BlockSpec((halo, cwid), halo_map(cc)),
            pl.BlockSpec((tm, y_x.shape[1]), lambda i: (i, 0)),
            pl.BlockSpec((tm, gates.shape[1]), lambda i: (i, 0)),
            full(conv_w.shape), full(w_a.shape), full(w_c.shape), full(w_x.shape), full(w_o.shape),
        ],
        out_specs=pl.BlockSpec((tm, d), lambda i: (i, 0)),
        compiler_params=pltpu.CompilerParams(
            dimension_semantics=("parallel",), vmem_limit_bytes=VMEM_LIMIT),
        name="branch_merge",
    )(x2d, y_a, proj, proj, proj, proj, proj, y_x, gates, conv_w, w_a, w_c, w_x, w_o)


def _oddeven_merge_sort_pairs(n):
    pairs = []
    p = 1
    while p < n:
        k = p
        while k >= 1:
            for j in range(k % p, n - k, 2 * k):
                for i in range(min(k, n - j - k)):
                    if (i + j) // (2 * p) == (i + j + k) // (2 * p):
                        pairs.append((i + j, i + j + k))
            k //= 2
        p *= 2
    return pairs


def _bitonic_merge_pairs(n):
    pairs = []
    d = n // 2
    while d >= 1:
        pairs += [(i, i + d) for i in range(n) if i & d == 0]
        d //= 2
    return pairs


def _compare_exchange(slabs, pairs):
    for a, b in pairs:
        hi = jnp.maximum(slabs[a], slabs[b])
        lo = jnp.minimum(slabs[a], slabs[b])
        slabs[a], slabs[b] = hi, lo
    return slabs


def _top_sorted(slabs, k):
    n = len(slabs)
    slabs = _compare_exchange(list(slabs), _oddeven_merge_sort_pairs(n))
    merge = _bitonic_merge_pairs(k)
    for shift in (4, 2, 1):
        rolled = [pltpu.roll(s, shift, 0) for s in slabs]
        if len(slabs) < k:
            slabs = slabs + rolled[::-1]
        else:
            slabs = [jnp.maximum(slabs[i], rolled[k - 1 - i]) for i in range(k)]
        slabs = _compare_exchange(slabs, merge)
    return slabs


def _hyperbola(k):
    return [(i, j) for i in range(k) for j in range(k) if (i + 1) * (j + 1) <= k]


def _peer_scores(h, keys_ref, qry_ref, s2_ref, e1_ref, e2_ref, thr_ref, *, tt, lane_w):
    k = PEER_TOPK
    nkeys = keys_ref.shape[2]
    pairs = _hyperbola(k)
    n_slab = -(-len(pairs) // SUBLANES)
    n_slab = k // 2 if n_slab <= k // 2 else k
    sub = lax.broadcasted_iota(jnp.int32, (SUBLANES, lane_w), 0)
    inf = float("inf")
    for ts in range(tt // lane_w):
        cols = slice(ts * lane_w, (ts + 1) * lane_w)
        q1 = qry_ref[cols, pl.ds(pl.multiple_of(2 * h * nkeys, nkeys), nkeys)]
        q2 = qry_ref[cols, pl.ds(pl.multiple_of((2 * h + 1) * nkeys, nkeys), nkeys)]
        s1 = lax.dot_general(keys_ref[h, 0], q1, _NT, preferred_element_type=F32)
        s2 = lax.dot_general(keys_ref[h, 1], q2, _NT, preferred_element_type=F32)
        a = _top_sorted([s1[SUBLANES * v:SUBLANES * (v + 1)] for v in range(nkeys // SUBLANES)], k)
        b = _top_sorted([s2[SUBLANES * v:SUBLANES * (v + 1)] for v in range(nkeys // SUBLANES)], k)
        sums = {(i, j): a[i] + b[j] for (i, j) in pairs}
        packed = []
        for v in range(n_slab):
            slab = jnp.full((SUBLANES, lane_w), -inf, F32)
            for s in range(SUBLANES):
                idx = v * SUBLANES + s
                if idx < len(pairs):
                    slab = jnp.where(sub == s, sums[pairs[idx]], slab)
            packed.append(slab)
        tau = _top_sorted(packed, k)[k - 1]
        top = a[0] + b[0]
        z = jnp.zeros((SUBLANES, lane_w), F32)
        for slab in packed:
            z = z + jnp.where(slab >= tau, jnp.exp(slab - top), 0.0)
        zsum = jnp.sum(z, axis=0, keepdims=True)
        thr = jnp.full((nkeys, lane_w), inf, F32)
        for i in range(k - 1, -1, -1):
            t = jnp.full((SUBLANES, lane_w), inf, F32)
            for j in range(k):
                if (i, j) in sums:
                    t = jnp.minimum(t, jnp.where(sums[(i, j)] >= tau, b[j], inf))
            thr = jnp.where(s1 >= a[i][0:1, :], t[0:1, :], thr)
        thr_ref[h, ts] = thr
        s2_ref[h, ts] = s2
        e1_ref[h, ts] = jnp.exp(s1 - a[0][0:1, :])
        e2_ref[h, ts] = jnp.exp(s2 - b[0][0:1, :]) / zsum


def _peer_weights(key_blk, key_off, col0, ncols, act_ref, p_ref, s2_ref, e1_ref, e2_ref, thr_ref, *, lane_w):
    n_heads, _, nkeys, _ = s2_ref.shape
    ec = act_ref.shape[0]
    key_rows = pl.ds(pl.multiple_of(key_blk * SUBLANES, SUBLANES), SUBLANES)

    def unit(ts, il):
        r = key_off + il
        rows = slice(il * nkeys, (il + 1) * nkeys)
        cols = slice(ts * lane_w, (ts + 1) * lane_w)
        w = jnp.zeros((nkeys, lane_w), F32)
        for h in range(n_heads):
            thr = thr_ref[h, ts, key_rows, :][r:r + 1, :]
            e1 = e1_ref[h, ts, key_rows, :][r:r + 1, :]
            w = w + jnp.where(s2_ref[h, ts] >= thr, e2_ref[h, ts], 0.0) * e1
        a = act_ref[rows, cols]
        gelu = 0.5 * a * (1.0 + lax.erf(a * (2.0 ** -0.5)))
        p_ref[rows, cols] = (w * gelu).astype(BF16)

    return [functools.partial(unit, ts, il) for ts in range(col0 // lane_w, (col0 + ncols) // lane_w)
            for il in range(ec // nkeys)]


def _peer_body(h_ref, g_ref, wq_ref, keys_ref, u_ref, vt_ref, o_ref,
               xn_ref, qry_ref, s2_ref, e1_ref, e2_ref, thr_ref,
               act0_ref, act1_ref, p0_ref, p1_ref, acc_ref, *, lane_w, n_chunks):
    s = pl.program_id(1)
    tt = h_ref.shape[0]
    n_heads, _, nkeys, _ = keys_ref.shape
    ec, d = u_ref.shape
    acts, ps = (act0_ref, act1_ref), (p0_ref, p1_ref)
    keys_per_chunk = ec // nkeys
    assert 2 * keys_per_chunk == SUBLANES and n_chunks % 2 == 0

    def compute(par, do_act, do_gate, do_out):
        act_w, act_r, p_w, p_r = acts[par], acts[1 - par], ps[1 - par], ps[par]
        key_blk = (s - 1) // 2
        key_off = keys_per_chunk * (1 - par)
        group = 2 * lane_w
        kq = d // PEER_ACT_SLICES
        kv = ec // PEER_OUT_SLICES
        for c0 in range(0, tt, group):
            cols = slice(c0, c0 + group)
            units = []
            if do_gate:
                units = _peer_weights(key_blk, key_off, c0, group, act_r, p_w, s2_ref, e1_ref, e2_ref,
                                      thr_ref, lane_w=lane_w)

            def act_part(q):
                return lax.dot_general(u_ref[:, q * kq:(q + 1) * kq], xn_ref[cols, q * kq:(q + 1) * kq],
                                       _NT, preferred_element_type=F32)

            def out_part(q):
                return jnp.dot(vt_ref[0, :, q * kv:(q + 1) * kv], p_r[q * kv:(q + 1) * kv, cols],
                               preferred_element_type=F32)

            a_parts = [(act_part, q) for q in range(PEER_ACT_SLICES)] if do_act else []
            o_parts = [(out_part, q) for q in range(PEER_OUT_SLICES)] if do_out else []
            granules = []
            while a_parts or o_parts:
                granules += a_parts[:1] + o_parts[:1] + a_parts[1:2]
                a_parts, o_parts = a_parts[2:], o_parts[1:]
            sums = {act_part: None, out_part: None}
            done = 0
            for n, unit in enumerate(units + [None]):
                while done < len(granules) and done * len(units) <= n * len(granules):
                    fn, q = granules[done]
                    part = fn(q)
                    sums[fn] = part if sums[fn] is None else sums[fn] + part
                    done += 1
                if unit is not None:
                    unit()
            if do_act:
                act_w[:, cols] = sums[act_part]
            if do_out:
                acc_ref[:, cols] += sums[out_part]

    @pl.when(s == 0)
    def _():
        xn = _rms(h_ref[...], g_ref[...]).astype(BF16)
        xn_ref[...] = xn
        qry_ref[...] = jnp.dot(xn, wq_ref[...], preferred_element_type=F32).astype(BF16)

        def head(h, carry):
            _peer_scores(h, keys_ref, qry_ref, s2_ref, e1_ref, e2_ref, thr_ref, tt=tt, lane_w=lane_w)
            return carry

        lax.fori_loop(0, n_heads, head, 0)
        acc_ref[...] = jnp.zeros(acc_ref.shape, F32)
        compute(0, True, False, False)

    @pl.when(s == 1)
    def _():
        compute(1, True, True, False)

    @pl.when((s >= 2) & (s < n_chunks) & (s % 2 == 0))
    def _():
        compute(0, True, True, True)

    @pl.when((s >= 2) & (s < n_chunks) & (s % 2 == 1))
    def _():
        compute(1, True, True, True)

    @pl.when(s == n_chunks)
    def _():
        compute(0, False, True, True)

    @pl.when(s == n_chunks + 1)
    def _():
        compute(1, False, False, True)
        o_ref[...] = h_ref[...] + acc_ref[...].T


def _peer(h2d, g, w_query, keys, u, v, tt, ec):
    t, d = h2d.shape
    v_t = v.reshape(u.shape[0] // ec, ec, d).transpose(0, 2, 1)
    n_heads, _, nkeys, dk = keys.shape
    n_chunks = u.shape[0] // ec
    qw = w_query.shape[1]
    stat = lambda: pltpu.VMEM((n_heads, tt // LANES, nkeys, LANES), F32)
    return pl.pallas_call(
        functools.partial(_peer_body, lane_w=LANES, n_chunks=n_chunks),
        out_shape=jax.ShapeDtypeStruct((t, d), F32),
        grid=(t // tt, n_chunks + 2),
        in_specs=[
            pl.BlockSpec((tt, d), lambda i, s: (i, 0)),
            pl.BlockSpec((1, d), lambda i, s: (0, 0)),
            pl.BlockSpec((d, qw), lambda i, s: (0, 0)),
            pl.BlockSpec(keys.shape, lambda i, s: (0, 0, 0, 0)),
            pl.BlockSpec((ec, d), lambda i, s: (jnp.minimum(s, n_chunks - 1), 0)),
            pl.BlockSpec((1, d, ec), lambda i, s: (jnp.clip(s - 2, 0, n_chunks - 1), 0, 0)),
        ],
        out_specs=pl.BlockSpec((tt, d), lambda i, s: (i, 0)),
        scratch_shapes=[
            pltpu.VMEM((tt, d), BF16),
            pltpu.VMEM((tt, qw), BF16),
            stat(), stat(), stat(), stat(),
            pltpu.VMEM((ec, tt), F32), pltpu.VMEM((ec, tt), F32),
            pltpu.VMEM((ec, tt), BF16), pltpu.VMEM((ec, tt), BF16),
            pltpu.VMEM((d, tt), F32),
        ],
        compiler_params=pltpu.CompilerParams(
            dimension_semantics=("parallel", "arbitrary"), vmem_limit_bytes=VMEM_LIMIT),
        name="peer",
    )(h2d, g.reshape(1, d), w_query, keys, u, v_t)


def _layer(h2d, mem2d, batch, g_mix, w_in, q_gain_a, k_gain_a, bias_t, conv_w, g_mem, w_mem_kv,
           q_gain_x, k_gain_x, w_br_attn, w_br_conv, w_br_x, w_out, g_ffn,
           peer_w_query, peer_sub_keys, peer_u, peer_v):
    t, d = h2d.shape
    s_len = t // batch
    n_heads_a = bias_t.shape[0]
    dh_a = q_gain_a.shape[0]
    attn_w = n_heads_a * dh_a
    conv_wid = conv_w.shape[1]
    dh_x = q_gain_x.shape[0]
    xattn_w = w_mem_kv.shape[1] // 2
    assert attn_w == conv_wid == xattn_w, "column blocks of the projection must have equal width"
    branch_cols = 3 * attn_w + 3 * conv_wid + xattn_w

    w_in16 = w_in.astype(BF16)
    proj, gates = _norm_matmul(h2d, g_mix, [w_in16[:, :branch_cols], w_in16[:, branch_cols:]], BF16, 512,
                               "in_proj")
    proj3 = proj.reshape(batch, s_len, branch_cols)

    y_a = _moba(proj3, q_gain_a, k_gain_a, bias_t, n_heads_a, dh_a).reshape(t, attn_w)

    (kv,) = _norm_matmul(mem2d, g_mem, [w_mem_kv.astype(BF16)], F32, 512, "mem_kv")
    kv3 = kv.reshape(batch, mem2d.shape[0] // batch, kv.shape[1])
    y_x = _xattn(proj3, 6, kv3, q_gain_x, k_gain_x, dh_x).reshape(t, xattn_w)

    h2d = _merge(h2d, y_a, proj, (3, 4, 5), y_x, gates, conv_w,
                 w_br_attn.astype(BF16), w_br_conv.astype(BF16), w_br_x.astype(BF16),
                 w_out.astype(BF16), s_len, 512)

    return _peer(h2d, g_ffn, peer_w_query.astype(BF16), peer_sub_keys.astype(BF16),
                 peer_u.astype(BF16), peer_v.astype(BF16), 512, 512)


def kernel(x, mem, g_mix, w_in, q_gain_a, k_gain_a, rel_bias, conv_w, g_mem, w_mem_kv, q_gain_x, k_gain_x, w_br_attn, w_br_conv, w_br_x, w_out, g_ffn, peer_w_query, peer_sub_keys, peer_u, peer_v):
    batch, s_len, d = x.shape
    depth = g_mix.shape[0]
    bias_t = _bias_tiles(rel_bias, s_len // MOBA_BLOCK)
    h2d = x.reshape(batch * s_len, d)
    mem2d = mem.reshape(batch * mem.shape[1], d)
    for l in range(depth):
        h2d = _layer(h2d, mem2d, batch, g_mix[l], w_in[l], q_gain_a[l], k_gain_a[l], bias_t, conv_w[l],
                     g_mem[l], w_mem_kv[l], q_gain_x[l], k_gain_x[l], w_br_attn[l], w_br_conv[l],
                     w_br_x[l], w_out[l], g_ffn[l], peer_w_query[l], peer_sub_keys[l],
                     peer_u[l], peer_v[l])
    return h2d.reshape(batch, s_len, d)
```

```python
import functools
import math

import numpy as np
import jax
import jax.numpy as jnp
from jax import lax
from jax.experimental import pallas as pl
from jax.experimental.pallas import tpu as pltpu

F32 = jnp.float32
BF16 = jnp.bfloat16

RMS_EPS = 1e-6
NEG = -1e30
LOG2E = math.log2(math.e)

MOBA_BLOCK = 256
MOBA_TOPK = 3
REL_MAX_DIST = 1024
PEER_TOPK = 16
PEER_ACT_SLICES, PEER_OUT_SLICES = 2, 1

SUBLANES = 8
BF16_SUBLANES = 16
LANES = 128
VMEM_LIMIT = 56 * 1024 * 1024

_NT = (((1,), (1,)), ((), ()))


def _rms(x, g):
    r = lax.rsqrt(jnp.mean(x * x, axis=-1, keepdims=True) + RMS_EPS)
    return x * r * g


def _norm_matmul_body(x_ref, g_ref, *refs):
    n_out = len(refs) // 2
    xn = _rms(x_ref[...], g_ref[...]).astype(BF16)
    for w_ref, o_ref in zip(refs[:n_out], refs[n_out:]):
        o_ref[...] = jnp.dot(xn, w_ref[...], preferred_element_type=F32).astype(o_ref.dtype)


def _norm_matmul(x2d, g, weights, out_dtype, tm, name):
    t, d = x2d.shape
    return pl.pallas_call(
        _norm_matmul_body,
        out_shape=[jax.ShapeDtypeStruct((t, w.shape[1]), out_dtype) for w in weights],
        grid=(t // tm,),
        in_specs=[pl.BlockSpec((tm, d), lambda i: (i, 0)), pl.BlockSpec((1, d), lambda i: (0, 0))]
        + [pl.BlockSpec(w.shape, lambda i: (0, 0)) for w in weights],
        out_specs=[pl.BlockSpec((tm, w.shape[1]), lambda i: (i, 0)) for w in weights],
        compiler_params=pltpu.CompilerParams(
            dimension_semantics=("parallel",), vmem_limit_bytes=VMEM_LIMIT),
        name=name,
    )(x2d, g.reshape(1, d), *weights)


def _rel_bucket(dist, n_buckets):
    max_exact = n_buckets // 2
    d = jnp.maximum(dist, 0)
    df = jnp.maximum(d, 1).astype(F32)
    large = max_exact + (jnp.log(df / max_exact) / math.log(REL_MAX_DIST / max_exact)
                         * (n_buckets - max_exact)).astype(jnp.int32)
    large = jnp.minimum(large, n_buckets - 1)
    return jnp.where(d < max_exact, d, large)


def _bias_tiles_body(bucket_ref, rel_ref, o_ref, *, n_buckets, n_delta):
    h = pl.program_id(0)
    blk = MOBA_BLOCK
    for delta in range(n_delta):
        bk = bucket_ref[delta:delta + 1, :]
        w = jnp.zeros(bk.shape, F32)
        for b in range(n_buckets):
            w = jnp.where(bk == b, rel_ref[b, h], w)
        x = jnp.broadcast_to(w, (blk, 2 * blk))
        r = pltpu.roll(x, blk, 1, stride=1, stride_axis=0)
        tile = r[:, :blk]
        if delta == 0:
            krow = lax.broadcasted_iota(jnp.int32, (blk, blk), 0)
            qcol = lax.broadcasted_iota(jnp.int32, (blk, blk), 1)
            tile = jnp.where(qcol >= krow, tile, NEG)
        o_ref[0, delta] = tile * LOG2E


def _bias_tiles(rel_bias, n_delta):
    n_buckets, n_heads = rel_bias.shape
    blk = MOBA_BLOCK
    dist = (jnp.arange(n_delta, dtype=jnp.int32)[:, None] * blk - blk
            + jnp.arange(2 * blk, dtype=jnp.int32)[None, :])
    bucket = _rel_bucket(dist, n_buckets).astype(jnp.int32)
    return pl.pallas_call(
        functools.partial(_bias_tiles_body, n_buckets=n_buckets, n_delta=n_delta),
        out_shape=jax.ShapeDtypeStruct((n_heads, n_delta, blk, blk), F32),
        grid=(n_heads,),
        in_specs=[
            pl.BlockSpec((n_delta, 2 * blk), lambda h: (0, 0)),
            pl.BlockSpec(memory_space=pltpu.SMEM),
        ],
        out_specs=pl.BlockSpec((1, n_delta, blk, blk), lambda h: (h, 0, 0, 0)),
        compiler_params=pltpu.CompilerParams(dimension_semantics=("arbitrary",)),
        name="rel_bias_tiles",
    )(bucket, rel_bias)


def _moba_body(q_ref, k_ref, v_ref, qg_ref, kg_ref, bias_ref, o_ref, ot_ref, logit_ref, p_ref, *, dh):
    s_len = q_ref.shape[1]
    blk = MOBA_BLOCK
    nb = s_len // blk
    scale = dh ** -0.5
    heads_per_step = q_ref.shape[2] // dh
    q_all = q_ref[0].astype(F32)
    k_all = k_ref[0].astype(F32)
    vt_all = v_ref[0].astype(F32).T
    opaque0 = pl.multiple_of(jnp.minimum(pl.program_id(1), 0) * blk, blk)
    for hh in range(heads_per_step):
        sl = slice(hh * dh, (hh + 1) * dh)
        qn = _rms(q_all[:, sl], qg_ref[...])
        kn = _rms(k_all[:, sl], kg_ref[...])
        kmean = jnp.mean(kn.reshape(nb, blk, dh), axis=1)
        qb16 = qn.astype(BF16)
        qs16 = (qn * (scale * LOG2E)).astype(BF16)
        kb16 = kn.astype(BF16)
        vt = jnp.concatenate([vt_all[sl, :], jnp.ones((BF16_SUBLANES, s_len), F32)], axis=0).astype(BF16)
        gate_t = lax.dot_general(kmean.astype(BF16), qb16, _NT, preferred_element_type=F32)
        row = lax.broadcasted_iota(jnp.int32, (nb, blk), 0)
        for qb in range(nb):
            qs = slice(qb * blk, (qb + 1) * blk)
            qt = qs16[qs]
            negm = None
            if qb > MOBA_TOPK:
                g = gate_t[:, qs]
                rank = jnp.zeros((nb, blk), jnp.int32)
                for j in range(qb):
                    gj = g[j:j + 1, :]
                    beats = jnp.where(gj > g, 1, jnp.where((gj == g) & (j < row), 1, 0))
                    rank = rank + beats
                negm = jnp.where((rank < MOBA_TOPK) & (row < qb), 0.0, NEG)
            m = None
            for kb in range(qb + 1):
                ks = slice(kb * blk, (kb + 1) * blk)
                st = lax.dot_general(kb16[ks], qt, _NT, preferred_element_type=F32)
                st = st + bias_ref[hh, qb - kb]
                if negm is not None and kb < qb:
                    st = st + negm[kb:kb + 1, :]
                logit_ref[pl.ds(opaque0 + kb * blk, blk), :] = st
                mt = jnp.max(st, axis=0, keepdims=True)
                m = mt if m is None else jnp.maximum(m, mt)
            for kb in range(qb + 1):
                ks = slice(kb * blk, (kb + 1) * blk)
                p_ref[ks, :] = jnp.exp2(logit_ref[pl.ds(opaque0 + kb * blk, blk), :] - m).astype(BF16)
            nk = (qb + 1) * blk
            o_t = jnp.dot(vt[:, :nk], p_ref[0:nk, :], preferred_element_type=F32)
            ot_ref[sl, qs] = o_t[:dh] / o_t[dh:dh + 1]
    o_ref[0] = ot_ref[...].T.astype(o_ref.dtype)


def _moba(proj3, q_gain, k_gain, bias_t, n_heads, dh):
    b, s_len, _ = proj3.shape
    hps = LANES // dh
    width = n_heads * dh
    nblk = width // LANES
    n_delta = bias_t.shape[1]
    blk = MOBA_BLOCK
    return pl.pallas_call(
        functools.partial(_moba_body, dh=dh),
        out_shape=jax.ShapeDtypeStruct((b, s_len, width), BF16),
        grid=(n_heads // hps, b),
        in_specs=[
            pl.BlockSpec((1, s_len, LANES), lambda hp, bi: (bi, 0, hp)),
            pl.BlockSpec((1, s_len, LANES), lambda hp, bi: (bi, 0, nblk + hp)),
            pl.BlockSpec((1, s_len, LANES), lambda hp, bi: (bi, 0, 2 * nblk + hp)),
            pl.BlockSpec((1, dh), lambda hp, bi: (0, 0)),
            pl.BlockSpec((1, dh), lambda hp, bi: (0, 0)),
            pl.BlockSpec((hps, n_delta, blk, blk), lambda hp, bi: (hp, 0, 0, 0)),
        ],
        out_specs=pl.BlockSpec((1, s_len, LANES), lambda hp, bi: (bi, 0, hp)),
        scratch_shapes=[pltpu.VMEM((LANES, s_len), F32),
                        pltpu.VMEM((s_len, blk), F32),
                        pltpu.VMEM((s_len, blk), BF16)],
        compiler_params=pltpu.CompilerParams(
            dimension_semantics=("parallel", "arbitrary"), vmem_limit_bytes=VMEM_LIMIT),
        name="moba_attention",
    )(proj3, proj3, proj3, q_gain.reshape(1, dh), k_gain.reshape(1, dh), bias_t)


def _xattn_body(q_ref, kv_ref, qg_ref, kg_ref, o_ref, *, dh, chunk):
    s_len = q_ref.shape[1]
    width = q_ref.shape[2]
    n_heads = width // dh
    scale = dh ** -0.5
    for h in range(n_heads):
        sl = slice(h * dh, (h + 1) * dh)
        kn = _rms(kv_ref[0, :, sl], kg_ref[...]).astype(BF16)
        v = kv_ref[0, :, width + h * dh:width + (h + 1) * dh].astype(BF16)
        for c in range(s_len // chunk):
            rs = slice(c * chunk, (c + 1) * chunk)
            qn = _rms(q_ref[0, rs, sl].astype(F32), qg_ref[...]).astype(BF16)
            s = lax.dot_general(qn, kn, _NT, preferred_element_type=F32) * scale
            m = jnp.max(s, axis=-1, keepdims=True)
            p = jnp.exp(s - m)
            denom = jnp.sum(p, axis=-1, keepdims=True)
            o = jnp.dot(p.astype(BF16), v, preferred_element_type=F32) / denom
            o_ref[0, rs, sl] = o.astype(o_ref.dtype)


def _xattn(proj3, q_col_block, kv3, q_gain, k_gain, dh):
    b, s_len, _ = proj3.shape
    m_len, kv_width = kv3.shape[1], kv3.shape[2]
    width = kv_width // 2
    return pl.pallas_call(
        functools.partial(_xattn_body, dh=dh, chunk=512),
        out_shape=jax.ShapeDtypeStruct((b, s_len, width), BF16),
        grid=(b,),
        in_specs=[
            pl.BlockSpec((1, s_len, width), lambda bi: (bi, 0, q_col_block)),
            pl.BlockSpec((1, m_len, kv_width), lambda bi: (bi, 0, 0)),
            pl.BlockSpec((1, dh), lambda bi: (0, 0)),
            pl.BlockSpec((1, dh), lambda bi: (0, 0)),
        ],
        out_specs=pl.BlockSpec((1, s_len, width), lambda bi: (bi, 0, 0)),
        compiler_params=pltpu.CompilerParams(
            dimension_semantics=("parallel",), vmem_limit_bytes=VMEM_LIMIT),
        name="cross_attention",
    )(proj3, kv3, q_gain.reshape(1, dh), k_gain.reshape(1, dh))


def _merge_body(x_ref, ya_ref, u_ref, b_ref, c_ref, uh_ref, ch_ref, yx_ref, gt_ref, cw_ref,
                wa_ref, wc_ref, wx_ref, wo_ref, o_ref, *, tiles_per_seq):
    i = pl.program_id(0)
    tm, d = x_ref.shape
    halo = uh_ref.shape[0]
    z = c_ref[...].astype(F32) * u_ref[...].astype(F32)
    zh = ch_ref[...].astype(F32) * uh_ref[...].astype(F32)
    zh = jnp.where(i % tiles_per_seq == 0, 0.0, zh)
    row = lax.broadcasted_iota(jnp.int32, z.shape, 0)
    z1 = pltpu.roll(z, 1, 0)
    z1 = jnp.where(row == 0, zh[halo - 1:halo, :], z1)
    z2 = pltpu.roll(z, 2, 0)
    z2 = jnp.where(row == 0, zh[halo - 2:halo - 1, :], jnp.where(row == 1, zh[halo - 1:halo, :], z2))
    conv = cw_ref[0:1, :] * z2 + cw_ref[1:2, :] * z1 + cw_ref[2:3, :] * z
    y_c = jnp.dot((b_ref[...].astype(F32) * conv).astype(BF16), wc_ref[...], preferred_element_type=F32)
    y_a = jnp.dot(ya_ref[...], wa_ref[...], preferred_element_type=F32)
    y_x = jnp.dot(yx_ref[...], wx_ref[...], preferred_element_type=F32)
    merged = (jax.nn.sigmoid(gt_ref[:, 0:d].astype(F32)) * y_a
              + jax.nn.sigmoid(gt_ref[:, d:2 * d].astype(F32)) * y_c
              + jax.nn.sigmoid(gt_ref[:, 2 * d:3 * d].astype(F32)) * y_x)
    o_ref[...] = x_ref[...] + jnp.dot(merged.astype(BF16), wo_ref[...], preferred_element_type=F32)


def _merge(x2d, y_a, proj, conv_cols, y_x, gates, conv_w, w_a, w_c, w_x, w_o, s_len, tm):
    t, d = x2d.shape
    cwid = conv_w.shape[1]
    cu, cb, cc = conv_cols
    halo = BF16_SUBLANES
    rows_per_tile = tm // halo

    def halo_map(col):
        return lambda i: (jnp.maximum(i * rows_per_tile - 1, 0), col)

    full = lambda shape: pl.BlockSpec(shape, lambda i: (0, 0))
    return pl.pallas_call(
        functools.partial(_merge_body, tiles_per_seq=s_len // tm),
        out_shape=jax.ShapeDtypeStruct((t, d), F32),
        grid=(t // tm,),
        in_specs=[
            pl.BlockSpec((tm, d), lambda i: (i, 0)),
            pl.BlockSpec((tm, y_a.shape[1]), lambda i: (i, 0)),
            pl.BlockSpec((tm, cwid), lambda i: (i, cu)),
            pl.BlockSpec((tm, cwid), lambda i: (i, cb)),
            pl.BlockSpec((tm, cwid), lambda i: (i, cc)),
            pl.BlockSpec((halo, cwid), halo_map(cu)),
            pl.BlockSpec((halo, cwid), halo_map(cc)),
            pl.BlockSpec((tm, y_x.shape[1]), lambda i: (i, 0)),
            pl.BlockSpec((tm, gates.shape[1]), lambda i: (i, 0)),
            full(conv_w.shape), full(w_a.shape), full(w_c.shape), full(w_x.shape), full(w_o.shape),
        ],
        out_specs=pl.BlockSpec((tm, d), lambda i: (i, 0)),
        compiler_params=pltpu.CompilerParams(
            dimension_semantics=("parallel",), vmem_limit_bytes=VMEM_LIMIT),
        name="branch_merge",
    )(x2d, y_a, proj, proj, proj, proj, proj, y_x, gates, conv_w, w_a, w_c, w_x, w_o)


def _oddeven_merge_sort_pairs(n):
    pairs = []
    p = 1
    while p < n:
        k = p
        while k >= 1:
            for j in range(k % p, n - k, 2 * k):
                for i in range(min(k, n - j - k)):
                    if (i + j) // (2 * p) == (i + j + k) // (2 * p):
                        pairs.append((i + j, i + j + k))
            k //= 2
        p *= 2
    return pairs


def _bitonic_merge_pairs(n):
    pairs = []
    d = n // 2
    while d >= 1:
        pairs += [(i, i + d) for i in range(n) if i & d == 0]
        d //= 2
    return pairs


def _compare_exchange(slabs, pairs):
    for a, b in pairs:
        hi = jnp.maximum(slabs[a], slabs[b])
        lo = jnp.minimum(slabs[a], slabs[b])
        slabs[a], slabs[b] = hi, lo
    return slabs


def _top_sorted(slabs, k):
    n = len(slabs)
    slabs = _compare_exchange(list(slabs), _oddeven_merge_sort_pairs(n))
    merge = _bitonic_merge_pairs(k)
    for shift in (4, 2, 1):
        rolled = [pltpu.roll(s, shift, 0) for s in slabs]
        if len(slabs) < k:
            slabs = slabs + rolled[::-1]
        else:
            slabs = [jnp.maximum(slabs[i], rolled[k - 1 - i]) for i in range(k)]
        slabs = _compare_exchange(slabs, merge)
    return slabs


def _hyperbola(k):
    return [(i, j) for i in range(k) for j in range(k) if (i + 1) * (j + 1) <= k]


def _peer_scores(h, keys_ref, qry_ref, s2_ref, e1_ref, e2_ref, thr_ref, *, tt, lane_w):
    k = PEER_TOPK
    nkeys = keys_ref.shape[2]
    pairs = _hyperbola(k)
    n_slab = -(-len(pairs) // SUBLANES)
    n_slab = k // 2 if n_slab <= k // 2 else k
    sub = lax.broadcasted_iota(jnp.int32, (SUBLANES, lane_w), 0)
    inf = float("inf")
    for ts in range(tt // lane_w):
        cols = slice(ts * lane_w, (ts + 1) * lane_w)
        q1 = qry_ref[cols, pl.ds(pl.multiple_of(2 * h * nkeys, nkeys), nkeys)]
        q2 = qry_ref[cols, pl.ds(pl.multiple_of((2 * h + 1) * nkeys, nkeys), nkeys)]
        s1 = lax.dot_general(keys_ref[h, 0], q1, _NT, preferred_element_type=F32)
        s2 = lax.dot_general(keys_ref[h, 1], q2, _NT, preferred_element_type=F32)
        a = _top_sorted([s1[SUBLANES * v:SUBLANES * (v + 1)] for v in range(nkeys // SUBLANES)], k)
        b = _top_sorted([s2[SUBLANES * v:SUBLANES * (v + 1)] for v in range(nkeys // SUBLANES)], k)
        sums = {(i, j): a[i] + b[j] for (i, j) in pairs}
        packed = []
        for v in range(n_slab):
            slab = jnp.full((SUBLANES, lane_w), -inf, F32)
            for s in range(SUBLANES):
                idx = v * SUBLANES + s
                if idx < len(pairs):
                    slab = jnp.where(sub == s, sums[pairs[idx]], slab)
            packed.append(slab)
        tau = _top_sorted(packed, k)[k - 1]
        top = a[0] + b[0]
        z = jnp.zeros((SUBLANES, lane_w), F32)
        for slab in packed:
            z = z + jnp.where(slab >= tau, jnp.exp(slab - top), 0.0)
        zsum = jnp.sum(z, axis=0, keepdims=True)
        thr = jnp.full((nkeys, lane_w), inf, F32)
        for i in range(k - 1, -1, -1):
            t = jnp.full((SUBLANES, lane_w), inf, F32)
            for j in range(k):
                if (i, j) in sums:
                    t = jnp.minimum(t, jnp.where(sums[(i, j)] >= tau, b[j], inf))
            thr = jnp.where(s1 >= a[i][0:1, :], t[0:1, :], thr)
        thr_ref[h, ts] = thr
        s2_ref[h, ts] = s2
        e1_ref[h, ts] = jnp.exp(s1 - a[0][0:1, :])
        e2_ref[h, ts] = jnp.exp(s2 - b[0][0:1, :]) / zsum


def _peer_weights(key_blk, key_off, col0, ncols, act_ref, p_ref, s2_ref, e1_ref, e2_ref, thr_ref, *, lane_w):
    n_heads, _, nkeys, _ = s2_ref.shape
    ec = act_ref.shape[0]
    key_rows = pl.ds(pl.multiple_of(key_blk * SUBLANES, SUBLANES), SUBLANES)

    def unit(ts, il):
        r = key_off + il
        rows = slice(il * nkeys, (il + 1) * nkeys)
        cols = slice(ts * lane_w, (ts + 1) * lane_w)
        w = jnp.zeros((nkeys, lane_w), F32)
        for h in range(n_heads):
            thr = thr_ref[h, ts, key_rows, :][r:r + 1, :]
            e1 = e1_ref[h, ts, key_rows, :][r:r + 1, :]
            w = w + jnp.where(s2_ref[h, ts] >= thr, e2_ref[h, ts], 0.0) * e1
        a = act_ref[rows, cols]
        gelu = 0.5 * a * (1.0 + lax.erf(a * (2.0 ** -0.5)))
        p_ref[rows, cols] = (w * gelu).astype(BF16)

    return [functools.partial(unit, ts, il) for ts in range(col0 // lane_w, (col0 + ncols) // lane_w)
            for il in range(ec // nkeys)]


def _peer_body(h_ref, g_ref, wq_ref, keys_ref, u_ref, vt_ref, o_ref,
               xn_ref, qry_ref, s2_ref, e1_ref, e2_ref, thr_ref,
               act0_ref, act1_ref, p0_ref, p1_ref, acc_ref, *, lane_w, n_chunks):
    s = pl.program_id(1)
    tt = h_ref.shape[0]
    n_heads, _, nkeys, _ = keys_ref.shape
    ec, d = u_ref.shape
    acts, ps = (act0_ref, act1_ref), (p0_ref, p1_ref)
    keys_per_chunk = ec // nkeys
    assert 2 * keys_per_chunk == SUBLANES and n_chunks % 2 == 0

    def compute(par, do_act, do_gate, do_out):
        act_w, act_r, p_w, p_r = acts[par], acts[1 - par], ps[1 - par], ps[par]
        key_blk = (s - 1) // 2
        key_off = keys_per_chunk * (1 - par)
        group = 2 * lane_w
        kq = d // PEER_ACT_SLICES
        kv = ec // PEER_OUT_SLICES
        for c0 in range(0, tt, group):
            cols = slice(c0, c0 + group)
            units = []
            if do_gate:
                units = _peer_weights(key_blk, key_off, c0, group, act_r, p_w, s2_ref, e1_ref, e2_ref,
                                      thr_ref, lane_w=lane_w)

            def act_part(q):
                return lax.dot_general(u_ref[:, q * kq:(q + 1) * kq], xn_ref[cols, q * kq:(q + 1) * kq],
                                       _NT, preferred_element_type=F32)

            def out_part(q):
                return jnp.dot(vt_ref[0, :, q * kv:(q + 1) * kv], p_r[q * kv:(q + 1) * kv, cols],
                               preferred_element_type=F32)

            a_parts = [(act_part, q) for q in range(PEER_ACT_SLICES)] if do_act else []
            o_parts = [(out_part, q) for q in range(PEER_OUT_SLICES)] if do_out else []
            granules = []
            while a_parts or o_parts:
                granules += a_parts[:1] + o_parts[:1] + a_parts[1:2]
                a_parts, o_parts = a_parts[2:], o_parts[1:]
            sums = {act_part: None, out_part: None}
            done = 0
            for n, unit in enumerate(units + [None]):
                while done < len(granules) and done * len(units) <= n * len(granules):
                    fn, q = granules[done]
                    part = fn(q)
                    sums[fn] = part if sums[fn] is None else sums[fn] + part
                    done += 1
                if unit is not None:
                    unit()
            if do_act:
                act_w[:, cols] = sums[act_part]
            if do_out:
                acc_ref[:, cols] += sums[out_part]

    @pl.when(s == 0)
    def _():
        xn = _rms(h_ref[...], g_ref[...]).astype(BF16)
        xn_ref[...] = xn
        qry_ref[...] = jnp.dot(xn, wq_ref[...], preferred_element_type=F32).astype(BF16)

        def head(h, carry):
            _peer_scores(h, keys_ref, qry_ref, s2_ref, e1_ref, e2_ref, thr_ref, tt=tt, lane_w=lane_w)
            return carry

        lax.fori_loop(0, n_heads, head, 0)
        acc_ref[...] = jnp.zeros(acc_ref.shape, F32)
        compute(0, True, False, False)

    @pl.when(s == 1)
    def _():
        compute(1, True, True, False)

    @pl.when((s >= 2) & (s < n_chunks) & (s % 2 == 0))
    def _():
        compute(0, True, True, True)

    @pl.when((s >= 2) & (s < n_chunks) & (s % 2 == 1))
    def _():
        compute(1, True, True, True)

    @pl.when(s == n_chunks)
    def _():
        compute(0, False, True, True)

    @pl.when(s == n_chunks + 1)
    def _():
        compute(1, False, False, True)
        o_ref[...] = h_ref[...] + acc_ref[...].T


def _peer(h2d, g, w_query, keys, u, v, tt, ec):
    t, d = h2d.shape
    v_t = v.reshape(u.shape[0] // ec, ec, d).transpose(0, 2, 1)
    n_heads, _, nkeys, dk = keys.shape
    n_chunks = u.shape[0] // ec
    qw = w_query.shape[1]
    stat = lambda: pltpu.VMEM((n_heads, tt // LANES, nkeys, LANES), F32)
    return pl.pallas_call(
        functools.partial(_peer_body, lane_w=LANES, n_chunks=n_chunks),
        out_shape=jax.ShapeDtypeStruct((t, d), F32),
        grid=(t // tt, n_chunks + 2),
        in_specs=[
            pl.BlockSpec((tt, d), lambda i, s: (i, 0)),
            pl.BlockSpec((1, d), lambda i, s: (0, 0)),
            pl.BlockSpec((d, qw), lambda i, s: (0, 0)),
            pl.BlockSpec(keys.shape, lambda i, s: (0, 0, 0, 0)),
            pl.BlockSpec((ec, d), lambda i, s: (jnp.minimum(s, n_chunks - 1), 0)),
            pl.BlockSpec((1, d, ec), lambda i, s: (jnp.clip(s - 2, 0, n_chunks - 1), 0, 0)),
        ],
        out_specs=pl.BlockSpec((tt, d), lambda i, s: (i, 0)),
        scratch_shapes=[
            pltpu.VMEM((tt, d), BF16),
            pltpu.VMEM((tt, qw), BF16),
            stat(), stat(), stat(), stat(),
            pltpu.VMEM((ec, tt), F32), pltpu.VMEM((ec, tt), F32),
            pltpu.VMEM((ec, tt), BF16), pltpu.VMEM((ec, tt), BF16),
            pltpu.VMEM((d, tt), F32),
        ],
        compiler_params=pltpu.CompilerParams(
            dimension_semantics=("parallel", "arbitrary"), vmem_limit_bytes=VMEM_LIMIT),
        name="peer",
    )(h2d, g.reshape(1, d), w_query, keys, u, v_t)


def _layer(h2d, mem2d, batch, g_mix, w_in, q_gain_a, k_gain_a, bias_t, conv_w, g_mem, w_mem_kv,
           q_gain_x, k_gain_x, w_br_attn, w_br_conv, w_br_x, w_out, g_ffn,
           peer_w_query, peer_sub_keys, peer_u, peer_v):
    t, d = h2d.shape
    s_len = t // batch
    n_heads_a = bias_t.shape[0]
    dh_a = q_gain_a.shape[0]
    attn_w = n_heads_a * dh_a
    conv_wid = conv_w.shape[1]
    dh_x = q_gain_x.shape[0]
    xattn_w = w_mem_kv.shape[1] // 2
    assert attn_w == conv_wid == xattn_w, "column blocks of the projection must have equal width"
    branch_cols = 3 * attn_w + 3 * conv_wid + xattn_w

    w_in16 = w_in.astype(BF16)
    proj, gates = _norm_matmul(h2d, g_mix, [w_in16[:, :branch_cols], w_in16[:, branch_cols:]], BF16, 512,
                               "in_proj")
    proj3 = proj.reshape(batch, s_len, branch_cols)

    y_a = _moba(proj3, q_gain_a, k_gain_a, bias_t, n_heads_a, dh_a).reshape(t, attn_w)

    (kv,) = _norm_matmul(mem2d, g_mem, [w_mem_kv.astype(BF16)], F32, 512, "mem_kv")
    kv3 = kv.reshape(batch, mem2d.shape[0] // batch, kv.shape[1])
    y_x = _xattn(proj3, 6, kv3, q_gain_x, k_gain_x, dh_x).reshape(t, xattn_w)

    h2d = _merge(h2d, y_a, proj, (3, 4, 5), y_x, gates, conv_w,
                 w_br_attn.astype(BF16), w_br_conv.astype(BF16), w_br_x.astype(BF16),
                 w_out.astype(BF16), s_len, 512)

    return _peer(h2d, g_ffn, peer_w_query.astype(BF16), peer_sub_keys.astype(BF16),
                 peer_u.astype(BF16), peer_v.astype(BF16), 512, 512)


def kernel(x, mem, g_mix, w_in, q_gain_a, k_gain_a, rel_bias, conv_w, g_mem, w_mem_kv, q_gain_x, k_gain_x, w_br_attn, w_br_conv, w_br_x, w_out, g_ffn, peer_w_query, peer_sub_keys, peer_u, peer_v):
    batch, s_len, d = x.shape
    depth = g_mix.shape[0]
    bias_t = _bias_tiles(rel_bias, s_len // MOBA_BLOCK)
    h2d = x.reshape(batch * s_len, d)
    mem2d = mem.reshape(batch * mem.shape[1], d)
    for l in range(depth):
        h2d = _layer(h2d, mem2d, batch, g_mix[l], w_in[l], q_gain_a[l], k_gain_a[l], bias_t, conv_w[l],
                     g_mem[l], w_mem_kv[l], q_gain_x[l], k_gain_x[l], w_br_attn[l], w_br_conv[l],
                     w_br_x[l], w_out[l], g_ffn[l], peer_w_query[l], peer_sub_keys[l],
                     peer_u[l], peer_v[l])
    return h2d.reshape(batch, s_len, d)
```

```python
import functools
import math

import numpy as np
import jax
import jax.numpy as jnp
from jax import lax
from jax.experimental import pallas as pl
from jax.experimental.pallas import tpu as pltpu

F32 = jnp.float32
BF16 = jnp.bfloat16

RMS_EPS = 1e-6
NEG = -1e30
LOG2E = math.log2(math.e)

MOBA_BLOCK = 256
MOBA_TOPK = 3
REL_MAX_DIST = 1024
PEER_TOPK = 16
PEER_ACT_SLICES, PEER_OUT_SLICES = 4, 1

SUBLANES = 8
BF16_SUBLANES = 16
LANES = 128
VMEM_LIMIT = 56 * 1024 * 1024

_NT = (((1,), (1,)), ((), ()))


def _rms(x, g):
    r = lax.rsqrt(jnp.mean(x * x, axis=-1, keepdims=True) + RMS_EPS)
    return x * r * g


def _norm_matmul_body(x_ref, g_ref, *refs):
    n_out = len(refs) // 2
    xn = _rms(x_ref[...], g_ref[...]).astype(BF16)
    for w_ref, o_ref in zip(refs[:n_out], refs[n_out:]):
        o_ref[...] = jnp.dot(xn, w_ref[...], preferred_element_type=F32).astype(o_ref.dtype)


def _norm_matmul(x2d, g, weights, out_dtype, tm, name):
    t, d = x2d.shape
    return pl.pallas_call(
        _norm_matmul_body,
        out_shape=[jax.ShapeDtypeStruct((t, w.shape[1]), out_dtype) for w in weights],
        grid=(t // tm,),
        in_specs=[pl.BlockSpec((tm, d), lambda i: (i, 0)), pl.BlockSpec((1, d), lambda i: (0, 0))]
        + [pl.BlockSpec(w.shape, lambda i: (0, 0)) for w in weights],
        out_specs=[pl.BlockSpec((tm, w.shape[1]), lambda i: (i, 0)) for w in weights],
        compiler_params=pltpu.CompilerParams(
            dimension_semantics=("parallel",), vmem_limit_bytes=VMEM_LIMIT),
        name=name,
    )(x2d, g.reshape(1, d), *weights)


def _rel_bucket(dist, n_buckets):
    max_exact = n_buckets // 2
    d = jnp.maximum(dist, 0)
    df = jnp.maximum(d, 1).astype(F32)
    large = max_exact + (jnp.log(df / max_exact) / math.log(REL_MAX_DIST / max_exact)
                         * (n_buckets - max_exact)).astype(jnp.int32)
    large = jnp.minimum(large, n_buckets - 1)
    return jnp.where(d < max_exact, d, large)


def _bias_tiles_body(bucket_ref, rel_ref, o_ref, *, n_buckets, n_delta):
    h = pl.program_id(0)
    blk = MOBA_BLOCK
    for delta in range(n_delta):
        bk = bucket_ref[delta:delta + 1, :]
        w = jnp.zeros(bk.shape, F32)
        for b in range(n_buckets):
            w = jnp.where(bk == b, rel_ref[b, h], w)
        x = jnp.broadcast_to(w, (blk, 2 * blk))
        r = pltpu.roll(x, blk, 1, stride=1, stride_axis=0)
        tile = r[:, :blk]
        if delta == 0:
            krow = lax.broadcasted_iota(jnp.int32, (blk, blk), 0)
            qcol = lax.broadcasted_iota(jnp.int32, (blk, blk), 1)
            tile = jnp.where(qcol >= krow, tile, NEG)
        o_ref[0, delta] = tile * LOG2E


def _bias_tiles(rel_bias, n_delta):
    n_buckets, n_heads = rel_bias.shape
    blk = MOBA_BLOCK
    dist = (jnp.arange(n_delta, dtype=jnp.int32)[:, None] * blk - blk
            + jnp.arange(2 * blk, dtype=jnp.int32)[None, :])
    bucket = _rel_bucket(dist, n_buckets).astype(jnp.int32)
    return pl.pallas_call(
        functools.partial(_bias_tiles_body, n_buckets=n_buckets, n_delta=n_delta),
        out_shape=jax.ShapeDtypeStruct((n_heads, n_delta, blk, blk), F32),
        grid=(n_heads,),
        in_specs=[
            pl.BlockSpec((n_delta, 2 * blk), lambda h: (0, 0)),
            pl.BlockSpec(memory_space=pltpu.SMEM),
        ],
        out_specs=pl.BlockSpec((1, n_delta, blk, blk), lambda h: (h, 0, 0, 0)),
        compiler_params=pltpu.CompilerParams(dimension_semantics=("arbitrary",)),
        name="rel_bias_tiles",
    )(bucket, rel_bias)


def _moba_body(q_ref, k_ref, v_ref, qg_ref, kg_ref, bias_ref, o_ref, ot_ref, logit_ref, p_ref, *, dh):
    s_len = q_ref.shape[1]
    blk = MOBA_BLOCK
    nb = s_len // blk
    scale = dh ** -0.5
    heads_per_step = q_ref.shape[2] // dh
    q_all = q_ref[0].astype(F32)
    k_all = k_ref[0].astype(F32)
    vt_all = v_ref[0].astype(F32).T
    opaque0 = pl.multiple_of(jnp.minimum(pl.program_id(1), 0) * blk, blk)
    for hh in range(heads_per_step):
        sl = slice(hh * dh, (hh + 1) * dh)
        qn = _rms(q_all[:, sl], qg_ref[...])
        kn = _rms(k_all[:, sl], kg_ref[...])
        kmean = jnp.mean(kn.reshape(nb, blk, dh), axis=1)
        qb16 = qn.astype(BF16)
        qs16 = (qn * (scale * LOG2E)).astype(BF16)
        kb16 = kn.astype(BF16)
        vt = jnp.concatenate([vt_all[sl, :], jnp.ones((BF16_SUBLANES, s_len), F32)], axis=0).astype(BF16)
        gate_t = lax.dot_general(kmean.astype(BF16), qb16, _NT, preferred_element_type=F32)
        row = lax.broadcasted_iota(jnp.int32, (nb, blk), 0)
        for qb in range(nb):
            qs = slice(qb * blk, (qb + 1) * blk)
            qt = qs16[qs]
            negm = None
            if qb > MOBA_TOPK:
                g = gate_t[:, qs]
                rank = jnp.zeros((nb, blk), jnp.int32)
                for j in range(qb):
                    gj = g[j:j + 1, :]
                    beats = jnp.where(gj > g, 1, jnp.where((gj == g) & (j < row), 1, 0))
                    rank = rank + beats
                negm = jnp.where((rank < MOBA_TOPK) & (row < qb), 0.0, NEG)
            m = None
            for kb in range(qb + 1):
                ks = slice(kb * blk, (kb + 1) * blk)
                st = lax.dot_general(kb16[ks], qt, _NT, preferred_element_type=F32)
                st = st + bias_ref[hh, qb - kb]
                if negm is not None and kb < qb:
                    st = st + negm[kb:kb + 1, :]
                logit_ref[pl.ds(opaque0 + kb * blk, blk), :] = st
                mt = jnp.max(st, axis=0, keepdims=True)
                m = mt if m is None else jnp.maximum(m, mt)
            for kb in range(qb + 1):
                ks = slice(kb * blk, (kb + 1) * blk)
                p_ref[ks, :] = jnp.exp2(logit_ref[pl.ds(opaque0 + kb * blk, blk), :] - m).astype(BF16)
            nk = (qb + 1) * blk
            o_t = jnp.dot(vt[:, :nk], p_ref[0:nk, :], preferred_element_type=F32)
            ot_ref[sl, qs] = o_t[:dh] / o_t[dh:dh + 1]
    o_ref[0] = ot_ref[...].T.astype(o_ref.dtype)


def _moba(proj3, q_gain, k_gain, bias_t, n_heads, dh):
    b, s_len, _ = proj3.shape
    hps = LANES // dh
    width = n_heads * dh
    nblk = width // LANES
    n_delta = bias_t.shape[1]
    blk = MOBA_BLOCK
    return pl.pallas_call(
        functools.partial(_moba_body, dh=dh),
        out_shape=jax.ShapeDtypeStruct((b, s_len, width), BF16),
        grid=(n_heads // hps, b),
        in_specs=[
            pl.BlockSpec((1, s_len, LANES), lambda hp, bi: (bi, 0, hp)),
            pl.BlockSpec((1, s_len, LANES), lambda hp, bi: (bi, 0, nblk + hp)),
            pl.BlockSpec((1, s_len, LANES), lambda hp, bi: (bi, 0, 2 * nblk + hp)),
            pl.BlockSpec((1, dh), lambda hp, bi: (0, 0)),
            pl.BlockSpec((1, dh), lambda hp, bi: (0, 0)),
            pl.BlockSpec((hps, n_delta, blk, blk), lambda hp, bi: (hp, 0, 0, 0)),
        ],
        out_specs=pl.BlockSpec((1, s_len, LANES), lambda hp, bi: (bi, 0, hp)),
        scratch_shapes=[pltpu.VMEM((LANES, s_len), F32),
                        pltpu.VMEM((s_len, blk), F32),
                        pltpu.VMEM((s_len, blk), BF16)],
        compiler_params=pltpu.CompilerParams(
            dimension_semantics=("parallel", "arbitrary"), vmem_limit_bytes=VMEM_LIMIT),
        name="moba_attention",
    )(proj3, proj3, proj3, q_gain.reshape(1, dh), k_gain.reshape(1, dh), bias_t)


def _xattn_body(q_ref, kv_ref, qg_ref, kg_ref, o_ref, *, dh, chunk):
    s_len = q_ref.shape[1]
    width = q_ref.shape[2]
    n_heads = width // dh
    scale = dh ** -0.5
    for h in range(n_heads):
        sl = slice(h * dh, (h + 1) * dh)
        kn = _rms(kv_ref[0, :, sl], kg_ref[...]).astype(BF16)
        v = kv_ref[0, :, width + h * dh:width + (h + 1) * dh].astype(BF16)
        for c in range(s_len // chunk):
            rs = slice(c * chunk, (c + 1) * chunk)
            qn = _rms(q_ref[0, rs, sl].astype(F32), qg_ref[...]).astype(BF16)
            s = lax.dot_general(qn, kn, _NT, preferred_element_type=F32) * scale
            m = jnp.max(s, axis=-1, keepdims=True)
            p = jnp.exp(s - m)
            denom = jnp.sum(p, axis=-1, keepdims=True)
            o = jnp.dot(p.astype(BF16), v, preferred_element_type=F32) / denom
            o_ref[0, rs, sl] = o.astype(o_ref.dtype)


def _xattn(proj3, q_col_block, kv3, q_gain, k_gain, dh):
    b, s_len, _ = proj3.shape
    m_len, kv_width = kv3.shape[1], kv3.shape[2]
    width = kv_width // 2
    return pl.pallas_call(
        functools.partial(_xattn_body, dh=dh, chunk=512),
        out_shape=jax.ShapeDtypeStruct((b, s_len, width), BF16),
        grid=(b,),
        in_specs=[
            pl.BlockSpec((1, s_len, width), lambda bi: (bi, 0, q_col_block)),
            pl.BlockSpec((1, m_len, kv_width), lambda bi: (bi, 0, 0)),
            pl.BlockSpec((1, dh), lambda bi: (0, 0)),
            pl.BlockSpec((1, dh), lambda bi: (0, 0)),
        ],
        out_specs=pl.BlockSpec((1, s_len, width), lambda bi: (bi, 0, 0)),
        compiler_params=pltpu.CompilerParams(
            dimension_semantics=("parallel",), vmem_limit_bytes=VMEM_LIMIT),
        name="cross_attention",
    )(proj3, kv3, q_gain.reshape(1, dh), k_gain.reshape(1, dh))


def _merge_body(x_ref, ya_ref, u_ref, b_ref, c_ref, uh_ref, ch_ref, yx_ref, gt_ref, cw_ref,
                wa_ref, wc_ref, wx_ref, wo_ref, o_ref, *, tiles_per_seq):
    i = pl.program_id(0)
    tm, d = x_ref.shape
    halo = uh_ref.shape[0]
    z = c_ref[...].astype(F32) * u_ref[...].astype(F32)
    zh = ch_ref[...].astype(F32) * uh_ref[...].astype(F32)
    zh = jnp.where(i % tiles_per_seq == 0, 0.0, zh)
    row = lax.broadcasted_iota(jnp.int32, z.shape, 0)
    z1 = pltpu.roll(z, 1, 0)
    z1 = jnp.where(row == 0, zh[halo - 1:halo, :], z1)
    z2 = pltpu.roll(z, 2, 0)
    z2 = jnp.where(row == 0, zh[halo - 2:halo - 1, :], jnp.where(row == 1, zh[halo - 1:halo, :], z2))
    conv = cw_ref[0:1, :] * z2 + cw_ref[1:2, :] * z1 + cw_ref[2:3, :] * z
    y_c = jnp.dot((b_ref[...].astype(F32) * conv).astype(BF16), wc_ref[...], preferred_element_type=F32)
    y_a = jnp.dot(ya_ref[...], wa_ref[...], preferred_element_type=F32)
    y_x = jnp.dot(yx_ref[...], wx_ref[...], preferred_element_type=F32)
    merged = (jax.nn.sigmoid(gt_ref[:, 0:d].astype(F32)) * y_a
              + jax.nn.sigmoid(gt_ref[:, d:2 * d].astype(F32)) * y_c
              + jax.nn.sigmoid(gt_ref[:, 2 * d:3 * d].astype(F32)) * y_x)
    o_ref[...] = x_ref[...] + jnp.dot(merged.astype(BF16), wo_ref[...], preferred_element_type=F32)


def _merge(x2d, y_a, proj, conv_cols, y_x, gates, conv_w, w_a, w_c, w_x, w_o, s_len, tm):
    t, d = x2d.shape
    cwid = conv_w.shape[1]
    cu, cb, cc = conv_cols
    halo = BF16_SUBLANES
    rows_per_tile = tm // halo

    def halo_map(col):
        return lambda i: (jnp.maximum(i * rows_per_tile - 1, 0), col)

    full = lambda shape: pl.BlockSpec(shape, lambda i: (0, 0))
    return pl.pallas_call(
        functools.partial(_merge_body, tiles_per_seq=s_len // tm),
        out_shape=jax.ShapeDtypeStruct((t, d), F32),
        grid=(t // tm,),
        in_specs=[
            pl.BlockSpec((tm, d), lambda i: (i, 0)),
            pl.BlockSpec((tm, y_a.shape[1]), lambda i: (i, 0)),
            pl.BlockSpec((tm, cwid), lambda i: (i, cu)),
            pl.BlockSpec((tm, cwid), lambda i: (i, cb)),
            pl.BlockSpec((tm, cwid), lambda i: (i, cc)),
            pl.BlockSpec((halo, cwid), halo_map(cu)),
            pl.BlockSpec((halo, cwid), halo_map(cc)),
            pl.BlockSpec((tm, y_x.shape[1]), lambda i: (i, 0)),
            pl.BlockSpec((tm, gates.shape[1]), lambda i: (i, 0)),
            full(conv_w.shape), full(w_a.shape), full(w_c.shape), full(w_x.shape), full(w_o.shape),
        ],
        out_specs=pl.BlockSpec((tm, d), lambda i: (i, 0)),
        compiler_params=pltpu.CompilerParams(
            dimension_semantics=("parallel",), vmem_limit_bytes=VMEM_LIMIT),
        name="branch_merge",
    )(x2d, y_a, proj, proj, proj, proj, proj, y_x, gates, conv_w, w_a, w_c, w_x, w_o)


def _oddeven_merge_sort_pairs(n):
    pairs = []
    p = 1
    while p < n:
        k = p
        while k >= 1:
            for j in range(k % p, n - k, 2 * k):
                for i in range(min(k, n - j - k)):
                    if (i + j) // (2 * p) == (i + j + k) // (2 * p):
                        pairs.append((i + j, i + j + k))
            k //= 2
        p *= 2
    return pairs


def _bitonic_merge_pairs(n):
    pairs = []
    d = n // 2
    while d >= 1:
        pairs += [(i, i + d) for i in range(n) if i & d == 0]
        d //= 2
    return pairs


def _compare_exchange(slabs, pairs):
    for a, b in pairs:
        hi = jnp.maximum(slabs[a], slabs[b])
        lo = jnp.minimum(slabs[a], slabs[b])
        slabs[a], slabs[b] = hi, lo
    return slabs


def _top_sorted(slabs, k):
    n = len(slabs)
    slabs = _compare_exchange(list(slabs), _oddeven_merge_sort_pairs(n))
    merge = _bitonic_merge_pairs(k)
    for shift in (4, 2, 1):
        rolled = [pltpu.roll(s, shift, 0) for s in slabs]
        if len(slabs) < k:
            slabs = slabs + rolled[::-1]
        else:
            slabs = [jnp.maximum(slabs[i], rolled[k - 1 - i]) for i in range(k)]
        slabs = _compare_exchange(slabs, merge)
    return slabs


def _hyperbola(k):
    return [(i, j) for i in range(k) for j in range(k) if (i + 1) * (j + 1) <= k]


def _peer_scores(h, keys_ref, qry_ref, s2_ref, e1_ref, e2_ref, thr_ref, *, tt, lane_w):
    k = PEER_TOPK
    nkeys = keys_ref.shape[2]
    pairs = _hyperbola(k)
    n_slab = -(-len(pairs) // SUBLANES)
    n_slab = k // 2 if n_slab <= k // 2 else k
    sub = lax.broadcasted_iota(jnp.int32, (SUBLANES, lane_w), 0)
    inf = float("inf")
    for ts in range(tt // lane_w):
        cols = slice(ts * lane_w, (ts + 1) * lane_w)
        q1 = qry_ref[cols, pl.ds(pl.multiple_of(2 * h * nkeys, nkeys), nkeys)]
        q2 = qry_ref[cols, pl.ds(pl.multiple_of((2 * h + 1) * nkeys, nkeys), nkeys)]
        s1 = lax.dot_general(keys_ref[h, 0], q1, _NT, preferred_element_type=F32)
        s2 = lax.dot_general(keys_ref[h, 1], q2, _NT, preferred_element_type=F32)
        a = _top_sorted([s1[SUBLANES * v:SUBLANES * (v + 1)] for v in range(nkeys // SUBLANES)], k)
        b = _top_sorted([s2[SUBLANES * v:SUBLANES * (v + 1)] for v in range(nkeys // SUBLANES)], k)
        sums = {(i, j): a[i] + b[j] for (i, j) in pairs}
        packed = []
        for v in range(n_slab):
            slab = jnp.full((SUBLANES, lane_w), -inf, F32)
            for s in range(SUBLANES):
                idx = v * SUBLANES + s
                if idx < len(pairs):
                    slab = jnp.where(sub == s, sums[pairs[idx]], slab)
            packed.append(slab)
        tau = _top_sorted(packed, k)[k - 1]
        top = a[0] + b[0]
        z = jnp.zeros((SUBLANES, lane_w), F32)
        for slab in packed:
            z = z + jnp.where(slab >= tau, jnp.exp(slab - top), 0.0)
        zsum = jnp.sum(z, axis=0, keepdims=True)
        thr = jnp.full((nkeys, lane_w), inf, F32)
        for i in range(k - 1, -1, -1):
            t = jnp.full((SUBLANES, lane_w), inf, F32)
            for j in range(k):
                if (i, j) in sums:
                    t = jnp.minimum(t, jnp.where(sums[(i, j)] >= tau, b[j], inf))
            thr = jnp.where(s1 >= a[i][0:1, :], t[0:1, :], thr)
        thr_ref[h, ts] = thr
        s2_ref[h, ts] = s2
        e1_ref[h, ts] = jnp.exp(s1 - a[0][0:1, :])
        e2_ref[h, ts] = jnp.exp(s2 - b[0][0:1, :]) * (0.5 / zsum)


def _peer_weights(key_blk, key_off, col0, ncols, act_ref, p_ref, s2_ref, e1_ref, e2_ref, thr_ref, *, lane_w):
    n_heads, _, nkeys, _ = s2_ref.shape
    ec = act_ref.shape[0]
    key_rows = pl.ds(pl.multiple_of(key_blk * SUBLANES, SUBLANES), SUBLANES)

    def unit(ts, il):
        r = key_off + il
        rows = slice(il * nkeys, (il + 1) * nkeys)
        cols = slice(ts * lane_w, (ts + 1) * lane_w)
        w = jnp.zeros((nkeys, lane_w), F32)
        for h in range(n_heads):
            thr = thr_ref[h, ts, key_rows, :][r:r + 1, :]
            e1 = e1_ref[h, ts, key_rows, :][r:r + 1, :]
            w = w + jnp.where(s2_ref[h, ts] >= thr, e2_ref[h, ts], 0.0) * e1
        a = act_ref[rows, cols]
        gelu2 = a * (1.0 + lax.erf(a * (2.0 ** -0.5)))
        p_ref[rows, cols] = (w * gelu2).astype(BF16)

    return [functools.partial(unit, ts, il) for ts in range(col0 // lane_w, (col0 + ncols) // lane_w)
            for il in range(ec // nkeys)]


def _peer_body(h_ref, g_ref, wq_ref, keys_ref, u_ref, vt_ref, o_ref,
               xn_ref, qry_ref, s2_ref, e1_ref, e2_ref, thr_ref,
               act0_ref, act1_ref, p0_ref, p1_ref, acc_ref, *, lane_w, n_chunks):
    s = pl.program_id(1)
    tt = h_ref.shape[0]
    n_heads, _, nkeys, _ = keys_ref.shape
    ec, d = u_ref.shape
    acts, ps = (act0_ref, act1_ref), (p0_ref, p1_ref)
    keys_per_chunk = ec // nkeys
    assert 2 * keys_per_chunk == SUBLANES and n_chunks % 2 == 0

    def compute(par, do_act, do_gate, do_out):
        act_w, act_r, p_w, p_r = acts[par], acts[1 - par], ps[1 - par], ps[par]
        key_blk = (s - 1) // 2
        key_off = keys_per_chunk * (1 - par)
        group = 4 * lane_w
        kq = d // PEER_ACT_SLICES
        kv = ec // PEER_OUT_SLICES
        for c0 in range(0, tt, group):
            cols = slice(c0, c0 + group)
            units = []
            if do_gate:
                units = _peer_weights(key_blk, key_off, c0, group, act_r, p_w, s2_ref, e1_ref, e2_ref,
                                      thr_ref, lane_w=lane_w)

            def act_part(q):
                return lax.dot_general(u_ref[:, q * kq:(q + 1) * kq], xn_ref[cols, q * kq:(q + 1) * kq],
                                       _NT, preferred_element_type=F32)

            def out_part(q):
                return jnp.dot(vt_ref[0, :, q * kv:(q + 1) * kv], p_r[q * kv:(q + 1) * kv, cols],
                               preferred_element_type=F32)

            a_parts = [(act_part, q) for q in range(PEER_ACT_SLICES)] if do_act else []
            o_parts = [(out_part, q) for q in range(PEER_OUT_SLICES)] if do_out else []
            granules = []
            while a_parts or o_parts:
                granules += a_parts[:1] + o_parts[:1] + a_parts[1:2]
                a_parts, o_parts = a_parts[2:], o_parts[1:]
            sums = {act_part: None, out_part: None}
            done = 0
            for n, unit in enumerate(units + [None]):
                while done < len(granules) and done * len(units) <= n * len(granules):
                    fn, q = granules[done]
                    part = fn(q)
                    sums[fn] = part if sums[fn] is None else sums[fn] + part
                    done += 1
                if unit is not None:
                    unit()
            if do_act:
                act_w[:, cols] = sums[act_part]
            if do_out:
                acc_ref[:, cols] += sums[out_part]

    @pl.when(s == 0)
    def _():
        xn = _rms(h_ref[...], g_ref[...]).astype(BF16)
        xn_ref[...] = xn
        qry_ref[...] = jnp.dot(xn, wq_ref[...], preferred_element_type=F32).astype(BF16)

        def head(h, carry):
            _peer_scores(h, keys_ref, qry_ref, s2_ref, e1_ref, e2_ref, thr_ref, tt=tt, lane_w=lane_w)
            return carry

        lax.fori_loop(0, n_heads, head, 0)
        acc_ref[...] = jnp.zeros(acc_ref.shape, F32)
        compute(0, True, False, False)

    @pl.when(s == 1)
    def _():
        compute(1, True, True, False)

    @pl.when((s >= 2) & (s < n_chunks) & (s % 2 == 0))
    def _():
        compute(0, True, True, True)

    @pl.when((s >= 2) & (s < n_chunks) & (s % 2 == 1))
    def _():
        compute(1, True, True, True)

    @pl.when(s == n_chunks)
    def _():
        compute(0, False, True, True)

    @pl.when(s == n_chunks + 1)
    def _():
        compute(1, False, False, True)
        o_ref[...] = h_ref[...] + acc_ref[...].T


def _peer(h2d, g, w_query, keys, u, v, tt, ec):
    t, d = h2d.shape
    v_t = v.reshape(u.shape[0] // ec, ec, d).transpose(0, 2, 1)
    n_heads, _, nkeys, dk = keys.shape
    n_chunks = u.shape[0] // ec
    qw = w_query.shape[1]
    stat = lambda: pltpu.VMEM((n_heads, tt // LANES, nkeys, LANES), F32)
    return pl.pallas_call(
        functools.partial(_peer_body, lane_w=LANES, n_chunks=n_chunks),
        out_shape=jax.ShapeDtypeStruct((t, d), F32),
        grid=(t // tt, n_chunks + 2),
        in_specs=[
            pl.BlockSpec((tt, d), lambda i, s: (i, 0)),
            pl.BlockSpec((1, d), lambda i, s: (0, 0)),
            pl.BlockSpec((d, qw), lambda i, s: (0, 0)),
            pl.BlockSpec(keys.shape, lambda i, s: (0, 0, 0, 0)),
            pl.BlockSpec((ec, d), lambda i, s: (jnp.minimum(s, n_chunks - 1), 0)),
            pl.BlockSpec((1, d, ec), lambda i, s: (jnp.clip(s - 2, 0, n_chunks - 1), 0, 0)),
        ],
        out_specs=pl.BlockSpec((tt, d), lambda i, s: (i, 0)),
        scratch_shapes=[
            pltpu.VMEM((tt, d), BF16),
            pltpu.VMEM((tt, qw), BF16),
            stat(), stat(), stat(), stat(),
            pltpu.VMEM((ec, tt), F32), pltpu.VMEM((ec, tt), F32),
            pltpu.VMEM((ec, tt), BF16), pltpu.VMEM((ec, tt), BF16),
            pltpu.VMEM((d, tt), F32),
        ],
        compiler_params=pltpu.CompilerParams(
            dimension_semantics=("parallel", "arbitrary"), vmem_limit_bytes=VMEM_LIMIT),
        name="peer",
    )(h2d, g.reshape(1, d), w_query, keys, u, v_t)


def _layer(h2d, mem2d, batch, g_mix, w_in, q_gain_a, k_gain_a, bias_t, conv_w, g_mem, w_mem_kv,
           q_gain_x, k_gain_x, w_br_attn, w_br_conv, w_br_x, w_out, g_ffn,
           peer_w_query, peer_sub_keys, peer_u, peer_v):
    t, d = h2d.shape
    s_len = t // batch
    n_heads_a = bias_t.shape[0]
    dh_a = q_gain_a.shape[0]
    attn_w = n_heads_a * dh_a
    conv_wid = conv_w.shape[1]
    dh_x = q_gain_x.shape[0]
    xattn_w = w_mem_kv.shape[1] // 2
    assert attn_w == conv_wid == xattn_w, "column blocks of the projection must have equal width"
    branch_cols = 3 * attn_w + 3 * conv_wid + xattn_w

    w_in16 = w_in.astype(BF16)
    proj, gates = _norm_matmul(h2d, g_mix, [w_in16[:, :branch_cols], w_in16[:, branch_cols:]], BF16, 512,
                               "in_proj")
    proj3 = proj.reshape(batch, s_len, branch_cols)

    y_a = _moba(proj3, q_gain_a, k_gain_a, bias_t, n_heads_a, dh_a).reshape(t, attn_w)

    (kv,) = _norm_matmul(mem2d, g_mem, [w_mem_kv.astype(BF16)], F32, 512, "mem_kv")
    kv3 = kv.reshape(batch, mem2d.shape[0] // batch, kv.shape[1])
    y_x = _xattn(proj3, 6, kv3, q_gain_x, k_gain_x, dh_x).reshape(t, xattn_w)

    h2d = _merge(h2d, y_a, proj, (3, 4, 5), y_x, gates, conv_w,
                 w_br_attn.astype(BF16), w_br_conv.astype(BF16), w_br_x.astype(BF16),
                 w_out.astype(BF16), s_len, 512)

    return _peer(h2d, g_ffn, peer_w_query.astype(BF16), peer_sub_keys.astype(BF16),
                 peer_u.astype(BF16), peer_v.astype(BF16), 512, 512)


def kernel(x, mem, g_mix, w_in, q_gain_a, k_gain_a, rel_bias, conv_w, g_mem, w_mem_kv, q_gain_x, k_gain_x, w_br_attn, w_br_conv, w_br_x, w_out, g_ffn, peer_w_query, peer_sub_keys, peer_u, peer_v):
    batch, s_len, d = x.shape
    depth = g_mix.shape[0]
    bias_t = _bias_tiles(rel_bias, s_len // MOBA_BLOCK)
    h2d = x.reshape(batch * s_len, d)
    mem2d = mem.reshape(batch * mem.shape[1], d)
    for l in range(depth):
        h2d = _layer(h2d, mem2d, batch, g_mix[l], w_in[l], q_gain_a[l], k_gain_a[l], bias_t, conv_w[l],
                     g_mem[l], w_mem_kv[l], q_gain_x[l], k_gain_x[l], w_br_attn[l], w_br_conv[l],
                     w_br_x[l], w_out[l], g_ffn[l], peer_w_query[l], peer_sub_keys[l],
                     peer_u[l], peer_v[l])
    return h2d.reshape(batch, s_len, d)
```

```python
import functools
import math

import numpy as np
import jax
import jax.numpy as jnp
from jax import lax
from jax.experimental import pallas as pl
from jax.experimental.pallas import tpu as pltpu

F32 = jnp.float32
BF16 = jnp.bfloat16

RMS_EPS = 1e-6
NEG = -1e30
LOG2E = math.log2(math.e)

MOBA_BLOCK = 256
MOBA_TOPK = 3
REL_MAX_DIST = 1024
PEER_TOPK = 16
PEER_ACT_SLICES, PEER_OUT_SLICES = 4, 2

SUBLANES = 8
BF16_SUBLANES = 16
LANES = 128
VMEM_LIMIT = 56 * 1024 * 1024

_NT = (((1,), (1,)), ((), ()))


def _rms(x, g):
    r = lax.rsqrt(jnp.mean(x * x, axis=-1, keepdims=True) + RMS_EPS)
    return x * r * g


def _norm_matmul_body(x_ref, g_ref, *refs):
    n_out = len(refs) // 2
    xn = _rms(x_ref[...], g_ref[...]).astype(BF16)
    for w_ref, o_ref in zip(refs[:n_out], refs[n_out:]):
        o_ref[...] = jnp.dot(xn, w_ref[...], preferred_element_type=F32).astype(o_ref.dtype)


def _norm_matmul(x2d, g, weights, out_dtype, tm, name):
    t, d = x2d.shape
    return pl.pallas_call(
        _norm_matmul_body,
        out_shape=[jax.ShapeDtypeStruct((t, w.shape[1]), out_dtype) for w in weights],
        grid=(t // tm,),
        in_specs=[pl.BlockSpec((tm, d), lambda i: (i, 0)), pl.BlockSpec((1, d), lambda i: (0, 0))]
        + [pl.BlockSpec(w.shape, lambda i: (0, 0)) for w in weights],
        out_specs=[pl.BlockSpec((tm, w.shape[1]), lambda i: (i, 0)) for w in weights],
        compiler_params=pltpu.CompilerParams(
            dimension_semantics=("parallel",), vmem_limit_bytes=VMEM_LIMIT),
        name=name,
    )(x2d, g.reshape(1, d), *weights)


def _rel_bucket(dist, n_buckets):
    max_exact = n_buckets // 2
    d = jnp.maximum(dist, 0)
    df = jnp.maximum(d, 1).astype(F32)
    large = max_exact + (jnp.log(df / max_exact) / math.log(REL_MAX_DIST / max_exact)
                         * (n_buckets - max_exact)).astype(jnp.int32)
    large = jnp.minimum(large, n_buckets - 1)
    return jnp.where(d < max_exact, d, large)


def _bias_tiles_body(bucket_ref, rel_ref, o_ref, *, n_buckets, n_delta):
    h = pl.program_id(0)
    blk = MOBA_BLOCK
    for delta in range(n_delta):
        bk = bucket_ref[delta:delta + 1, :]
        w = jnp.zeros(bk.shape, F32)
        for b in range(n_buckets):
            w = jnp.where(bk == b, rel_ref[b, h], w)
        x = jnp.broadcast_to(w, (blk, 2 * blk))
        r = pltpu.roll(x, blk, 1, stride=1, stride_axis=0)
        tile = r[:, :blk]
        if delta == 0:
            krow = lax.broadcasted_iota(jnp.int32, (blk, blk), 0)
            qcol = lax.broadcasted_iota(jnp.int32, (blk, blk), 1)
            tile = jnp.where(qcol >= krow, tile, NEG)
        o_ref[0, delta] = tile * LOG2E


def _bias_tiles(rel_bias, n_delta):
    n_buckets, n_heads = rel_bias.shape
    blk = MOBA_BLOCK
    dist = (jnp.arange(n_delta, dtype=jnp.int32)[:, None] * blk - blk
            + jnp.arange(2 * blk, dtype=jnp.int32)[None, :])
    bucket = _rel_bucket(dist, n_buckets).astype(jnp.int32)
    return pl.pallas_call(
        functools.partial(_bias_tiles_body, n_buckets=n_buckets, n_delta=n_delta),
        out_shape=jax.ShapeDtypeStruct((n_heads, n_delta, blk, blk), F32),
        grid=(n_heads,),
        in_specs=[
            pl.BlockSpec((n_delta, 2 * blk), lambda h: (0, 0)),
            pl.BlockSpec(memory_space=pltpu.SMEM),
        ],
        out_specs=pl.BlockSpec((1, n_delta, blk, blk), lambda h: (h, 0, 0, 0)),
        compiler_params=pltpu.CompilerParams(dimension_semantics=("arbitrary",)),
        name="rel_bias_tiles",
    )(bucket, rel_bias)


def _moba_body(q_ref, k_ref, v_ref, qg_ref, kg_ref, bias_ref, o_ref, ot_ref, logit_ref, p_ref, *, dh):
    s_len = q_ref.shape[1]
    blk = MOBA_BLOCK
    nb = s_len // blk
    scale = dh ** -0.5
    heads_per_step = q_ref.shape[2] // dh
    q_all = q_ref[0].astype(F32)
    k_all = k_ref[0].astype(F32)
    vt_all = v_ref[0].astype(F32).T
    opaque0 = pl.multiple_of(jnp.minimum(pl.program_id(1), 0) * blk, blk)
    for hh in range(heads_per_step):
        sl = slice(hh * dh, (hh + 1) * dh)
        qn = _rms(q_all[:, sl], qg_ref[...])
        kn = _rms(k_all[:, sl], kg_ref[...])
        kmean = jnp.mean(kn.reshape(nb, blk, dh), axis=1)
        qb16 = qn.astype(BF16)
        qs16 = (qn * (scale * LOG2E)).astype(BF16)
        kb16 = kn.astype(BF16)
        vt = jnp.concatenate([vt_all[sl, :], jnp.ones((BF16_SUBLANES, s_len), F32)], axis=0).astype(BF16)
        gate_t = lax.dot_general(kmean.astype(BF16), qb16, _NT, preferred_element_type=F32)
        row = lax.broadcasted_iota(jnp.int32, (nb, blk), 0)
        for qb in range(nb):
            qs = slice(qb * blk, (qb + 1) * blk)
            qt = qs16[qs]
            negm = None
            if qb > MOBA_TOPK:
                g = gate_t[:, qs]
                rank = jnp.zeros((nb, blk), jnp.int32)
                for j in range(qb):
                    gj = g[j:j + 1, :]
                    beats = jnp.where(gj > g, 1, jnp.where((gj == g) & (j < row), 1, 0))
                    rank = rank + beats
                negm = jnp.where((rank < MOBA_TOPK) & (row < qb), 0.0, NEG)
            m = None
            for kb in range(qb + 1):
                ks = slice(kb * blk, (kb + 1) * blk)
                st = lax.dot_general(kb16[ks], qt, _NT, preferred_element_type=F32)
                st = st + bias_ref[hh, qb - kb]
                if negm is not None and kb < qb:
                    st = st + negm[kb:kb + 1, :]
                logit_ref[pl.ds(opaque0 + kb * blk, blk), :] = st
                mt = jnp.max(st, axis=0, keepdims=True)
                m = mt if m is None else jnp.maximum(m, mt)
            for kb in range(qb + 1):
                ks = slice(kb * blk, (kb + 1) * blk)
                p_ref[ks, :] = jnp.exp2(logit_ref[pl.ds(opaque0 + kb * blk, blk), :] - m).astype(BF16)
            nk = (qb + 1) * blk
            o_t = jnp.dot(vt[:, :nk], p_ref[0:nk, :], preferred_element_type=F32)
            ot_ref[sl, qs] = o_t[:dh] / o_t[dh:dh + 1]
    o_ref[0] = ot_ref[...].T.astype(o_ref.dtype)


def _moba(proj3, q_gain, k_gain, bias_t, n_heads, dh):
    b, s_len, _ = proj3.shape
    hps = LANES // dh
    width = n_heads * dh
    nblk = width // LANES
    n_delta = bias_t.shape[1]
    blk = MOBA_BLOCK
    return pl.pallas_call(
        functools.partial(_moba_body, dh=dh),
        out_shape=jax.ShapeDtypeStruct((b, s_len, width), BF16),
        grid=(n_heads // hps, b),
        in_specs=[
            pl.BlockSpec((1, s_len, LANES), lambda hp, bi: (bi, 0, hp)),
            pl.BlockSpec((1, s_len, LANES), lambda hp, bi: (bi, 0, nblk + hp)),
            pl.BlockSpec((1, s_len, LANES), lambda hp, bi: (bi, 0, 2 * nblk + hp)),
            pl.BlockSpec((1, dh), lambda hp, bi: (0, 0)),
            pl.BlockSpec((1, dh), lambda hp, bi: (0, 0)),
            pl.BlockSpec((hps, n_delta, blk, blk), lambda hp, bi: (hp, 0, 0, 0)),
        ],
        out_specs=pl.BlockSpec((1, s_len, LANES), lambda hp, bi: (bi, 0, hp)),
        scratch_shapes=[pltpu.VMEM((LANES, s_len), F32),
                        pltpu.VMEM((s_len, blk), F32),
                        pltpu.VMEM((s_len, blk), BF16)],
        compiler_params=pltpu.CompilerParams(
            dimension_semantics=("parallel", "arbitrary"), vmem_limit_bytes=VMEM_LIMIT),
        name="moba_attention",
    )(proj3, proj3, proj3, q_gain.reshape(1, dh), k_gain.reshape(1, dh), bias_t)


def _xattn_body(q_ref, kv_ref, qg_ref, kg_ref, o_ref, *, dh, chunk):
    s_len = q_ref.shape[1]
    width = q_ref.shape[2]
    n_heads = width // dh
    scale = dh ** -0.5
    for h in range(n_heads):
        sl = slice(h * dh, (h + 1) * dh)
        kn = _rms(kv_ref[0, :, sl], kg_ref[...]).astype(BF16)
        v = kv_ref[0, :, width + h * dh:width + (h + 1) * dh].astype(BF16)
        for c in range(s_len // chunk):
            rs = slice(c * chunk, (c + 1) * chunk)
            qn = _rms(q_ref[0, rs, sl].astype(F32), qg_ref[...]).astype(BF16)
            s = lax.dot_general(qn, kn, _NT, preferred_element_type=F32) * scale
            m = jnp.max(s, axis=-1, keepdims=True)
            p = jnp.exp(s - m)
            denom = jnp.sum(p, axis=-1, keepdims=True)
            o = jnp.dot(p.astype(BF16), v, preferred_element_type=F32) / denom
            o_ref[0, rs, sl] = o.astype(o_ref.dtype)


def _xattn(proj3, q_col_block, kv3, q_gain, k_gain, dh):
    b, s_len, _ = proj3.shape
    m_len, kv_width = kv3.shape[1], kv3.shape[2]
    width = kv_width // 2
    return pl.pallas_call(
        functools.partial(_xattn_body, dh=dh, chunk=512),
        out_shape=jax.ShapeDtypeStruct((b, s_len, width), BF16),
        grid=(b,),
        in_specs=[
            pl.BlockSpec((1, s_len, width), lambda bi: (bi, 0, q_col_block)),
            pl.BlockSpec((1, m_len, kv_width), lambda bi: (bi, 0, 0)),
            pl.BlockSpec((1, dh), lambda bi: (0, 0)),
            pl.BlockSpec((1, dh), lambda bi: (0, 0)),
        ],
        out_specs=pl.BlockSpec((1, s_len, width), lambda bi: (bi, 0, 0)),
        compiler_params=pltpu.CompilerParams(
            dimension_semantics=("parallel",), vmem_limit_bytes=VMEM_LIMIT),
        name="cross_attention",
    )(proj3, kv3, q_gain.reshape(1, dh), k_gain.reshape(1, dh))


def _merge_body(x_ref, ya_ref, u_ref, b_ref, c_ref, uh_ref, ch_ref, yx_ref, gt_ref, cw_ref,
                wa_ref, wc_ref, wx_ref, wo_ref, o_ref, *, tiles_per_seq):
    i = pl.program_id(0)
    tm, d = x_ref.shape
    halo = uh_ref.shape[0]
    z = c_ref[...].astype(F32) * u_ref[...].astype(F32)
    zh = ch_ref[...].astype(F32) * uh_ref[...].astype(F32)
    zh = jnp.where(i % tiles_per_seq == 0, 0.0, zh)
    row = lax.broadcasted_iota(jnp.int32, z.shape, 0)
    z1 = pltpu.roll(z, 1, 0)
    z1 = jnp.where(row == 0, zh[halo - 1:halo, :], z1)
    z2 = pltpu.roll(z, 2, 0)
    z2 = jnp.where(row == 0, zh[halo - 2:halo - 1, :], jnp.where(row == 1, zh[halo - 1:halo, :], z2))
    conv = cw_ref[0:1, :] * z2 + cw_ref[1:2, :] * z1 + cw_ref[2:3, :] * z
    y_c = jnp.dot((b_ref[...].astype(F32) * conv).astype(BF16), wc_ref[...], preferred_element_type=F32)
    y_a = jnp.dot(ya_ref[...], wa_ref[...], preferred_element_type=F32)
    y_x = jnp.dot(yx_ref[...], wx_ref[...], preferred_element_type=F32)
    merged = (jax.nn.sigmoid(gt_ref[:, 0:d].astype(F32)) * y_a
              + jax.nn.sigmoid(gt_ref[:, d:2 * d].astype(F32)) * y_c
              + jax.nn.sigmoid(gt_ref[:, 2 * d:3 * d].astype(F32)) * y_x)
    o_ref[...] = x_ref[...] + jnp.dot(merged.astype(BF16), wo_ref[...], preferred_element_type=F32)


def _merge(x2d, y_a, proj, conv_cols, y_x, gates, conv_w, w_a, w_c, w_x, w_o, s_len, tm):
    t, d = x2d.shape
    cwid = conv_w.shape[1]
    cu, cb, cc = conv_cols
    halo = BF16_SUBLANES
    rows_per_tile = tm // halo

    def halo_map(col):
        return lambda i: (jnp.maximum(i * rows_per_tile - 1, 0), col)

    full = lambda shape: pl.BlockSpec(shape, lambda i: (0, 0))
    return pl.pallas_call(
        functools.partial(_merge_body, tiles_per_seq=s_len // tm),
        out_shape=jax.ShapeDtypeStruct((t, d), F32),
        grid=(t // tm,),
        in_specs=[
            pl.BlockSpec((tm, d), lambda i: (i, 0)),
            pl.BlockSpec((tm, y_a.shape[1]), lambda i: (i, 0)),
            pl.BlockSpec((tm, cwid), lambda i: (i, cu)),
            pl.BlockSpec((tm, cwid), lambda i: (i, cb)),
            pl.BlockSpec((tm, cwid), lambda i: (i, cc)),
            pl.BlockSpec((halo, cwid), halo_map(cu)),
            pl.BlockSpec((halo, cwid), halo_map(cc)),
            pl.BlockSpec((tm, y_x.shape[1]), lambda i: (i, 0)),
            pl.BlockSpec((tm, gates.shape[1]), lambda i: (i, 0)),
            full(conv_w.shape), full(w_a.shape), full(w_c.shape), full(w_x.shape), full(w_o.shape),
        ],
        out_specs=pl.BlockSpec((tm, d), lambda i: (i, 0)),
        compiler_params=pltpu.CompilerParams(
            dimension_semantics=("parallel",), vmem_limit_bytes=VMEM_LIMIT),
        name="branch_merge",
    )(x2d, y_a, proj, proj, proj, proj, proj, y_x, gates, conv_w, w_a, w_c, w_x, w_o)


def _oddeven_merge_sort_pairs(n):
    pairs = []
    p = 1
    while p < n:
        k = p
        while k >= 1:
            for j in range(k % p, n - k, 2 * k):
                for i in range(min(k, n - j - k)):
                    if (i + j) // (2 * p) == (i + j + k) // (2 * p):
                        pairs.append((i + j, i + j + k))
            k //= 2
        p *= 2
    return pairs


def _bitonic_merge_pairs(n):
    pairs = []
    d = n // 2
    while d >= 1:
        pairs += [(i, i + d) for i in range(n) if i & d == 0]
        d //= 2
    return pairs


def _compare_exchange(slabs, pairs):
    for a, b in pairs:
        hi = jnp.maximum(slabs[a], slabs[b])
        lo = jnp.minimum(slabs[a], slabs[b])
        slabs[a], slabs[b] = hi, lo
    return slabs


def _top_sorted(slabs, k):
    n = len(slabs)
    slabs = _compare_exchange(list(slabs), _oddeven_merge_sort_pairs(n))
    merge = _bitonic_merge_pairs(k)
    for shift in (4, 2, 1):
        rolled = [pltpu.roll(s, shift, 0) for s in slabs]
        if len(slabs) < k:
            slabs = slabs + rolled[::-1]
        else:
            slabs = [jnp.maximum(slabs[i], rolled[k - 1 - i]) for i in range(k)]
        slabs = _compare_exchange(slabs, merge)
    return slabs


def _hyperbola(k):
    return [(i, j) for i in range(k) for j in range(k) if (i + 1) * (j + 1) <= k]


def _peer_scores(h, keys_ref, qry_ref, s2_ref, e1_ref, e2_ref, thr_ref, *, tt, lane_w):
    k = PEER_TOPK
    nkeys = keys_ref.shape[2]
    pairs = _hyperbola(k)
    n_slab = -(-len(pairs) // SUBLANES)
    n_slab = k // 2 if n_slab <= k // 2 else k
    sub = lax.broadcasted_iota(jnp.int32, (SUBLANES, lane_w), 0)
    inf = float("inf")
    for ts in range(tt // lane_w):
        cols = slice(ts * lane_w, (ts + 1) * lane_w)
        q1 = qry_ref[cols, pl.ds(pl.multiple_of(2 * h * nkeys, nkeys), nkeys)]
        q2 = qry_ref[cols, pl.ds(pl.multiple_of((2 * h + 1) * nkeys, nkeys), nkeys)]
        s1 = lax.dot_general(keys_ref[h, 0], q1, _NT, preferred_element_type=F32)
        s2 = lax.dot_general(keys_ref[h, 1], q2, _NT, preferred_element_type=F32)
        a = _top_sorted([s1[SUBLANES * v:SUBLANES * (v + 1)] for v in range(nkeys // SUBLANES)], k)
        b = _top_sorted([s2[SUBLANES * v:SUBLANES * (v + 1)] for v in range(nkeys // SUBLANES)], k)
        sums = {(i, j): a[i] + b[j] for (i, j) in pairs}
        packed = []
        for v in range(n_slab):
            slab = jnp.full((SUBLANES, lane_w), -inf, F32)
            for s in range(SUBLANES):
                idx = v * SUBLANES + s
                if idx < len(pairs):
                    slab = jnp.where(sub == s, sums[pairs[idx]], slab)
            packed.append(slab)
        tau = _top_sorted(packed, k)[k - 1]
        top = a[0] + b[0]
        z = jnp.zeros((SUBLANES, lane_w), F32)
        for slab in packed:
            z = z + jnp.where(slab >= tau, jnp.exp(slab - top), 0.0)
        zsum = jnp.sum(z, axis=0, keepdims=True)
        thr = jnp.full((nkeys, lane_w), inf, F32)
        for i in range(k - 1, -1, -1):
            t = jnp.full((SUBLANES, lane_w), inf, F32)
            for j in range(k):
                if (i, j) in sums:
                    t = jnp.minimum(t, jnp.where(sums[(i, j)] >= tau, b[j], inf))
            thr = jnp.where(s1 >= a[i][0:1, :], t[0:1, :], thr)
        thr_ref[h, ts] = thr
        s2_ref[h, ts] = s2
        e1_ref[h, ts] = jnp.exp(s1 - a[0][0:1, :])
        e2_ref[h, ts] = jnp.exp(s2 - b[0][0:1, :]) / zsum


def _peer_weights(key_blk, key_off, col0, ncols, act_ref, p_ref, s2_ref, e1_ref, e2_ref, thr_ref, *, lane_w):
    n_heads, _, nkeys, _ = s2_ref.shape
    ec = act_ref.shape[0]
    key_rows = pl.ds(pl.multiple_of(key_blk * SUBLANES, SUBLANES), SUBLANES)

    def unit(ts, il):
        r = key_off + il
        rows = slice(il * nkeys, (il + 1) * nkeys)
        cols = slice(ts * lane_w, (ts + 1) * lane_w)
        w = jnp.zeros((nkeys, lane_w), F32)
        for h in range(n_heads):
            thr = thr_ref[h, ts, key_rows, :][r:r + 1, :]
            e1 = e1_ref[h, ts, key_rows, :][r:r + 1, :]
            w = w + jnp.where(s2_ref[h, ts] >= thr, e2_ref[h, ts], 0.0) * e1
        a = act_ref[rows, cols]
        gelu = 0.5 * a * (1.0 + lax.erf(a * (2.0 ** -0.5)))
        p_ref[rows, cols] = (w * gelu).astype(BF16)

    return [functools.partial(unit, ts, il) for ts in range(col0 // lane_w, (col0 + ncols) // lane_w)
            for il in range(ec // nkeys)]


def _peer_body(h_ref, g_ref, wq_ref, keys_ref, u_ref, vt_ref, o_ref,
               xn_ref, qry_ref, s2_ref, e1_ref, e2_ref, thr_ref,
               act0_ref, act1_ref, p0_ref, p1_ref, acc_ref, *, lane_w, n_chunks):
    s = pl.program_id(1)
    tt = h_ref.shape[0]
    n_heads, _, nkeys, _ = keys_ref.shape
    ec, d = u_ref.shape
    acts, ps = (act0_ref, act1_ref), (p0_ref, p1_ref)
    keys_per_chunk = ec // nkeys
    assert 2 * keys_per_chunk == SUBLANES and n_chunks % 2 == 0

    def compute(par, do_act, do_gate, do_out):
        act_w, act_r, p_w, p_r = acts[par], acts[1 - par], ps[1 - par], ps[par]
        key_blk = (s - 1) // 2
        key_off = keys_per_chunk * (1 - par)
        group = 2 * lane_w
        kq = d // PEER_ACT_SLICES
        kv = ec // PEER_OUT_SLICES
        for c0 in range(0, tt, group):
            cols = slice(c0, c0 + group)
            units = []
            if do_gate:
                units = _peer_weights(key_blk, key_off, c0, group, act_r, p_w, s2_ref, e1_ref, e2_ref,
                                      thr_ref, lane_w=lane_w)

            def act_part(q):
                return lax.dot_general(u_ref[:, q * kq:(q + 1) * kq], xn_ref[cols, q * kq:(q + 1) * kq],
                                       _NT, preferred_element_type=F32)

            def out_part(q):
                return jnp.dot(vt_ref[0, :, q * kv:(q + 1) * kv], p_r[q * kv:(q + 1) * kv, cols],
                               preferred_element_type=F32)

            a_parts = [(act_part, q) for q in range(PEER_ACT_SLICES)] if do_act else []
            o_parts = [(out_part, q) for q in range(PEER_OUT_SLICES)] if do_out else []
            granules = []
            while a_parts or o_parts:
                granules += a_parts[:1] + o_parts[:1] + a_parts[1:2]
                a_parts, o_parts = a_parts[2:], o_parts[1:]
            sums = {act_part: None, out_part: None}
            done = 0
            for n, unit in enumerate(units + [None]):
                while done < len(granules) and done * len(units) <= n * len(granules):
                    fn, q = granules[done]
                    part = fn(q)
                    sums[fn] = part if sums[fn] is None else sums[fn] + part
                    done += 1
                if unit is not None:
                    unit()
            if do_act:
                act_w[:, cols] = sums[act_part]
            if do_out:
                acc_ref[:, cols] += sums[out_part]

    @pl.when(s == 0)
    def _():
        xn = _rms(h_ref[...], g_ref[...]).astype(BF16)
        xn_ref[...] = xn
        qry_ref[...] = jnp.dot(xn, wq_ref[...], preferred_element_type=F32).astype(BF16)

        def head(h, carry):
            _peer_scores(h, keys_ref, qry_ref, s2_ref, e1_ref, e2_ref, thr_ref, tt=tt, lane_w=lane_w)
            return carry

        lax.fori_loop(0, n_heads, head, 0)
        acc_ref[...] = jnp.zeros(acc_ref.shape, F32)
        compute(0, True, False, False)

    @pl.when(s == 1)
    def _():
        compute(1, True, True, False)

    @pl.when((s >= 2) & (s < n_chunks) & (s % 2 == 0))
    def _():
        compute(0, True, True, True)

    @pl.when((s >= 2) & (s < n_chunks) & (s % 2 == 1))
    def _():
        compute(1, True, True, True)

    @pl.when(s == n_chunks)
    def _():
        compute(0, False, True, True)

    @pl.when(s == n_chunks + 1)
    def _():
        compute(1, False, False, True)
        o_ref[...] = h_ref[...] + acc_ref[...].T


def _peer(h2d, g, w_query, keys, u, v, tt, ec):
    t, d = h2d.shape
    v_t = v.reshape(u.shape[0] // ec, ec, d).transpose(0, 2, 1).astype(BF16)
    n_heads, _, nkeys, dk = keys.shape
    n_chunks = u.shape[0] // ec
    qw = w_query.shape[1]
    stat = lambda: pltpu.VMEM((n_heads, tt // LANES, nkeys, LANES), F32)
    return pl.pallas_call(
        functools.partial(_peer_body, lane_w=LANES, n_chunks=n_chunks),
        out_shape=jax.ShapeDtypeStruct((t, d), F32),
        grid=(t // tt, n_chunks + 2),
        in_specs=[
            pl.BlockSpec((tt, d), lambda i, s: (i, 0)),
            pl.BlockSpec((1, d), lambda i, s: (0, 0)),
            pl.BlockSpec((d, qw), lambda i, s: (0, 0)),
            pl.BlockSpec(keys.shape, lambda i, s: (0, 0, 0, 0)),
            pl.BlockSpec((ec, d), lambda i, s: (jnp.minimum(s, n_chunks - 1), 0)),
            pl.BlockSpec((1, d, ec), lambda i, s: (jnp.clip(s - 2, 0, n_chunks - 1), 0, 0)),
        ],
        out_specs=pl.BlockSpec((tt, d), lambda i, s: (i, 0)),
        scratch_shapes=[
            pltpu.VMEM((tt, d), BF16),
            pltpu.VMEM((tt, qw), BF16),
            stat(), stat(), stat(), stat(),
            pltpu.VMEM((ec, tt), F32), pltpu.VMEM((ec, tt), F32),
            pltpu.VMEM((ec, tt), BF16), pltpu.VMEM((ec, tt), BF16),
            pltpu.VMEM((d, tt), F32),
        ],
        compiler_params=pltpu.CompilerParams(
            dimension_semantics=("parallel", "arbitrary"), vmem_limit_bytes=VMEM_LIMIT),
        name="peer",
    )(h2d, g.reshape(1, d), w_query, keys, u, v_t)


def _layer(h2d, mem2d, batch, g_mix, w_in, q_gain_a, k_gain_a, bias_t, conv_w, g_mem, w_mem_kv,
           q_gain_x, k_gain_x, w_br_attn, w_br_conv, w_br_x, w_out, g_ffn,
           peer_w_query, peer_sub_keys, peer_u, peer_v):
    t, d = h2d.shape
    s_len = t // batch
    n_heads_a = bias_t.shape[0]
    dh_a = q_gain_a.shape[0]
    attn_w = n_heads_a * dh_a
    conv_wid = conv_w.shape[1]
    dh_x = q_gain_x.shape[0]
    xattn_w = w_mem_kv.shape[1] // 2
    assert attn_w == conv_wid == xattn_w, "column blocks of the projection must have equal width"
    branch_cols = 3 * attn_w + 3 * conv_wid + xattn_w

    w_in16 = w_in.astype(BF16)
    proj, gates = _norm_matmul(h2d, g_mix, [w_in16[:, :branch_cols], w_in16[:, branch_cols:]], BF16, 512,
                               "in_proj")
    proj3 = proj.reshape(batch, s_len, branch_cols)

    y_a = _moba(proj3, q_gain_a, k_gain_a, bias_t, n_heads_a, dh_a).reshape(t, attn_w)

    (kv,) = _norm_matmul(mem2d, g_mem, [w_mem_kv.astype(BF16)], F32, 512, "mem_kv")
    kv3 = kv.reshape(batch, mem2d.shape[0] // batch, kv.shape[1])
    y_x = _xattn(proj3, 6, kv3, q_gain_x, k_gain_x, dh_x).reshape(t, xattn_w)

    h2d = _merge(h2d, y_a, proj, (3, 4, 5), y_x, gates, conv_w,
                 w_br_attn.astype(BF16), w_br_conv.astype(BF16), w_br_x.astype(BF16),
                 w_out.astype(BF16), s_len, 512)

    return _peer(h2d, g_ffn, peer_w_query.astype(BF16), peer_sub_keys.astype(BF16),
                 peer_u.astype(BF16), peer_v, 512, 512)


def kernel(x, mem, g_mix, w_in, q_gain_a, k_gain_a, rel_bias, conv_w, g_mem, w_mem_kv, q_gain_x, k_gain_x, w_br_attn, w_br_conv, w_br_x, w_out, g_ffn, peer_w_query, peer_sub_keys, peer_u, peer_v):
    batch, s_len, d = x.shape
    depth = g_mix.shape[0]
    bias_t = _bias_tiles(rel_bias, s_len // MOBA_BLOCK)
    h2d = x.reshape(batch * s_len, d)
    mem2d = mem.reshape(batch * mem.shape[1], d)
    for l in range(depth):
        h2d = _layer(h2d, mem2d, batch, g_mix[l], w_in[l], q_gain_a[l], k_gain_a[l], bias_t, conv_w[l],
                     g_mem[l], w_mem_kv[l], q_gain_x[l], k_gain_x[l], w_br_attn[l], w_br_conv[l],
                     w_br_x[l], w_out[l], g_ffn[l], peer_w_query[l], peer_sub_keys[l],
                     peer_u[l], peer_v[l])
    return h2d.reshape(batch, s_len, d)
```

```python
import functools
import math

import numpy as np
import jax
import jax.numpy as jnp
from jax import lax
from jax.experimental import pallas as pl
from jax.experimental.pallas import tpu as pltpu

F32 = jnp.float32
BF16 = jnp.bfloat16

RMS_EPS = 1e-6
NEG = -1e30
LOG2E = math.log2(math.e)

MOBA_BLOCK = 256
MOBA_TOPK = 3
REL_MAX_DIST = 1024
PEER_TOPK = 16
PEER_ACT_SLICES, PEER_OUT_SLICES = 4, 2

SUBLANES = 8
BF16_SUBLANES = 16
LANES = 128
VMEM_LIMIT = 56 * 1024 * 1024

_NT = (((1,), (1,)), ((), ()))


def _rms(x, g):
    r = lax.rsqrt(jnp.mean(x * x, axis=-1, keepdims=True) + RMS_EPS)
    return x * r * g


def _norm_matmul_body(x_ref, g_ref, *refs):
    n_out = len(refs) // 2
    xn = _rms(x_ref[...], g_ref[...]).astype(BF16)
    for w_ref, o_ref in zip(refs[:n_out], refs[n_out:]):
        o_ref[...] = jnp.dot(xn, w_ref[...], preferred_element_type=F32).astype(o_ref.dtype)


def _norm_matmul(x2d, g, weights, out_dtype, tm, name):
    t, d = x2d.shape
    return pl.pallas_call(
        _norm_matmul_body,
        out_shape=[jax.ShapeDtypeStruct((t, w.shape[1]), out_dtype) for w in weights],
        grid=(t // tm,),
        in_specs=[pl.BlockSpec((tm, d), lambda i: (i, 0)), pl.BlockSpec((1, d), lambda i: (0, 0))]
        + [pl.BlockSpec(w.shape, lambda i: (0, 0)) for w in weights],
        out_specs=[pl.BlockSpec((tm, w.shape[1]), lambda i: (i, 0)) for w in weights],
        compiler_params=pltpu.CompilerParams(
            dimension_semantics=("parallel",), vmem_limit_bytes=VMEM_LIMIT),
        name=name,
    )(x2d, g.reshape(1, d), *weights)


def _rel_bucket(dist, n_buckets):
    max_exact = n_buckets // 2
    d = jnp.maximum(dist, 0)
    df = jnp.maximum(d, 1).astype(F32)
    large = max_exact + (jnp.log(df / max_exact) / math.log(REL_MAX_DIST / max_exact)
                         * (n_buckets - max_exact)).astype(jnp.int32)
    large = jnp.minimum(large, n_buckets - 1)
    return jnp.where(d < max_exact, d, large)


def _bias_tiles_body(bucket_ref, rel_ref, o_ref, *, n_buckets, n_delta):
    h = pl.program_id(0)
    blk = MOBA_BLOCK
    for delta in range(n_delta):
        bk = bucket_ref[delta:delta + 1, :]
        w = jnp.zeros(bk.shape, F32)
        for b in range(n_buckets):
            w = jnp.where(bk == b, rel_ref[b, h], w)
        x = jnp.broadcast_to(w, (blk, 2 * blk))
        r = pltpu.roll(x, blk, 1, stride=1, stride_axis=0)
        tile = r[:, :blk]
        if delta == 0:
            krow = lax.broadcasted_iota(jnp.int32, (blk, blk), 0)
            qcol = lax.broadcasted_iota(jnp.int32, (blk, blk), 1)
            tile = jnp.where(qcol >= krow, tile, NEG)
        o_ref[0, delta] = tile * LOG2E


def _bias_tiles(rel_bias, n_delta):
    n_buckets, n_heads = rel_bias.shape
    blk = MOBA_BLOCK
    dist = (jnp.arange(n_delta, dtype=jnp.int32)[:, None] * blk - blk
            + jnp.arange(2 * blk, dtype=jnp.int32)[None, :])
    bucket = _rel_bucket(dist, n_buckets).astype(jnp.int32)
    return pl.pallas_call(
        functools.partial(_bias_tiles_body, n_buckets=n_buckets, n_delta=n_delta),
        out_shape=jax.ShapeDtypeStruct((n_heads, n_delta, blk, blk), F32),
        grid=(n_heads,),
        in_specs=[
            pl.BlockSpec((n_delta, 2 * blk), lambda h: (0, 0)),
            pl.BlockSpec(memory_space=pltpu.SMEM),
        ],
        out_specs=pl.BlockSpec((1, n_delta, blk, blk), lambda h: (h, 0, 0, 0)),
        compiler_params=pltpu.CompilerParams(dimension_semantics=("arbitrary",)),
        name="rel_bias_tiles",
    )(bucket, rel_bias)


def _moba_body(q_ref, k_ref, v_ref, qg_ref, kg_ref, bias_ref, o_ref, ot_ref, logit_ref, p_ref, *, dh):
    s_len = q_ref.shape[1]
    blk = MOBA_BLOCK
    nb = s_len // blk
    scale = dh ** -0.5
    heads_per_step = q_ref.shape[2] // dh
    q_all = q_ref[0].astype(F32)
    k_all = k_ref[0].astype(F32)
    vt_all = v_ref[0].astype(F32).T
    opaque0 = pl.multiple_of(jnp.minimum(pl.program_id(1), 0) * blk, blk)
    for hh in range(heads_per_step):
        sl = slice(hh * dh, (hh + 1) * dh)
        qn = _rms(q_all[:, sl], qg_ref[...])
        kn = _rms(k_all[:, sl], kg_ref[...])
        kmean = jnp.mean(kn.reshape(nb, blk, dh), axis=1)
        qb16 = qn.astype(BF16)
        qs16 = (qn * (scale * LOG2E)).astype(BF16)
        kb16 = kn.astype(BF16)
        vt = jnp.concatenate([vt_all[sl, :], jnp.ones((BF16_SUBLANES, s_len), F32)], axis=0).astype(BF16)
        gate_t = lax.dot_general(kmean.astype(BF16), qb16, _NT, preferred_element_type=F32)
        row = lax.broadcasted_iota(jnp.int32, (nb, blk), 0)
        for qb in range(nb):
            qs = slice(qb * blk, (qb + 1) * blk)
            qt = qs16[qs]
            negm = None
            if qb > MOBA_TOPK:
                g = gate_t[:, qs]
                rank = jnp.zeros((nb, blk), jnp.int32)
                for j in range(qb):
                    gj = g[j:j + 1, :]
                    beats = jnp.where(gj > g, 1, jnp.where((gj == g) & (j < row), 1, 0))
                    rank = rank + beats
                negm = jnp.where((rank < MOBA_TOPK) & (row < qb), 0.0, NEG)
            m = None
            for kb in range(qb + 1):
                ks = slice(kb * blk, (kb + 1) * blk)
                st = lax.dot_general(kb16[ks], qt, _NT, preferred_element_type=F32)
                st = st + bias_ref[hh, qb - kb]
                if negm is not None and kb < qb:
                    st = st + negm[kb:kb + 1, :]
                logit_ref[pl.ds(opaque0 + kb * blk, blk), :] = st
                mt = jnp.max(st, axis=0, keepdims=True)
                m = mt if m is None else jnp.maximum(m, mt)
            for kb in range(qb + 1):
                ks = slice(kb * blk, (kb + 1) * blk)
                p_ref[ks, :] = jnp.exp2(logit_ref[pl.ds(opaque0 + kb * blk, blk), :] - m).astype(BF16)
            nk = (qb + 1) * blk
            o_t = jnp.dot(vt[:, :nk], p_ref[0:nk, :], preferred_element_type=F32)
            ot_ref[sl, qs] = o_t[:dh] / o_t[dh:dh + 1]
    o_ref[0] = ot_ref[...].T.astype(o_ref.dtype)


def _moba(proj3, q_gain, k_gain, bias_t, n_heads, dh):
    b, s_len, _ = proj3.shape
    hps = LANES // dh
    width = n_heads * dh
    nblk = width // LANES
    n_delta = bias_t.shape[1]
    blk = MOBA_BLOCK
    return pl.pallas_call(
        functools.partial(_moba_body, dh=dh),
        out_shape=jax.ShapeDtypeStruct((b, s_len, width), BF16),
        grid=(n_heads // hps, b),
        in_specs=[
            pl.BlockSpec((1, s_len, LANES), lambda hp, bi: (bi, 0, hp)),
            pl.BlockSpec((1, s_len, LANES), lambda hp, bi: (bi, 0, nblk + hp)),
            pl.BlockSpec((1, s_len, LANES), lambda hp, bi: (bi, 0, 2 * nblk + hp)),
            pl.BlockSpec((1, dh), lambda hp, bi: (0, 0)),
            pl.BlockSpec((1, dh), lambda hp, bi: (0, 0)),
            pl.BlockSpec((hps, n_delta, blk, blk), lambda hp, bi: (hp, 0, 0, 0)),
        ],
        out_specs=pl.BlockSpec((1, s_len, LANES), lambda hp, bi: (bi, 0, hp)),
        scratch_shapes=[pltpu.VMEM((LANES, s_len), F32),
                        pltpu.VMEM((s_len, blk), F32),
                        pltpu.VMEM((s_len, blk), BF16)],
        compiler_params=pltpu.CompilerParams(
            dimension_semantics=("parallel", "arbitrary"), vmem_limit_bytes=VMEM_LIMIT),
        name="moba_attention",
    )(proj3, proj3, proj3, q_gain.reshape(1, dh), k_gain.reshape(1, dh), bias_t)


def _xattn_body(q_ref, kv_ref, qg_ref, kg_ref, o_ref, *, dh, chunk):
    s_len = q_ref.shape[1]
    width = q_ref.shape[2]
    n_heads = width // dh
    scale = dh ** -0.5
    for h in range(n_heads):
        sl = slice(h * dh, (h + 1) * dh)
        kn = _rms(kv_ref[0, :, sl], kg_ref[...]).astype(BF16)
        v = kv_ref[0, :, width + h * dh:width + (h + 1) * dh].astype(BF16)
        for c in range(s_len // chunk):
            rs = slice(c * chunk, (c + 1) * chunk)
            qn = _rms(q_ref[0, rs, sl].astype(F32), qg_ref[...]).astype(BF16)
            s = lax.dot_general(qn, kn, _NT, preferred_element_type=F32) * scale
            m = jnp.max(s, axis=-1, keepdims=True)
            p = jnp.exp(s - m)
            denom = jnp.sum(p, axis=-1, keepdims=True)
            o = jnp.dot(p.astype(BF16), v, preferred_element_type=F32) / denom
            o_ref[0, rs, sl] = o.astype(o_ref.dtype)


def _xattn(proj3, q_col_block, kv3, q_gain, k_gain, dh):
    b, s_len, _ = proj3.shape
    m_len, kv_width = kv3.shape[1], kv3.shape[2]
    width = kv_width // 2
    return pl.pallas_call(
        functools.partial(_xattn_body, dh=dh, chunk=512),
        out_shape=jax.ShapeDtypeStruct((b, s_len, width), BF16),
        grid=(b,),
        in_specs=[
            pl.BlockSpec((1, s_len, width), lambda bi: (bi, 0, q_col_block)),
            pl.BlockSpec((1, m_len, kv_width), lambda bi: (bi, 0, 0)),
            pl.BlockSpec((1, dh), lambda bi: (0, 0)),
            pl.BlockSpec((1, dh), lambda bi: (0, 0)),
        ],
        out_specs=pl.BlockSpec((1, s_len, width), lambda bi: (bi, 0, 0)),
        compiler_params=pltpu.CompilerParams(
            dimension_semantics=("parallel",), vmem_limit_bytes=VMEM_LIMIT),
        name="cross_attention",
    )(proj3, kv3, q_gain.reshape(1, dh), k_gain.reshape(1, dh))


def _merge_body(x_ref, ya_ref, u_ref, b_ref, c_ref, uh_ref, ch_ref, yx_ref, gt_ref, cw_ref,
                wa_ref, wc_ref, wx_ref, wo_ref, o_ref, *, tiles_per_seq):
    i = pl.program_id(0)
    tm, d = x_ref.shape
    halo = uh_ref.shape[0]
    z = c_ref[...].astype(F32) * u_ref[...].astype(F32)
    zh = ch_ref[...].astype(F32) * uh_ref[...].astype(F32)
    zh = jnp.where(i % tiles_per_seq == 0, 0.0, zh)
    row = lax.broadcasted_iota(jnp.int32, z.shape, 0)
    z1 = pltpu.roll(z, 1, 0)
    z1 = jnp.where(row == 0, zh[halo - 1:halo, :], z1)
    z2 = pltpu.roll(z, 2, 0)
    z2 = jnp.where(row == 0, zh[halo - 2:halo - 1, :], jnp.where(row == 1, zh[halo - 1:halo, :], z2))
    conv = cw_ref[0:1, :] * z2 + cw_ref[1:2, :] * z1 + cw_ref[2:3, :] * z
    y_c = jnp.dot((b_ref[...].astype(F32) * conv).astype(BF16), wc_ref[...], preferred_element_type=F32)
    y_a = jnp.dot(ya_ref[...], wa_ref[...], preferred_element_type=F32)
    y_x = jnp.dot(yx_ref[...], wx_ref[...], preferred_element_type=F32)
    merged = (jax.nn.sigmoid(gt_ref[:, 0:d].astype(F32)) * y_a
              + jax.nn.sigmoid(gt_ref[:, d:2 * d].astype(F32)) * y_c
              + jax.nn.sigmoid(gt_ref[:, 2 * d:3 * d].astype(F32)) * y_x)
    o_ref[...] = x_ref[...] + jnp.dot(merged.astype(BF16), wo_ref[...], preferred_element_type=F32)


def _merge(x2d, y_a, proj, conv_cols, y_x, gates, conv_w, w_a, w_c, w_x, w_o, s_len, tm):
    t, d = x2d.shape
    cwid = conv_w.shape[1]
    cu, cb, cc = conv_cols
    halo = BF16_SUBLANES
    rows_per_tile = tm // halo

    def halo_map(col):
        return lambda i: (jnp.maximum(i * rows_per_tile - 1, 0), col)

    full = lambda shape: pl.BlockSpec(shape, lambda i: (0, 0))
    return pl.pallas_call(
        functools.partial(_merge_body, tiles_per_seq=s_len // tm),
        out_shape=jax.ShapeDtypeStruct((t, d), F32),
        grid=(t // tm,),
        in_specs=[
            pl.BlockSpec((tm, d), lambda i: (i, 0)),
            pl.BlockSpec((tm, y_a.shape[1]), lambda i: (i, 0)),
            pl.BlockSpec((tm, cwid), lambda i: (i, cu)),
            pl.BlockSpec((tm, cwid), lambda i: (i, cb)),
            pl.BlockSpec((tm, cwid), lambda i: (i, cc)),
            pl.BlockSpec((halo, cwid), halo_map(cu)),
            pl.BlockSpec((halo, cwid), halo_map(cc)),
            pl.BlockSpec((tm, y_x.shape[1]), lambda i: (i, 0)),
            pl.BlockSpec((tm, gates.shape[1]), lambda i: (i, 0)),
            full(conv_w.shape), full(w_a.shape), full(w_c.shape), full(w_x.shape), full(w_o.shape),
        ],
        out_specs=pl.BlockSpec((tm, d), lambda i: (i, 0)),
        compiler_params=pltpu.CompilerParams(
            dimension_semantics=("parallel",), vmem_limit_bytes=VMEM_LIMIT),
        name="branch_merge",
    )(x2d, y_a, proj, proj, proj, proj, proj, y_x, gates, conv_w, w_a, w_c, w_x, w_o)


def _oddeven_merge_sort_pairs(n):
    pairs = []
    p = 1
    while p < n:
        k = p
        while k >= 1:
            for j in range(k % p, n - k, 2 * k):
                for i in range(min(k, n - j - k)):
                    if (i + j) // (2 * p) == (i + j + k) // (2 * p):
                        pairs.append((i + j, i + j + k))
            k //= 2
        p *= 2
    return pairs


def _bitonic_merge_pairs(n):
    pairs = []
    d = n // 2
    while d >= 1:
        pairs += [(i, i + d) for i in range(n) if i & d == 0]
        d //= 2
    return pairs


def _compare_exchange(slabs, pairs):
    for a, b in pairs:
        hi = jnp.maximum(slabs[a], slabs[b])
        lo = jnp.minimum(slabs[a], slabs[b])
        slabs[a], slabs[b] = hi, lo
    return slabs


def _top_sorted(slabs, k):
    n = len(slabs)
    slabs = _compare_exchange(list(slabs), _oddeven_merge_sort_pairs(n))
    merge = _bitonic_merge_pairs(k)
    for shift in (4, 2, 1):
        rolled = [pltpu.roll(s, shift, 0) for s in slabs]
        if len(slabs) < k:
            slabs = slabs + rolled[::-1]
        else:
            slabs = [jnp.maximum(slabs[i], rolled[k - 1 - i]) for i in range(k)]
        slabs = _compare_exchange(slabs, merge)
    return slabs


def _hyperbola(k):
    return [(i, j) for i in range(k) for j in range(k) if (i + 1) * (j + 1) <= k]


def _peer_scores(h, keys_ref, qry_ref, s2_ref, e1_ref, e2_ref, thr_ref, *, tt, lane_w):
    k = PEER_TOPK
    nkeys = keys_ref.shape[2]
    pairs = _hyperbola(k)
    n_slab = -(-len(pairs) // SUBLANES)
    n_slab = k // 2 if n_slab <= k // 2 else k
    sub = lax.broadcasted_iota(jnp.int32, (SUBLANES, lane_w), 0)
    inf = float("inf")
    for ts in range(tt // lane_w):
        cols = slice(ts * lane_w, (ts + 1) * lane_w)
        q1 = qry_ref[cols, pl.ds(pl.multiple_of(2 * h * nkeys, nkeys), nkeys)]
        q2 = qry_ref[cols, pl.ds(pl.multiple_of((2 * h + 1) * nkeys, nkeys), nkeys)]
        s1 = lax.dot_general(keys_ref[h, 0], q1, _NT, preferred_element_type=F32)
        s2 = lax.dot_general(keys_ref[h, 1], q2, _NT, preferred_element_type=F32)
        a = _top_sorted([s1[SUBLANES * v:SUBLANES * (v + 1)] for v in range(nkeys // SUBLANES)], k)
        b = _top_sorted([s2[SUBLANES * v:SUBLANES * (v + 1)] for v in range(nkeys // SUBLANES)], k)
        sums = {(i, j): a[i] + b[j] for (i, j) in pairs}
        packed = []
        for v in range(n_slab):
            slab = jnp.full((SUBLANES, lane_w), -inf, F32)
            for s in range(SUBLANES):
                idx = v * SUBLANES + s
                if idx < len(pairs):
                    slab = jnp.where(sub == s, sums[pairs[idx]], slab)
            packed.append(slab)
        tau = _top_sorted(packed, k)[k - 1]
        top = a[0] + b[0]
        z = jnp.zeros((SUBLANES, lane_w), F32)
        for slab in packed:
            z = z + jnp.where(slab >= tau, jnp.exp(slab - top), 0.0)
        zsum = jnp.sum(z, axis=0, keepdims=True)
        thr = jnp.full((nkeys, lane_w), inf, F32)
        for i in range(k - 1, -1, -1):
            t = jnp.full((SUBLANES, lane_w), inf, F32)
            for j in range(k):
                if (i, j) in sums:
                    t = jnp.minimum(t, jnp.where(sums[(i, j)] >= tau, b[j], inf))
            thr = jnp.where(s1 >= a[i][0:1, :], t[0:1, :], thr)
        thr_ref[h, ts] = thr
        s2_ref[h, ts] = s2
        e1_ref[h, ts] = jnp.exp(s1 - a[0][0:1, :])
        e2_ref[h, ts] = jnp.exp(s2 - b[0][0:1, :]) / zsum


def _peer_weights(key_blk, key_off, col0, ncols, act_ref, p_ref, s2_ref, e1_ref, e2_ref, thr_ref, *, lane_w):
    n_heads, _, nkeys, _ = s2_ref.shape
    ec = act_ref.shape[0]
    key_rows = pl.ds(pl.multiple_of(key_blk * SUBLANES, SUBLANES), SUBLANES)

    def unit(ts, il):
        r = key_off + il
        rows = slice(il * nkeys, (il + 1) * nkeys)
        cols = slice(ts * lane_w, (ts + 1) * lane_w)
        w = jnp.zeros((nkeys, lane_w), F32)
        for h in range(n_heads):
            thr = thr_ref[h, ts, key_rows, :][r:r + 1, :]
            e1 = e1_ref[h, ts, key_rows, :][r:r + 1, :]
            w = w + jnp.where(s2_ref[h, ts] >= thr, e2_ref[h, ts], 0.0) * e1
        a = act_ref[rows, cols]
        gelu = 0.5 * a * (1.0 + lax.erf(a * (2.0 ** -0.5)))
        p_ref[rows, cols] = (w * gelu).astype(BF16)

    return [functools.partial(unit, ts, il) for ts in range(col0 // lane_w, (col0 + ncols) // lane_w)
            for il in range(ec // nkeys)]


def _peer_body(h_ref, g_ref, wq_ref, keys_ref, u_ref, vt_ref, o_ref,
               xnt_ref, qry_ref, s2_ref, e1_ref, e2_ref, thr_ref,
               act0_ref, act1_ref, p0_ref, p1_ref, acc_ref, *, lane_w, n_chunks):
    s = pl.program_id(1)
    tt = h_ref.shape[0]
    n_heads, _, nkeys, _ = keys_ref.shape
    ec, d = u_ref.shape
    acts, ps = (act0_ref, act1_ref), (p0_ref, p1_ref)
    keys_per_chunk = ec // nkeys
    assert 2 * keys_per_chunk == SUBLANES and n_chunks % 2 == 0

    def compute(par, do_act, do_gate, do_out):
        act_w, act_r, p_w, p_r = acts[par], acts[1 - par], ps[1 - par], ps[par]
        key_blk = (s - 1) // 2
        key_off = keys_per_chunk * (1 - par)
        group = 2 * lane_w
        kq = d // PEER_ACT_SLICES
        kv = ec // PEER_OUT_SLICES
        for c0 in range(0, tt, group):
            cols = slice(c0, c0 + group)
            units = []
            if do_gate:
                units = _peer_weights(key_blk, key_off, c0, group, act_r, p_w, s2_ref, e1_ref, e2_ref,
                                      thr_ref, lane_w=lane_w)

            def act_part(q):
                return jnp.dot(u_ref[:, q * kq:(q + 1) * kq], xnt_ref[q * kq:(q + 1) * kq, cols],
                               preferred_element_type=F32)

            def out_part(q):
                return jnp.dot(vt_ref[0, :, q * kv:(q + 1) * kv], p_r[q * kv:(q + 1) * kv, cols],
                               preferred_element_type=F32)

            a_parts = [(act_part, q) for q in range(PEER_ACT_SLICES)] if do_act else []
            o_parts = [(out_part, q) for q in range(PEER_OUT_SLICES)] if do_out else []
            granules = []
            while a_parts or o_parts:
                granules += a_parts[:1] + o_parts[:1] + a_parts[1:2]
                a_parts, o_parts = a_parts[2:], o_parts[1:]
            sums = {act_part: None, out_part: None}
            done = 0
            for n, unit in enumerate(units + [None]):
                while done < len(granules) and done * len(units) <= n * len(granules):
                    fn, q = granules[done]
                    part = fn(q)
                    sums[fn] = part if sums[fn] is None else sums[fn] + part
                    done += 1
                if unit is not None:
                    unit()
            if do_act:
                act_w[:, cols] = sums[act_part]
            if do_out:
                acc_ref[:, cols] += sums[out_part]

    @pl.when(s == 0)
    def _():
        xn = _rms(h_ref[...], g_ref[...])
        xnt_ref[...] = xn.T.astype(BF16)
        qry_ref[...] = jnp.dot(xn.astype(BF16), wq_ref[...], preferred_element_type=F32).astype(BF16)

        def head(h, carry):
            _peer_scores(h, keys_ref, qry_ref, s2_ref, e1_ref, e2_ref, thr_ref, tt=tt, lane_w=lane_w)
            return carry

        lax.fori_loop(0, n_heads, head, 0)
        acc_ref[...] = jnp.zeros(acc_ref.shape, F32)
        compute(0, True, False, False)

    @pl.when(s == 1)
    def _():
        compute(1, True, True, False)

    @pl.when((s >= 2) & (s < n_chunks) & (s % 2 == 0))
    def _():
        compute(0, True, True, True)

    @pl.when((s >= 2) & (s < n_chunks) & (s % 2 == 1))
    def _():
        compute(1, True, True, True)

    @pl.when(s == n_chunks)
    def _():
        compute(0, False, True, True)

    @pl.when(s == n_chunks + 1)
    def _():
        compute(1, False, False, True)
        o_ref[...] = h_ref[...] + acc_ref[...].T


def _peer(h2d, g, w_query, keys, u, v, tt, ec):
    t, d = h2d.shape
    v_t = v.reshape(u.shape[0] // ec, ec, d).transpose(0, 2, 1).astype(BF16)
    n_heads, _, nkeys, dk = keys.shape
    n_chunks = u.shape[0] // ec
    qw = w_query.shape[1]
    stat = lambda: pltpu.VMEM((n_heads, tt // LANES, nkeys, LANES), F32)
    return pl.pallas_call(
        functools.partial(_peer_body, lane_w=LANES, n_chunks=n_chunks),
        out_shape=jax.ShapeDtypeStruct((t, d), F32),
        grid=(t // tt, n_chunks + 2),
        in_specs=[
            pl.BlockSpec((tt, d), lambda i, s: (i, 0)),
            pl.BlockSpec((1, d), lambda i, s: (0, 0)),
            pl.BlockSpec((d, qw), lambda i, s: (0, 0)),
            pl.BlockSpec(keys.shape, lambda i, s: (0, 0, 0, 0)),
            pl.BlockSpec((ec, d), lambda i, s: (jnp.minimum(s, n_chunks - 1), 0)),
            pl.BlockSpec((1, d, ec), lambda i, s: (jnp.clip(s - 2, 0, n_chunks - 1), 0, 0)),
        ],
        out_specs=pl.BlockSpec((tt, d), lambda i, s: (i, 0)),
        scratch_shapes=[
            pltpu.VMEM((d, tt), BF16),
            pltpu.VMEM((tt, qw), BF16),
            stat(), stat(), stat(), stat(),
            pltpu.VMEM((ec, tt), F32), pltpu.VMEM((ec, tt), F32),
            pltpu.VMEM((ec, tt), BF16), pltpu.VMEM((ec, tt), BF16),
            pltpu.VMEM((d, tt), F32),
        ],
        compiler_params=pltpu.CompilerParams(
            dimension_semantics=("parallel", "arbitrary"), vmem_limit_bytes=VMEM_LIMIT),
        name="peer",
    )(h2d, g.reshape(1, d), w_query, keys, u, v_t)


def _layer(h2d, mem2d, batch, g_mix, w_in, q_gain_a, k_gain_a, bias_t, conv_w, g_mem, w_mem_kv,
           q_gain_x, k_gain_x, w_br_attn, w_br_conv, w_br_x, w_out, g_ffn,
           peer_w_query, peer_sub_keys, peer_u, peer_v):
    t, d = h2d.shape
    s_len = t // batch
    n_heads_a = bias_t.shape[0]
    dh_a = q_gain_a.shape[0]
    attn_w = n_heads_a * dh_a
    conv_wid = conv_w.shape[1]
    dh_x = q_gain_x.shape[0]
    xattn_w = w_mem_kv.shape[1] // 2
    assert attn_w == conv_wid == xattn_w, "column blocks of the projection must have equal width"
    branch_cols = 3 * attn_w + 3 * conv_wid + xattn_w

    w_in16 = w_in.astype(BF16)
    proj, gates = _norm_matmul(h2d, g_mix, [w_in16[:, :branch_cols], w_in16[:, branch_cols:]], BF16, 512,
                               "in_proj")
    proj3 = proj.reshape(batch, s_len, branch_cols)

    y_a = _moba(proj3, q_gain_a, k_gain_a, bias_t, n_heads_a, dh_a).reshape(t, attn_w)

    (kv,) = _norm_matmul(mem2d, g_mem, [w_mem_kv.astype(BF16)], F32, 512, "mem_kv")
    kv3 = kv.reshape(batch, mem2d.shape[0] // batch, kv.shape[1])
    y_x = _xattn(proj3, 6, kv3, q_gain_x, k_gain_x, dh_x).reshape(t, xattn_w)

    h2d = _merge(h2d, y_a, proj, (3, 4, 5), y_x, gates, conv_w,
                 w_br_attn.astype(BF16), w_br_conv.astype(BF16), w_br_x.astype(BF16),
                 w_out.astype(BF16), s_len, 512)

    return _peer(h2d, g_ffn, peer_w_query.astype(BF16), peer_sub_keys.astype(BF16),
                 peer_u.astype(BF16), peer_v, 512, 512)


def kernel(x, mem, g_mix, w_in, q_gain_a, k_gain_a, rel_bias, conv_w, g_mem, w_mem_kv, q_gain_x, k_gain_x, w_br_attn, w_br_conv, w_br_x, w_out, g_ffn, peer_w_query, peer_sub_keys, peer_u, peer_v):
    batch, s_len, d = x.shape
    depth = g_mix.shape[0]
    bias_t = _bias_tiles(rel_bias, s_len // MOBA_BLOCK)
    h2d = x.reshape(batch * s_len, d)
    mem2d = mem.reshape(batch * mem.shape[1], d)
    for l in range(depth):
        h2d = _layer(h2d, mem2d, batch, g_mix[l], w_in[l], q_gain_a[l], k_gain_a[l], bias_t, conv_w[l],
                     g_mem[l], w_mem_kv[l], q_gain_x[l], k_gain_x[l], w_br_attn[l], w_br_conv[l],
                     w_br_x[l], w_out[l], g_ffn[l], peer_w_query[l], peer_sub_keys[l],
                     peer_u[l], peer_v[l])
    return h2d.reshape(batch, s_len, d)
```

```python
import functools
import math

import numpy as np
import jax
import jax.numpy as jnp
from jax import lax
from jax.experimental import pallas as pl
from jax.experimental.pallas import tpu as pltpu

F32 = jnp.float32
BF16 = jnp.bfloat16

RMS_EPS = 1e-6
NEG = -1e30
LOG2E = math.log2(math.e)

MOBA_BLOCK = 256
MOBA_TOPK = 3
REL_MAX_DIST = 1024
PEER_TOPK = 16
PEER_ACT_SLICES, PEER_OUT_SLICES = 4, 2

SUBLANES = 8
BF16_SUBLANES = 16
LANES = 128
VMEM_LIMIT = 56 * 1024 * 1024

_NT = (((1,), (1,)), ((), ()))


def _rms(x, g):
    r = lax.rsqrt(jnp.mean(x * x, axis=-1, keepdims=True) + RMS_EPS)
    return x * r * g


def _norm_matmul_body(x_ref, g_ref, *refs):
    n_out = len(refs) // 2
    xn = _rms(x_ref[...], g_ref[...]).astype(BF16)
    for w_ref, o_ref in zip(refs[:n_out], refs[n_out:]):
        o_ref[...] = jnp.dot(xn, w_ref[...], preferred_element_type=F32).astype(o_ref.dtype)


def _norm_matmul(x2d, g, weights, out_dtype, tm, name):
    t, d = x2d.shape
    return pl.pallas_call(
        _norm_matmul_body,
        out_shape=[jax.ShapeDtypeStruct((t, w.shape[1]), out_dtype) for w in weights],
        grid=(t // tm,),
        in_specs=[pl.BlockSpec((tm, d), lambda i: (i, 0)), pl.BlockSpec((1, d), lambda i: (0, 0))]
        + [pl.BlockSpec(w.shape, lambda i: (0, 0)) for w in weights],
        out_specs=[pl.BlockSpec((tm, w.shape[1]), lambda i: (i, 0)) for w in weights],
        compiler_params=pltpu.CompilerParams(
            dimension_semantics=("parallel",), vmem_limit_bytes=VMEM_LIMIT),
        name=name,
    )(x2d, g.reshape(1, d), *weights)


def _rel_bucket(dist, n_buckets):
    max_exact = n_buckets // 2
    d = jnp.maximum(dist, 0)
    df = jnp.maximum(d, 1).astype(F32)
    large = max_exact + (jnp.log(df / max_exact) / math.log(REL_MAX_DIST / max_exact)
                         * (n_buckets - max_exact)).astype(jnp.int32)
    large = jnp.minimum(large, n_buckets - 1)
    return jnp.where(d < max_exact, d, large)


def _bias_tiles_body(bucket_ref, rel_ref, o_ref, *, n_buckets, n_delta):
    h = pl.program_id(0)
    blk = MOBA_BLOCK
    for delta in range(n_delta):
        bk = bucket_ref[delta:delta + 1, :]
        w = jnp.zeros(bk.shape, F32)
        for b in range(n_buckets):
            w = jnp.where(bk == b, rel_ref[b, h], w)
        x = jnp.broadcast_to(w, (blk, 2 * blk))
        r = pltpu.roll(x, blk, 1, stride=1, stride_axis=0)
        tile = r[:, :blk]
        if delta == 0:
            krow = lax.broadcasted_iota(jnp.int32, (blk, blk), 0)
            qcol = lax.broadcasted_iota(jnp.int32, (blk, blk), 1)
            tile = jnp.where(qcol >= krow, tile, NEG)
        o_ref[0, delta] = tile * LOG2E


def _bias_tiles(rel_bias, n_delta):
    n_buckets, n_heads = rel_bias.shape
    blk = MOBA_BLOCK
    dist = (jnp.arange(n_delta, dtype=jnp.int32)[:, None] * blk - blk
            + jnp.arange(2 * blk, dtype=jnp.int32)[None, :])
    bucket = _rel_bucket(dist, n_buckets).astype(jnp.int32)
    return pl.pallas_call(
        functools.partial(_bias_tiles_body, n_buckets=n_buckets, n_delta=n_delta),
        out_shape=jax.ShapeDtypeStruct((n_heads, n_delta, blk, blk), F32),
        grid=(n_heads,),
        in_specs=[
            pl.BlockSpec((n_delta, 2 * blk), lambda h: (0, 0)),
            pl.BlockSpec(memory_space=pltpu.SMEM),
        ],
        out_specs=pl.BlockSpec((1, n_delta, blk, blk), lambda h: (h, 0, 0, 0)),
        compiler_params=pltpu.CompilerParams(dimension_semantics=("arbitrary",)),
        name="rel_bias_tiles",
    )(bucket, rel_bias)


def _moba_body(q_ref, k_ref, v_ref, qg_ref, kg_ref, bias_ref, o_ref, ot_ref, logit_ref, p_ref, *, dh):
    s_len = q_ref.shape[1]
    blk = MOBA_BLOCK
    nb = s_len // blk
    scale = dh ** -0.5
    heads_per_step = q_ref.shape[2] // dh
    q_all = q_ref[0].astype(F32)
    k_all = k_ref[0].astype(F32)
    vt_all = v_ref[0].astype(F32).T
    qnt_all = jnp.concatenate([_rms(q_all[:, hh * dh:(hh + 1) * dh], qg_ref[...])
                               for hh in range(heads_per_step)], axis=1).T
    opaque0 = pl.multiple_of(jnp.minimum(pl.program_id(1), 0) * blk, blk)
    for hh in range(heads_per_step):
        sl = slice(hh * dh, (hh + 1) * dh)
        qnt = qnt_all[sl, :]
        kn = _rms(k_all[:, sl], kg_ref[...])
        kmean = jnp.mean(kn.reshape(nb, blk, dh), axis=1)
        qs16 = (qnt * (scale * LOG2E)).astype(BF16)
        kb16 = kn.astype(BF16)
        vt = jnp.concatenate([vt_all[sl, :], jnp.ones((BF16_SUBLANES, s_len), F32)], axis=0).astype(BF16)
        gate_t = jnp.dot(kmean.astype(BF16), qnt.astype(BF16), preferred_element_type=F32)
        row = lax.broadcasted_iota(jnp.int32, (nb, blk), 0)
        for qb in range(nb):
            qs = slice(qb * blk, (qb + 1) * blk)
            qt = qs16[:, qs]
            negm = None
            if qb > MOBA_TOPK:
                g = gate_t[:, qs]
                rank = jnp.zeros((nb, blk), jnp.int32)
                for j in range(qb):
                    gj = g[j:j + 1, :]
                    beats = jnp.where(gj > g, 1, jnp.where((gj == g) & (j < row), 1, 0))
                    rank = rank + beats
                negm = jnp.where((rank < MOBA_TOPK) & (row < qb), 0.0, NEG)
            m = None
            for kb in range(qb + 1):
                ks = slice(kb * blk, (kb + 1) * blk)
                st = jnp.dot(kb16[ks], qt, preferred_element_type=F32)
                st = st + bias_ref[hh, qb - kb]
                if negm is not None and kb < qb:
                    st = st + negm[kb:kb + 1, :]
                logit_ref[pl.ds(opaque0 + kb * blk, blk), :] = st
                mt = jnp.max(st, axis=0, keepdims=True)
                m = mt if m is None else jnp.maximum(m, mt)
            for kb in range(qb + 1):
                ks = slice(kb * blk, (kb + 1) * blk)
                p_ref[ks, :] = jnp.exp2(logit_ref[pl.ds(opaque0 + kb * blk, blk), :] - m).astype(BF16)
            nk = (qb + 1) * blk
            o_t = jnp.dot(vt[:, :nk], p_ref[0:nk, :], preferred_element_type=F32)
            ot_ref[sl, qs] = o_t[:dh] / o_t[dh:dh + 1]
    o_ref[0] = ot_ref[...].T.astype(o_ref.dtype)


def _moba(proj3, q_gain, k_gain, bias_t, n_heads, dh):
    b, s_len, _ = proj3.shape
    hps = LANES // dh
    width = n_heads * dh
    nblk = width // LANES
    n_delta = bias_t.shape[1]
    blk = MOBA_BLOCK
    return pl.pallas_call(
        functools.partial(_moba_body, dh=dh),
        out_shape=jax.ShapeDtypeStruct((b, s_len, width), BF16),
        grid=(n_heads // hps, b),
        in_specs=[
            pl.BlockSpec((1, s_len, LANES), lambda hp, bi: (bi, 0, hp)),
            pl.BlockSpec((1, s_len, LANES), lambda hp, bi: (bi, 0, nblk + hp)),
            pl.BlockSpec((1, s_len, LANES), lambda hp, bi: (bi, 0, 2 * nblk + hp)),
            pl.BlockSpec((1, dh), lambda hp, bi: (0, 0)),
            pl.BlockSpec((1, dh), lambda hp, bi: (0, 0)),
            pl.BlockSpec((hps, n_delta, blk, blk), lambda hp, bi: (hp, 0, 0, 0)),
        ],
        out_specs=pl.BlockSpec((1, s_len, LANES), lambda hp, bi: (bi, 0, hp)),
        scratch_shapes=[pltpu.VMEM((LANES, s_len), F32),
                        pltpu.VMEM((s_len, blk), F32),
                        pltpu.VMEM((s_len, blk), BF16)],
        compiler_params=pltpu.CompilerParams(
            dimension_semantics=("parallel", "arbitrary"), vmem_limit_bytes=VMEM_LIMIT),
        name="moba_attention",
    )(proj3, proj3, proj3, q_gain.reshape(1, dh), k_gain.reshape(1, dh), bias_t)


def _xattn_body(q_ref, kv_ref, qg_ref, kg_ref, o_ref, *, dh, chunk):
    s_len = q_ref.shape[1]
    width = q_ref.shape[2]
    n_heads = width // dh
    scale = dh ** -0.5
    for h in range(n_heads):
        sl = slice(h * dh, (h + 1) * dh)
        knt = _rms(kv_ref[0, :, sl], kg_ref[...]).T.astype(BF16)
        v = kv_ref[0, :, width + h * dh:width + (h + 1) * dh].astype(BF16)
        for c in range(s_len // chunk):
            rs = slice(c * chunk, (c + 1) * chunk)
            qn = _rms(q_ref[0, rs, sl].astype(F32), qg_ref[...]).astype(BF16)
            s = jnp.dot(qn, knt, preferred_element_type=F32) * scale
            m = jnp.max(s, axis=-1, keepdims=True)
            p = jnp.exp(s - m)
            denom = jnp.sum(p, axis=-1, keepdims=True)
            o = jnp.dot(p.astype(BF16), v, preferred_element_type=F32) / denom
            o_ref[0, rs, sl] = o.astype(o_ref.dtype)


def _xattn(proj3, q_col_block, kv3, q_gain, k_gain, dh):
    b, s_len, _ = proj3.shape
    m_len, kv_width = kv3.shape[1], kv3.shape[2]
    width = kv_width // 2
    return pl.pallas_call(
        functools.partial(_xattn_body, dh=dh, chunk=512),
        out_shape=jax.ShapeDtypeStruct((b, s_len, width), BF16),
        grid=(b,),
        in_specs=[
            pl.BlockSpec((1, s_len, width), lambda bi: (bi, 0, q_col_block)),
            pl.BlockSpec((1, m_len, kv_width), lambda bi: (bi, 0, 0)),
            pl.BlockSpec((1, dh), lambda bi: (0, 0)),
            pl.BlockSpec((1, dh), lambda bi: (0, 0)),
        ],
        out_specs=pl.BlockSpec((1, s_len, width), lambda bi: (bi, 0, 0)),
        compiler_params=pltpu.CompilerParams(
            dimension_semantics=("parallel",), vmem_limit_bytes=VMEM_LIMIT),
        name="cross_attention",
    )(proj3, kv3, q_gain.reshape(1, dh), k_gain.reshape(1, dh))


def _merge_body(x_ref, ya_ref, u_ref, b_ref, c_ref, uh_ref, ch_ref, yx_ref, gt_ref, cw_ref,
                wa_ref, wc_ref, wx_ref, wo_ref, o_ref, *, tiles_per_seq):
    i = pl.program_id(0)
    tm, d = x_ref.shape
    halo = uh_ref.shape[0]
    z = c_ref[...].astype(F32) * u_ref[...].astype(F32)
    zh = ch_ref[...].astype(F32) * uh_ref[...].astype(F32)
    zh = jnp.where(i % tiles_per_seq == 0, 0.0, zh)
    row = lax.broadcasted_iota(jnp.int32, z.shape, 0)
    z1 = pltpu.roll(z, 1, 0)
    z1 = jnp.where(row == 0, zh[halo - 1:halo, :], z1)
    z2 = pltpu.roll(z, 2, 0)
    z2 = jnp.where(row == 0, zh[halo - 2:halo - 1, :], jnp.where(row == 1, zh[halo - 1:halo, :], z2))
    conv = cw_ref[0:1, :] * z2 + cw_ref[1:2, :] * z1 + cw_ref[2:3, :] * z
    y_c = jnp.dot((b_ref[...].astype(F32) * conv).astype(BF16), wc_ref[...], preferred_element_type=F32)
    y_a = jnp.dot(ya_ref[...], wa_ref[...], preferred_element_type=F32)
    y_x = jnp.dot(yx_ref[...], wx_ref[...], preferred_element_type=F32)
    merged = (jax.nn.sigmoid(gt_ref[:, 0:d].astype(F32)) * y_a
              + jax.nn.sigmoid(gt_ref[:, d:2 * d].astype(F32)) * y_c
              + jax.nn.sigmoid(gt_ref[:, 2 * d:3 * d].astype(F32)) * y_x)
    o_ref[...] = x_ref[...] + jnp.dot(merged.astype(BF16), wo_ref[...], preferred_element_type=F32)


def _merge(x2d, y_a, proj, conv_cols, y_x, gates, conv_w, w_a, w_c, w_x, w_o, s_len, tm):
    t, d = x2d.shape
    cwid = conv_w.shape[1]
    cu, cb, cc = conv_cols
    halo = BF16_SUBLANES
    rows_per_tile = tm // halo

    def halo_map(col):
        return lambda i: (jnp.maximum(i * rows_per_tile - 1, 0), col)

    full = lambda shape: pl.BlockSpec(shape, lambda i: (0, 0))
    return pl.pallas_call(
        functools.partial(_merge_body, tiles_per_seq=s_len // tm),
        out_shape=jax.ShapeDtypeStruct((t, d), F32),
        grid=(t // tm,),
        in_specs=[
            pl.BlockSpec((tm, d), lambda i: (i, 0)),
            pl.BlockSpec((tm, y_a.shape[1]), lambda i: (i, 0)),
            pl.BlockSpec((tm, cwid), lambda i: (i, cu)),
            pl.BlockSpec((tm, cwid), lambda i: (i, cb)),
            pl.BlockSpec((tm, cwid), lambda i: (i, cc)),
            pl.BlockSpec((halo, cwid), halo_map(cu)),
            pl.BlockSpec((halo, cwid), halo_map(cc)),
            pl.BlockSpec((tm, y_x.shape[1]), lambda i: (i, 0)),
            pl.BlockSpec((tm, gates.shape[1]), lambda i: (i, 0)),
            full(conv_w.shape), full(w_a.shape), full(w_c.shape), full(w_x.shape), full(w_o.shape),
        ],
        out_specs=pl.BlockSpec((tm, d), lambda i: (i, 0)),
        compiler_params=pltpu.CompilerParams(
            dimension_semantics=("parallel",), vmem_limit_bytes=VMEM_LIMIT),
        name="branch_merge",
    )(x2d, y_a, proj, proj, proj, proj, proj, y_x, gates, conv_w, w_a, w_c, w_x, w_o)


def _oddeven_merge_sort_pairs(n):
    pairs = []
    p = 1
    while p < n:
        k = p
        while k >= 1:
            for j in range(k % p, n - k, 2 * k):
                for i in range(min(k, n - j - k)):
                    if (i + j) // (2 * p) == (i + j + k) // (2 * p):
                        pairs.append((i + j, i + j + k))
            k //= 2
        p *= 2
    return pairs


def _bitonic_merge_pairs(n):
    pairs = []
    d = n // 2
    while d >= 1:
        pairs += [(i, i + d) for i in range(n) if i & d == 0]
        d //= 2
    return pairs


def _compare_exchange(slabs, pairs):
    for a, b in pairs:
        hi = jnp.maximum(slabs[a], slabs[b])
        lo = jnp.minimum(slabs[a], slabs[b])
        slabs[a], slabs[b] = hi, lo
    return slabs


def _top_sorted(slabs, k):
    n = len(slabs)
    slabs = _compare_exchange(list(slabs), _oddeven_merge_sort_pairs(n))
    merge = _bitonic_merge_pairs(k)
    for shift in (4, 2, 1):
        rolled = [pltpu.roll(s, shift, 0) for s in slabs]
        if len(slabs) < k:
            slabs = slabs + rolled[::-1]
        else:
            slabs = [jnp.maximum(slabs[i], rolled[k - 1 - i]) for i in range(k)]
        slabs = _compare_exchange(slabs, merge)
    return slabs


def _hyperbola(k):
    return [(i, j) for i in range(k) for j in range(k) if (i + 1) * (j + 1) <= k]


def _peer_scores(h, keys_ref, qry_ref, s2_ref, e1_ref, e2_ref, thr_ref, *, tt, lane_w):
    k = PEER_TOPK
    nkeys = keys_ref.shape[2]
    pairs = _hyperbola(k)
    n_slab = -(-len(pairs) // SUBLANES)
    n_slab = k // 2 if n_slab <= k // 2 else k
    sub = lax.broadcasted_iota(jnp.int32, (SUBLANES, lane_w), 0)
    inf = float("inf")
    for ts in range(tt // lane_w):
        cols = slice(ts * lane_w, (ts + 1) * lane_w)
        q1 = qry_ref[cols, pl.ds(pl.multiple_of(2 * h * nkeys, nkeys), nkeys)]
        q2 = qry_ref[cols, pl.ds(pl.multiple_of((2 * h + 1) * nkeys, nkeys), nkeys)]
        s1 = lax.dot_general(keys_ref[h, 0], q1, _NT, preferred_element_type=F32)
        s2 = lax.dot_general(keys_ref[h, 1], q2, _NT, preferred_element_type=F32)
        a = _top_sorted([s1[SUBLANES * v:SUBLANES * (v + 1)] for v in range(nkeys // SUBLANES)], k)
        b = _top_sorted([s2[SUBLANES * v:SUBLANES * (v + 1)] for v in range(nkeys // SUBLANES)], k)
        sums = {(i, j): a[i] + b[j] for (i, j) in pairs}
        packed = []
        for v in range(n_slab):
            slab = jnp.full((SUBLANES, lane_w), -inf, F32)
            for s in range(SUBLANES):
                idx = v * SUBLANES + s
                if idx < len(pairs):
                    slab = jnp.where(sub == s, sums[pairs[idx]], slab)
            packed.append(slab)
        tau = _top_sorted(packed, k)[k - 1]
        top = a[0] + b[0]
        z = jnp.zeros((SUBLANES, lane_w), F32)
        for slab in packed:
            z = z + jnp.where(slab >= tau, jnp.exp(slab - top), 0.0)
        zsum = jnp.sum(z, axis=0, keepdims=True)
        thr = jnp.full((nkeys, lane_w), inf, F32)
        for i in range(k - 1, -1, -1):
            t = jnp.full((SUBLANES, lane_w), inf, F32)
            for j in range(k):
                if (i, j) in sums:
                    t = jnp.minimum(t, jnp.where(sums[(i, j)] >= tau, b[j], inf))
            thr = jnp.where(s1 >= a[i][0:1, :], t[0:1, :], thr)
        thr_ref[h, ts] = thr
        s2_ref[h, ts] = s2
        e1_ref[h, ts] = jnp.exp(s1 - a[0][0:1, :])
        e2_ref[h, ts] = jnp.exp(s2 - b[0][0:1, :]) / zsum


def _peer_weights(key_blk, key_off, col0, ncols, act_ref, p_ref, s2_ref, e1_ref, e2_ref, thr_ref, *, lane_w):
    n_heads, _, nkeys, _ = s2_ref.shape
    ec = act_ref.shape[0]
    key_rows = pl.ds(pl.multiple_of(key_blk * SUBLANES, SUBLANES), SUBLANES)

    def unit(ts, il):
        r = key_off + il
        rows = slice(il * nkeys, (il + 1) * nkeys)
        cols = slice(ts * lane_w, (ts + 1) * lane_w)
        w = jnp.zeros((nkeys, lane_w), F32)
        for h in range(n_heads):
            thr = thr_ref[h, ts, key_rows, :][r:r + 1, :]
            e1 = e1_ref[h, ts, key_rows, :][r:r + 1, :]
            w = w + jnp.where(s2_ref[h, ts] >= thr, e2_ref[h, ts], 0.0) * e1
        a = act_ref[rows, cols]
        gelu = 0.5 * a * (1.0 + lax.erf(a * (2.0 ** -0.5)))
        p_ref[rows, cols] = (w * gelu).astype(BF16)

    return [functools.partial(unit, ts, il) for ts in range(col0 // lane_w, (col0 + ncols) // lane_w)
            for il in range(ec // nkeys)]


def _peer_body(h_ref, g_ref, wq_ref, keys_ref, u_ref, vt_ref, o_ref,
               xnt_ref, qry_ref, s2_ref, e1_ref, e2_ref, thr_ref,
               act0_ref, act1_ref, p0_ref, p1_ref, acc_ref, *, lane_w, n_chunks):
    s = pl.program_id(1)
    tt = h_ref.shape[0]
    n_heads, _, nkeys, _ = keys_ref.shape
    ec, d = u_ref.shape
    acts, ps = (act0_ref, act1_ref), (p0_ref, p1_ref)
    keys_per_chunk = ec // nkeys
    assert 2 * keys_per_chunk == SUBLANES and n_chunks % 2 == 0

    def compute(par, do_act, do_gate, do_out):
        act_w, act_r, p_w, p_r = acts[par], acts[1 - par], ps[1 - par], ps[par]
        key_blk = (s - 1) // 2
        key_off = keys_per_chunk * (1 - par)
        group = 2 * lane_w
        kq = d // PEER_ACT_SLICES
        kv = ec // PEER_OUT_SLICES
        for c0 in range(0, tt, group):
            cols = slice(c0, c0 + group)
            units = []
            if do_gate:
                units = _peer_weights(key_blk, key_off, c0, group, act_r, p_w, s2_ref, e1_ref, e2_ref,
                                      thr_ref, lane_w=lane_w)

            def act_part(q):
                return jnp.dot(u_ref[:, q * kq:(q + 1) * kq], xnt_ref[q * kq:(q + 1) * kq, cols],
                               preferred_element_type=F32)

            def out_part(q):
                return jnp.dot(vt_ref[0, :, q * kv:(q + 1) * kv], p_r[q * kv:(q + 1) * kv, cols],
                               preferred_element_type=F32)

            a_parts = [(act_part, q) for q in range(PEER_ACT_SLICES)] if do_act else []
            o_parts = [(out_part, q) for q in range(PEER_OUT_SLICES)] if do_out else []
            granules = []
            while a_parts or o_parts:
                granules += a_parts[:1] + o_parts[:1] + a_parts[1:2]
                a_parts, o_parts = a_parts[2:], o_parts[1:]
            sums = {act_part: None, out_part: None}
            done = 0
            for n, unit in enumerate(units + [None]):
                while done < len(granules) and done * len(units) <= n * len(granules):
                    fn, q = granules[done]
                    part = fn(q)
                    sums[fn] = part if sums[fn] is None else sums[fn] + part
                    done += 1
                if unit is not None:
                    unit()
            if do_act:
                act_w[:, cols] = sums[act_part]
            if do_out:
                acc_ref[:, cols] += sums[out_part]

    @pl.when(s == 0)
    def _():
        xn = _rms(h_ref[...], g_ref[...])
        xnt_ref[...] = xn.T.astype(BF16)
        qry_ref[...] = jnp.dot(xn.astype(BF16), wq_ref[...], preferred_element_type=F32).astype(BF16)

        def head(h, carry):
            _peer_scores(h, keys_ref, qry_ref, s2_ref, e1_ref, e2_ref, thr_ref, tt=tt, lane_w=lane_w)
            return carry

        lax.fori_loop(0, n_heads, head, 0)
        acc_ref[...] = jnp.zeros(acc_ref.shape, F32)
        compute(0, True, False, False)

    @pl.when(s == 1)
    def _():
        compute(1, True, True, False)

    @pl.when((s >= 2) & (s < n_chunks) & (s % 2 == 0))
    def _():
        compute(0, True, True, True)

    @pl.when((s >= 2) & (s < n_chunks) & (s % 2 == 1))
    def _():
        compute(1, True, True, True)

    @pl.when(s == n_chunks)
    def _():
        compute(0, False, True, True)

    @pl.when(s == n_chunks + 1)
    def _():
        compute(1, False, False, True)
        o_ref[...] = h_ref[...] + acc_ref[...].T


def _peer(h2d, g, w_query, keys, u, v, tt, ec):
    t, d = h2d.shape
    v_t = v.reshape(u.shape[0] // ec, ec, d).transpose(0, 2, 1).astype(BF16)
    n_heads, _, nkeys, dk = keys.shape
    n_chunks = u.shape[0] // ec
    qw = w_query.shape[1]
    stat = lambda: pltpu.VMEM((n_heads, tt // LANES, nkeys, LANES), F32)
    return pl.pallas_call(
        functools.partial(_peer_body, lane_w=LANES, n_chunks=n_chunks),
        out_shape=jax.ShapeDtypeStruct((t, d), F32),
        grid=(t // tt, n_chunks + 2),
        in_specs=[
            pl.BlockSpec((tt, d), lambda i, s: (i, 0)),
            pl.BlockSpec((1, d), lambda i, s: (0, 0)),
            pl.BlockSpec((d, qw), lambda i, s: (0, 0)),
            pl.BlockSpec(keys.shape, lambda i, s: (0, 0, 0, 0)),
            pl.BlockSpec((ec, d), lambda i, s: (jnp.minimum(s, n_chunks - 1), 0)),
            pl.BlockSpec((1, d, ec), lambda i, s: (jnp.clip(s - 2, 0, n_chunks - 1), 0, 0)),
        ],
        out_specs=pl.BlockSpec((tt, d), lambda i, s: (i, 0)),
        scratch_shapes=[
            pltpu.VMEM((d, tt), BF16),
            pltpu.VMEM((tt, qw), BF16),
            stat(), stat(), stat(), stat(),
            pltpu.VMEM((ec, tt), F32), pltpu.VMEM((ec, tt), F32),
            pltpu.VMEM((ec, tt), BF16), pltpu.VMEM((ec, tt), BF16),
            pltpu.VMEM((d, tt), F32),
        ],
        compiler_params=pltpu.CompilerParams(
            dimension_semantics=("parallel", "arbitrary"), vmem_limit_bytes=VMEM_LIMIT),
        name="peer",
    )(h2d, g.reshape(1, d), w_query, keys, u, v_t)


def _layer(h2d, mem2d, batch, g_mix, w_in, q_gain_a, k_gain_a, bias_t, conv_w, g_mem, w_mem_kv,
           q_gain_x, k_gain_x, w_br_attn, w_br_conv, w_br_x, w_out, g_ffn,
           peer_w_query, peer_sub_keys, peer_u, peer_v):
    t, d = h2d.shape
    s_len = t // batch
    n_heads_a = bias_t.shape[0]
    dh_a = q_gain_a.shape[0]
    attn_w = n_heads_a * dh_a
    conv_wid = conv_w.shape[1]
    dh_x = q_gain_x.shape[0]
    xattn_w = w_mem_kv.shape[1] // 2
    assert attn_w == conv_wid == xattn_w, "column blocks of the projection must have equal width"
    branch_cols = 3 * attn_w + 3 * conv_wid + xattn_w

    w_in16 = w_in.astype(BF16)
    proj, gates = _norm_matmul(h2d, g_mix, [w_in16[:, :branch_cols], w_in16[:, branch_cols:]], BF16, 512,
                               "in_proj")
    proj3 = proj.reshape(batch, s_len, branch_cols)

    y_a = _moba(proj3, q_gain_a, k_gain_a, bias_t, n_heads_a, dh_a).reshape(t, attn_w)

    (kv,) = _norm_matmul(mem2d, g_mem, [w_mem_kv.astype(BF16)], F32, 512, "mem_kv")
    kv3 = kv.reshape(batch, mem2d.shape[0] // batch, kv.shape[1])
    y_x = _xattn(proj3, 6, kv3, q_gain_x, k_gain_x, dh_x).reshape(t, xattn_w)

    h2d = _merge(h2d, y_a, proj, (3, 4, 5), y_x, gates, conv_w,
                 w_br_attn.astype(BF16), w_br_conv.astype(BF16), w_br_x.astype(BF16),
                 w_out.astype(BF16), s_len, 512)

    return _peer(h2d, g_ffn, peer_w_query.astype(BF16), peer_sub_keys.astype(BF16),
                 peer_u.astype(BF16), peer_v, 512, 512)


def kernel(x, mem, g_mix, w_in, q_gain_a, k_gain_a, rel_bias, conv_w, g_mem, w_mem_kv, q_gain_x, k_gain_x, w_br_attn, w_br_conv, w_br_x, w_out, g_ffn, peer_w_query, peer_sub_keys, peer_u, peer_v):
    batch, s_len, d = x.shape
    depth = g_mix.shape[0]
    bias_t = _bias_tiles(rel_bias, s_len // MOBA_BLOCK)
    h2d = x.reshape(batch * s_len, d)
    mem2d = mem.reshape(batch * mem.shape[1], d)
    for l in range(depth):
        h2d = _layer(h2d, mem2d, batch, g_mix[l], w_in[l], q_gain_a[l], k_gain_a[l], bias_t, conv_w[l],
                     g_mem[l], w_mem_kv[l], q_gain_x[l], k_gain_x[l], w_br_attn[l], w_br_conv[l],
                     w_br_x[l], w_out[l], g_ffn[l], peer_w_query[l], peer_sub_keys[l],
                     peer_u[l], peer_v[l])
    return h2d.reshape(batch, s_len, d)
```
